```python
import math
import jax, jax.numpy as jnp
from jax import lax
import numpy as np

D_MODEL = 1024
BATCH = 8
SEQ = 2048
DEPTH = 4


CTX_LEN = 256
GRID_W = 64
N_MIXERS = 4
N_MOD = 9
D_FF = 2816
EPS = 1e-6
NEG = -1e30

NA_HEADS = 16
NA_HD = D_MODEL // NA_HEADS
NA_WIN_R = 8
NA_WIN_C = 16
NA_QBLK_C = 16
NA_KBLK_C = NA_QBLK_C + NA_WIN_C

ML_HEADS = 4
ML_HD = D_MODEL // ML_HEADS
ML_CONV = 3
ML_CHUNK = 64

DA_HEADS = 8
DA_HD = D_MODEL // DA_HEADS // 2
DA_QBLK = 128
ROPE_BASE = 10000.0

GLA_HEADS = 4
GLA_DK = D_MODEL // 2 // GLA_HEADS
GLA_DV = D_MODEL // GLA_HEADS
GLA_RANK = 16
GLA_TAU = 16.0
GLA_CHUNK = 64

LAYERS_PER_MIXER = tuple((DEPTH - m + N_MIXERS - 1) // N_MIXERS for m in range(N_MIXERS))

kernel_name = 'hybrid_interleaved_diffusion_block'


def rmsnorm(x, g):
    xf = x.astype(jnp.float32)
    y = xf * lax.rsqrt(jnp.mean(xf * xf, axis=-1, keepdims=True) + EPS)
    return (y * g.astype(jnp.float32)).astype(x.dtype)


def head_rmsnorm(o, g):
    of = o.astype(jnp.float32)
    y = of * lax.rsqrt(jnp.mean(of * of, axis=-1, keepdims=True) + EPS)
    return (y * g.reshape(o.shape[-2:]).astype(jnp.float32)).astype(o.dtype)


def modulate(x, shift, scale):
    return x * (1.0 + scale) + shift


def swiglu(x, w13, w2):
    gate, up = jnp.split(x @ w13, 2, axis=-1)
    return (jax.nn.silu(gate) * up) @ w2


def ffn_branch(h, g, shift, scale, gate, w13, w2):
    return 0.5 * gate * swiglu(modulate(rmsnorm(h, g), shift, scale), w13, w2)


def dwconv_centred(x, w):
    K, C = w.shape
    return lax.conv_general_dilated(x, w[:, None, :].astype(x.dtype), (1,), [(K // 2, K // 2)],
                                    dimension_numbers=('NWC', 'WIO', 'NWC'), feature_group_count=C)


def axial_rope_tables(n, dim):
    t = jnp.arange(n)
    row = (t // GRID_W).astype(jnp.float32)
    col = (t % GRID_W).astype(jnp.float32)
    per_axis = dim // 2
    freqs = ROPE_BASE ** (-jnp.arange(0, per_axis, 2, dtype=jnp.float32) / per_axis)
    ar = row[:, None] * freqs
    ac = col[:, None] * freqs
    ang = jnp.concatenate([ar, ar, ac, ac], axis=-1)
    return jnp.cos(ang), jnp.sin(ang)


def apply_axial_rope(x, cos, sin):
    x1, x2, x3, x4 = jnp.split(x, 4, axis=-1)
    rot = jnp.concatenate([-x2, x1, -x4, x3], axis=-1)
    return (x * cos + rot * sin).astype(x.dtype)


def softmax_attend(q, k, v):
    s = jnp.einsum('bthd,bshd->bhts', q, k).astype(jnp.float32)
    p = jax.nn.softmax(s, axis=-1).astype(v.dtype)
    return jnp.einsum('bhts,bshd->bthd', p, v)


def to_chunks(t, size):
    nc = t.shape[2] // size
    return jnp.moveaxis(t.reshape(t.shape[:2] + (nc, size) + t.shape[3:]), 2, 0)


def from_chunks(t):
    t = jnp.moveaxis(t, 0, 2)
    return t.reshape(t.shape[0], t.shape[1], -1, t.shape[-1])


def mlstm_scan(q, k, v, ig, lf, state):
    tril = np.tril(np.ones((ML_CHUNK, ML_CHUNK), bool))

    def step(carry, inp):
        C, n, m = carry
        qc, kc, vc, ic, fc = inp
        b = jnp.cumsum(fc, axis=-1)
        dmat = jnp.where(tril, b[..., :, None] - b[..., None, :] + ic[..., None, :], NEG)
        inter = b + m[..., None]
        m_t = jnp.maximum(inter, jnp.max(dmat, axis=-1))
        w_inter = jnp.exp(inter - m_t)
        s = jnp.einsum('bhtd,bhsd->bhts', qc, kc) * jnp.exp(dmat - m_t[..., None])
        num = w_inter[..., None] * jnp.einsum('bhtd,bhde->bhte', qc, C) + jnp.einsum('bhts,bhse->bhte', s, vc)
        den = w_inter * jnp.einsum('bhtd,bhd->bht', qc, n) + jnp.sum(s, axis=-1)
        h = num / jnp.maximum(jnp.abs(den), jnp.exp(-m_t))[..., None]
        g = b[..., -1:] - b + ic
        m_new = jnp.maximum(b[..., -1] + m, jnp.max(g, axis=-1))
        w_old = jnp.exp(b[..., -1] + m - m_new)
        w_s = jnp.exp(g - m_new[..., None])
        C_new = w_old[..., None, None] * C + jnp.einsum('bhs,bhsd,bhse->bhde', w_s, kc, vc)
        n_new = w_old[..., None] * n + jnp.einsum('bhs,bhsd->bhd', w_s, kc)
        return (C_new, n_new, m_new), h

    final, h = lax.scan(step, state, (to_chunks(q, ML_CHUNK), to_chunks(k, ML_CHUNK), to_chunks(v, ML_CHUNK),
                                      to_chunks(ig, ML_CHUNK), to_chunks(lf, ML_CHUNK)))
    return from_chunks(h), final


def mlstm_final_state(_q, k, v, ig, lf):
    b = jnp.cumsum(lf, axis=-1)
    g = b[..., -1:] - b + ig
    m = jnp.max(g, axis=-1)
    w = jnp.exp(g - m[..., None])
    return (jnp.einsum('bhs,bhsd,bhse->bhde', w, k, v), jnp.einsum('bhs,bhsd->bhd', w, k), m)


def gla_scan(q, k, v, la, s0):
    tril = np.tril(np.ones((GLA_CHUNK, GLA_CHUNK), bool))[..., None]

    def step(s, inp):
        qc, kc, vc, lc = inp
        bc = jnp.cumsum(lc, axis=2)
        decay = jnp.exp(jnp.where(tril, bc[:, :, :, None, :] - bc[:, :, None, :, :], NEG))
        att = jnp.einsum('bhtd,bhsd,bhtsd->bhts', qc, kc, decay)
        o = jnp.einsum('bhts,bhse->bhte', att, vc) + jnp.einsum('bhtd,bhde->bhte', qc * jnp.exp(bc), s)
        b_end = bc[:, :, -1:, :]
        s_new = jnp.exp(b_end[:, :, 0, :, None]) * s + jnp.einsum('bhsd,bhse->bhde', kc * jnp.exp(b_end - bc), vc)
        return s_new, o

    s_fin, o = lax.scan(step, s0, (to_chunks(q, GLA_CHUNK), to_chunks(k, GLA_CHUNK), to_chunks(v, GLA_CHUNK),
                                   to_chunks(la, GLA_CHUNK)))
    return from_chunks(o), s_fin


def gla_final_state(_q, k, v, la):
    b = jnp.cumsum(la, axis=2)
    return jnp.einsum('bhsd,bhse->bhde', k * jnp.exp(b[:, :, -1:] - b), v)


def prefix_bidir(scan_fn, final_fn, lat_dirs, ctx_dirs, state0, need_ctx):
    h_lat, h_ctx = None, None
    for reverse, lat, ctx in zip((False, True), lat_dirs, ctx_dirs):
        flip = (lambda t: None if t is None else jnp.flip(t, 2)) if reverse else (lambda t: t)
        lat = [flip(t) for t in lat]
        ctx = [flip(t) for t in ctx]
        if need_ctx:
            hc, state = scan_fn(*ctx, state0)
            hc = flip(hc)
            h_ctx = hc if h_ctx is None else h_ctx + hc
        else:
            state = final_fn(*ctx)
        hl, _ = scan_fn(*lat, state)
        hl = flip(hl)
        h_lat = hl if h_lat is None else h_lat + hl
    return h_lat, h_ctx


def na_mixer(a, ac, w_kvq, rpb, w_o, need_ctx):
    B, N, D = a.shape
    L = ac.shape[1]
    H, d = NA_HEADS, NA_HD
    rows = N // GRID_W
    kr = min(NA_WIN_R, rows)
    k, v, q = jnp.split(a @ w_kvq, 3, axis=-1)
    kg = k.reshape(B, rows, GRID_W, H, d)
    vg = v.reshape(B, rows, GRID_W, H, d)
    qg = (q * d ** -0.5).reshape(B, rows, GRID_W, H, d)
    pc = ac @ (w_kvq if need_ctx else w_kvq[:, :2 * D])
    kc = pc[..., :D].reshape(B, L, H, d)
    vc = pc[..., D:2 * D].reshape(B, L, H, d)
    n_cb = GRID_W // NA_QBLK_C
    qcol = np.arange(GRID_W).reshape(n_cb, NA_QBLK_C)
    win0 = np.clip(qcol - NA_WIN_C // 2, 0, GRID_W - NA_WIN_C)
    kcol = np.clip(win0[:, :1], 0, GRID_W - NA_KBLK_C) + np.arange(NA_KBLK_C)
    in_win = (kcol[:, None, :] >= win0[:, :, None]) & (kcol[:, None, :] < win0[:, :, None] + NA_WIN_C)
    col_off = np.clip(kcol[:, None, :] - qcol[:, :, None], 1 - NA_WIN_C, NA_WIN_C - 1) + NA_WIN_C - 1
    rpb_f = rpb.astype(jnp.float32)
    n_loc = kr * NA_KBLK_C

    def one_row(r):
        r0 = jnp.clip(r - kr // 2, 0, rows - kr)
        k_blk = lax.dynamic_slice_in_dim(kg, r0, kr, axis=1)[:, :, kcol]
        v_blk = lax.dynamic_slice_in_dim(vg, r0, kr, axis=1)[:, :, kcol]
        q_row = lax.dynamic_index_in_dim(qg, r, axis=1, keepdims=False).reshape(B, n_cb, NA_QBLK_C, H, d)
        row_off = r0 + jnp.arange(kr) - r + NA_WIN_R - 1
        bias = rpb_f[:, row_off[None, None, :, None], col_off[:, :, None, :]]
        s_loc = jnp.einsum('bjqhd,brjkhd->bhjqrk', q_row, k_blk).astype(jnp.float32) + bias[None]
        s_loc = jnp.where(in_win[None, None, :, :, None, :], s_loc, NEG)
        s_ctx = jnp.einsum('bjqhd,blhd->bhjql', q_row, kc).astype(jnp.float32)
        s = jnp.concatenate([s_loc.reshape(B, H, n_cb, NA_QBLK_C, n_loc), s_ctx], axis=-1)
        p = jax.nn.softmax(s, axis=-1).astype(v.dtype)
        p_loc = p[..., :n_loc].reshape(B, H, n_cb, NA_QBLK_C, kr, NA_KBLK_C)
        o = jnp.einsum('bhjqrk,brjkhd->bjqhd', p_loc, v_blk) + jnp.einsum('bhjql,blhd->bjqhd', p[..., n_loc:], vc)
        return o.reshape(B, GRID_W, D)

    o = lax.map(one_row, jnp.arange(rows))
    y = jnp.transpose(o, (1, 0, 2, 3)).reshape(B, N, D) @ w_o
    yc = None
    if need_ctx:
        qc = (pc[..., 2 * D:] * d ** -0.5).reshape(B, L, H, d)
        yc = softmax_attend(qc, kc, vc).reshape(B, L, D) @ w_o
    return y, yc


def mlstm_mixer(a, ac, w_in, gate_b, conv_w, norm_g, w_o, need_ctx):
    B, N, D = a.shape
    H = ML_HEADS
    n_kvg = 2 * D + 4 * H

    def heads(t):
        return jnp.transpose(t.reshape(t.shape[0], t.shape[1], H, -1), (0, 2, 1, 3)).astype(jnp.float32)

    def prep(p, with_q):
        k = heads(jax.nn.silu(dwconv_centred(p[..., :D], conv_w[:, :D]))) * ML_HD ** -0.5
        v = heads(p[..., D:2 * D])
        g = jnp.transpose((p[..., 2 * D:n_kvg] + gate_b).astype(jnp.float32), (0, 2, 1))
        q = heads(jax.nn.silu(dwconv_centred(p[..., n_kvg:n_kvg + D], conv_w[:, D:]))) if with_q else None
        fwd = (q, k, v, g[:, :H], jax.nn.log_sigmoid(g[:, H:2 * H]))
        bwd = (q, k, v, g[:, 2 * H:3 * H], jax.nn.log_sigmoid(g[:, 3 * H:]))
        return fwd, bwd

    p = a @ w_in
    pc = ac @ (w_in if need_ctx else w_in[:, :n_kvg])
    zero = (jnp.zeros((B, H, ML_HD, ML_HD), jnp.float32), jnp.zeros((B, H, ML_HD), jnp.float32),
            jnp.zeros((B, H), jnp.float32))
    h, hc = prefix_bidir(mlstm_scan, mlstm_final_state, prep(p, True), prep(pc, need_ctx), zero, need_ctx)

    def readout(hh, pp):
        hh = jnp.transpose(hh, (0, 2, 1, 3)).astype(a.dtype)
        T = hh.shape[1]
        return (jax.nn.sigmoid(pp[..., n_kvg + D:]) * head_rmsnorm(hh, norm_g).reshape(B, T, D)) @ w_o

    return readout(h, p), (readout(hc, pc) if need_ctx else None)


def diff_mixer(a, ac, w_kvq, lam_p, norm_g, w_o, layer_idx, need_ctx):
    B, N, D = a.shape
    L = ac.shape[1]
    H, d = DA_HEADS, DA_HD
    lam_init = 0.8 - 0.6 * math.exp(-0.3 * layer_idx)
    lp = lam_p.astype(jnp.float32)
    lam = jnp.exp(jnp.sum(lp[0] * lp[1])) - jnp.exp(jnp.sum(lp[2] * lp[3])) + lam_init

    def attend(q, k, v):
        s = jnp.einsum('bthjd,bshjd->bhjts', q, k).astype(jnp.float32)
        p = jax.nn.softmax(s, axis=-1)
        w = (p[:, :, 0] - lam * p[:, :, 1]).astype(v.dtype)
        return jnp.einsum('bhts,bshe->bthe', w, v)

    def finish(o):
        T = o.shape[1]
        return (head_rmsnorm(o, norm_g) * (1.0 - lam_init)).reshape(B, T, D) @ w_o

    p = a @ w_kvq
    cos, sin = axial_rope_tables(N, d)
    cos, sin = cos[:, None, None, :], sin[:, None, None, :]
    k = apply_axial_rope(p[..., :D].reshape(B, N, H, 2, d), cos, sin)
    v = p[..., D:2 * D].reshape(B, N, H, 2 * d)
    q = apply_axial_rope((p[..., 2 * D:] * d ** -0.5).reshape(B, N, H, 2, d), cos, sin)
    pc = ac @ (w_kvq if need_ctx else w_kvq[:, :2 * D])
    kc = pc[..., :D].reshape(B, L, H, 2, d)
    vc = pc[..., D:2 * D].reshape(B, L, H, 2 * d)
    k_all = jnp.concatenate([k, kc], axis=1)
    v_all = jnp.concatenate([v, vc], axis=1)
    n_blk = N // DA_QBLK
    q_blocks = jnp.moveaxis(q.reshape(B, n_blk, DA_QBLK, H, 2, d), 1, 0)
    o = lax.map(lambda qb: attend(qb, k_all, v_all), q_blocks)
    y = finish(jnp.moveaxis(o, 0, 1).reshape(B, N, H, 2 * d))
    yc = None
    if need_ctx:
        qc = (pc[..., 2 * D:] * d ** -0.5).reshape(B, L, H, 2, d)
        yc = finish(attend(qc, kc, vc))
    return y, yc


def gla_mixer(a, ac, w_in, w_gate_up, b_gate, norm_g, w_o, need_ctx):
    B, N, D = a.shape
    H = GLA_HEADS
    dk_t = H * GLA_DK
    n_kvg = dk_t + D + 2 * GLA_RANK

    def heads(t):
        return jnp.transpose(t.reshape(t.shape[0], t.shape[1], H, -1), (0, 2, 1, 3)).astype(jnp.float32)

    def prep(p, with_q):
        k = heads(p[..., :dk_t])
        v = heads(p[..., dk_t:dk_t + D])
        low = p[..., dk_t + D:n_kvg]
        la = [heads(jax.nn.log_sigmoid((low[..., r * GLA_RANK:(r + 1) * GLA_RANK] @ w_gate_up[r] + b_gate[r])
                                       .astype(jnp.float32))) / GLA_TAU for r in range(2)]
        q = heads(p[..., n_kvg:n_kvg + dk_t]) * GLA_DK ** -0.5 if with_q else None
        return (q, k, v, la[0]), (q, k, v, la[1])

    p = a @ w_in
    pc = ac @ (w_in if need_ctx else w_in[:, :n_kvg])
    zero = jnp.zeros((B, H, GLA_DK, GLA_DV), jnp.float32)
    o, oc = prefix_bidir(gla_scan, gla_final_state, prep(p, True), prep(pc, need_ctx), zero, need_ctx)

    def readout(oo, pp):
        oo = jnp.transpose(oo, (0, 2, 1, 3)).astype(a.dtype)
        T = oo.shape[1]
        return (head_rmsnorm(oo, norm_g).reshape(B, T, D) * jax.nn.silu(pp[..., n_kvg + dk_t:])) @ w_o

    return readout(o, p), (readout(oc, pc) if need_ctx else None)


def setup_inputs(seed: int = 0) -> dict:
    key = jax.random.key(seed)
    ks = jax.random.split(key, 32)
    D = D_MODEL
    n_na, n_ml, n_da, n_gla = LAYERS_PER_MIXER
    dk_t = GLA_HEADS * GLA_DK

    def nrm(i, shape, std):
        return std * jax.random.normal(ks[i], shape, jnp.float32)

    def gain(i, shape):
        return 1.0 + nrm(i, shape, 0.02)

    ig_b = nrm(14, (n_ml, 2, 1, ML_HEADS), 0.1)
    fg_b = jax.random.uniform(ks[15], (n_ml, 2, 1, ML_HEADS), jnp.float32, 3.0, 6.0)
    ml_gate_b = jnp.concatenate([ig_b, fg_b], axis=2).reshape(n_ml, 4 * ML_HEADS)
    return {
        'x': nrm(0, (BATCH, SEQ, D), 1.0),
        'c': nrm(1, (BATCH, D), 1.0),
        'ctx': nrm(2, (BATCH, CTX_LEN, D), 1.0),
        'c_ctx': nrm(3, (D,), 1.0),
        'ada_w': nrm(4, (DEPTH, D, N_MOD * D), 0.5 * D ** -0.5),
        'ada_b': nrm(5, (DEPTH, N_MOD * D), 0.01),
        'norm_g': gain(6, (DEPTH, 3, D)),
        'ffn_w13': nrm(7, (DEPTH, 2, D, 2 * D_FF), D ** -0.5),
        'ffn_w2': nrm(8, (DEPTH, 2, D_FF, D), D_FF ** -0.5),
        'final_g': gain(9, (D,)),
        'na_w_kvq': nrm(10, (n_na, D, 3 * D), D ** -0.5),
        'na_rpb': nrm(11, (n_na, NA_HEADS, 2 * NA_WIN_R - 1, 2 * NA_WIN_C - 1), 0.02),
        'na_w_o': nrm(12, (n_na, D, D), D ** -0.5),
        'ml_w_in': nrm(13, (n_ml, D, 4 * D + 4 * ML_HEADS), D ** -0.5),
        'ml_gate_b': ml_gate_b,
        'ml_conv_w': nrm(16, (n_ml, ML_CONV, 2 * D), ML_CONV ** -0.5),
        'ml_norm_g': gain(17, (n_ml, D)),
        'ml_w_o': nrm(18, (n_ml, D, D), D ** -0.5),
        'da_w_kvq': nrm(19, (n_da, D, 3 * D), D ** -0.5),
        'da_lam': nrm(20, (n_da, 4, DA_HD), 0.1),
        'da_norm_g': gain(21, (n_da, D)),
        'da_w_o': nrm(22, (n_da, D, D), D ** -0.5),
        'gla_w_in': nrm(23, (n_gla, D, 2 * dk_t + 2 * D + 2 * GLA_RANK), D ** -0.5),
        'gla_w_gate_up': nrm(24, (n_gla, 2, GLA_RANK, dk_t), GLA_RANK ** -0.5),
        'gla_b_gate': nrm(25, (n_gla, 2, dk_t), 0.1),
        'gla_norm_g': gain(26, (n_gla, D)),
        'gla_w_o': nrm(27, (n_gla, D, D), D ** -0.5),
    }


def reference(x, c, ctx, c_ctx, ada_w, ada_b, norm_g, ffn_w13, ffn_w2, final_g,
              na_w_kvq, na_rpb, na_w_o,
              ml_w_in, ml_gate_b, ml_conv_w, ml_norm_g, ml_w_o,
              da_w_kvq, da_lam, da_norm_g, da_w_o,
              gla_w_in, gla_w_gate_up, gla_b_gate, gla_norm_g, gla_w_o):
    D = D_MODEL
    s_lat = jax.nn.silu(c)
    s_ctx = jax.nn.silu(c_ctx)
    h, hc = x, ctx
    for i in range(DEPTH):
        kind, j = i % N_MIXERS, i // N_MIXERS
        need_ctx = i < DEPTH - 1
        mod = jnp.split((s_lat @ ada_w[i] + ada_b[i])[:, None, :], N_MOD, axis=-1)
        n_c = N_MOD if need_ctx else 5
        modc = jnp.split((s_ctx @ ada_w[i, :, :n_c * D] + ada_b[i, :n_c * D])[None, None, :], n_c, axis=-1)
        h = h + ffn_branch(h, norm_g[i, 0], mod[0], mod[1], mod[2], ffn_w13[i, 0], ffn_w2[i, 0])
        hc = hc + ffn_branch(hc, norm_g[i, 0], modc[0], modc[1], modc[2], ffn_w13[i, 0], ffn_w2[i, 0])
        a = modulate(rmsnorm(h, norm_g[i, 1]), mod[3], mod[4])
        ac = modulate(rmsnorm(hc, norm_g[i, 1]), modc[3], modc[4])
        if kind == 0:
            y, yc = na_mixer(a, ac, na_w_kvq[j], na_rpb[j], na_w_o[j], need_ctx)
        elif kind == 1:
            y, yc = mlstm_mixer(a, ac, ml_w_in[j], ml_gate_b[j], ml_conv_w[j], ml_norm_g[j], ml_w_o[j], need_ctx)
        elif kind == 2:
            y, yc = diff_mixer(a, ac, da_w_kvq[j], da_lam[j], da_norm_g[j], da_w_o[j], i, need_ctx)
        else:
            y, yc = gla_mixer(a, ac, gla_w_in[j], gla_w_gate_up[j], gla_b_gate[j], gla_norm_g[j], gla_w_o[j], need_ctx)
        h = h + mod[5] * y
        h = h + ffn_branch(h, norm_g[i, 2], mod[6], mod[7], mod[8], ffn_w13[i, 1], ffn_w2[i, 1])
        if need_ctx:
            hc = hc + modc[5] * yc
            hc = hc + ffn_branch(hc, norm_g[i, 2], modc[6], modc[7], modc[8], ffn_w13[i, 1], ffn_w2[i, 1])
    return rmsnorm(h, final_g)
```

```python
import functools

import numpy as np
import jax
import jax.numpy as jnp
from jax import lax
from jax.experimental import pallas as pl
from jax.experimental.pallas import tpu as pltpu

F32 = jnp.float32
BF16 = jnp.bfloat16

D = 1024
N_LAT = 2048
N_CTX = 256
DEPTH = 4
N_MOD = 9
D_FF = 2816
EPS = 1e-6
NEG = -1e30
GRID_W = 64

NA_HEADS = 16
NA_HD = 64
NA_WIN_R = 8
NA_WIN_C = 16

ML_HEADS = 4
ML_HD = 256

DA_HEADS = 8
DA_HD = 64
ROPE_BASE = 10000.0

GLA_HEADS = 4
GLA_DK = 128
GLA_DV = 256
GLA_RANK = 16
GLA_TAU = 16.0

ROW_BLK = 256
LAT_BLKS = N_LAT // ROW_BLK
MOD_ROWS = 16
LANES = 128
VMEM_LIMIT = 56 * 1024 * 1024


def _cparams(n_axes, vmem=VMEM_LIMIT):
    return pltpu.CompilerParams(dimension_semantics=("arbitrary",) * n_axes, vmem_limit_bytes=vmem)


def _sigmoid(x):
    return 1.0 / (1.0 + jnp.exp(-x))


def _silu(x):
    return x * _sigmoid(x)


def _log_sigmoid(x):
    return jnp.minimum(x, 0.0) - jnp.log1p(jnp.exp(-jnp.abs(x)))


def _rms(x, g):
    return x * lax.rsqrt(jnp.mean(x * x, axis=-1, keepdims=True) + EPS) * g


def _dot(a, b):
    return jnp.dot(a, b, preferred_element_type=F32)


def _dot_nt(a, b):
    return lax.dot_general(a, b, (((1,), (1,)), ((), ())), preferred_element_type=F32)


def _dot_tn(a, b):
    return lax.dot_general(a, b, (((0,), (0,)), ((), ())), preferred_element_type=F32)


def _mod_row(i, tm, n_batch):
    return jnp.minimum(i // (N_LAT // tm), n_batch)


def _mod_spec(k, tm, n_batch):
    return pl.BlockSpec((1, 1, D), lambda i: (_mod_row(i, tm, n_batch) * N_MOD + k, 0, 0))


def _mod_kernel(s_ref, w_ref, b_ref, o_ref):
    a = _silu(s_ref[...]).astype(BF16)
    o_ref[0] = _dot(a, w_ref[0].astype(BF16)) + b_ref[0]


def _mod_tables(s, ada_w, ada_b):
    return pl.pallas_call(
        _mod_kernel,
        grid=(DEPTH, N_MOD),
        in_specs=[
            pl.BlockSpec((MOD_ROWS, D), lambda i, k: (0, 0)),
            pl.BlockSpec((1, D, D), lambda i, k: (i, 0, k)),
            pl.BlockSpec((1, 1, D), lambda i, k: (i, 0, k)),
        ],
        out_specs=pl.BlockSpec((1, MOD_ROWS, D), lambda i, k: (i, 0, k)),
        out_shape=jax.ShapeDtypeStruct((DEPTH, MOD_ROWS, N_MOD * D), F32),
        compiler_params=_cparams(2),
        name="mod_tables",
    )(s, ada_w, ada_b.reshape(DEPTH, 1, N_MOD * D))


FFN_TM = 512
FFN_CHUNK = D_FF // 2


def _ffn_kernel(h_ref, g_ref, sh_ref, sc_ref, gt_ref, w13_ref, w2_ref, fg_ref, o_ref, *, final):
    x = h_ref[...]
    a = (_rms(x, g_ref[...]) * (1.0 + sc_ref[0]) + sh_ref[0]).astype(BF16)
    acc = jnp.zeros(x.shape, F32)
    for c0 in range(0, D_FF, FFN_CHUNK):
        gate = _dot(a, w13_ref[:, c0:c0 + FFN_CHUNK])
        up = _dot(a, w13_ref[:, D_FF + c0:D_FF + c0 + FFN_CHUNK])
        act = (_silu(gate) * up).astype(BF16)
        acc = acc + _dot(act, w2_ref[c0:c0 + FFN_CHUNK, :])
    y = x + 0.5 * gt_ref[0] * acc
    if final:
        y = _rms(y, fg_ref[...])
    o_ref[...] = y


def _ffn(h, mod3, g, w13, w2, final_g, k0, n_rows, n_batch, final=False):
    tm = FFN_TM
    const = lambda i: (0, 0)
    return pl.pallas_call(
        functools.partial(_ffn_kernel, final=final),
        grid=(n_rows // tm,),
        in_specs=[
            pl.BlockSpec((tm, D), lambda i: (i, 0)),
            pl.BlockSpec((1, D), const),
            _mod_spec(k0, tm, n_batch),
            _mod_spec(k0 + 1, tm, n_batch),
            _mod_spec(k0 + 2, tm, n_batch),
            pl.BlockSpec((D, 2 * D_FF), const, pipeline_mode=pl.Buffered(1)),
            pl.BlockSpec((D_FF, D), const, pipeline_mode=pl.Buffered(1)),
            pl.BlockSpec((1, D), const),
        ],
        out_specs=pl.BlockSpec((tm, D), lambda i: (i, 0)),
        out_shape=jax.ShapeDtypeStruct((n_rows, D), F32),
        compiler_params=_cparams(1),
        name="ffn",
    )(h, g.reshape(1, D), mod3, mod3, mod3, w13, w2, final_g.reshape(1, D))


PROJ_TM = 256


def _proj_kernel(h_ref, g_ref, sh_ref, sc_ref, w_ref, *o_refs, segs):
    x = h_ref[...]
    a = (_rms(x, g_ref[...]) * (1.0 + sc_ref[0]) + sh_ref[0]).astype(BF16)
    for o_ref, (c0, c1, scale) in zip(o_refs, segs):
        y = _dot(a, w_ref[:, c0:c1])
        if scale != 1.0:
            y = y * scale
        o_ref[...] = y.astype(o_ref.dtype)


def _proj(h, mod3, g, w, segs, n_batch):
    tm = PROJ_TM
    n_rows = h.shape[0]
    const = lambda i: (0, 0)
    return pl.pallas_call(
        functools.partial(_proj_kernel, segs=tuple((c0, c1, sc) for c0, c1, sc, _ in segs)),
        grid=(n_rows // tm,),
        in_specs=[
            pl.BlockSpec((tm, D), lambda i: (i, 0)),
            pl.BlockSpec((1, D), const),
            _mod_spec(3, tm, n_batch),
            _mod_spec(4, tm, n_batch),
            pl.BlockSpec(w.shape, const, pipeline_mode=pl.Buffered(1)),
        ],
        out_specs=[pl.BlockSpec((tm, c1 - c0), lambda i: (i, 0)) for c0, c1, _, _ in segs],
        out_shape=[jax.ShapeDtypeStruct((n_rows, c1 - c0), dt) for c0, c1, _, dt in segs],
        compiler_params=_cparams(1),
        name="mixer_proj",
    )(h, g.reshape(1, D), mod3, mod3, w)


OUT_TM = 256


def _head_rms(x, g, n_heads):
    hd = D // n_heads
    parts = []
    for i in range(n_heads):
        xi = x[:, i * hd:(i + 1) * hd]
        parts.append(xi * lax.rsqrt(jnp.mean(xi * xi, axis=-1, keepdims=True) + EPS))
    return jnp.concatenate(parts, axis=-1) * g


def _out_kernel(h_ref, gt_ref, w_ref, *refs, mode, n_lat_tiles):
    o_ref = refs[-1]
    if mode == "plain":
        y_lat, y_ctx = refs[:2]
        y = jnp.where(pl.program_id(0) < n_lat_tiles, y_lat[...], y_ctx[...])
    elif mode == "ml":
        hf, hb, og, ng = refs[:4]
        y = _sigmoid(og[...]) * _head_rms(hf[...] + hb[...], ng[...], ML_HEADS)
    else:
        of, ob, rg, ng = refs[:4]
        y = _head_rms(of[...] + ob[...], ng[...], GLA_HEADS) * _silu(rg[...])
    o_ref[...] = h_ref[...] + gt_ref[0] * _dot(y.astype(BF16), w_ref[...])


def _out_proj(h, mod3, w_o, ins, n_rows, n_batch, mode, norm_g=None):
    tm = OUT_TM
    nl = n_batch * N_LAT // tm
    const = lambda i: (0, 0)
    row = pl.BlockSpec((tm, D), lambda i: (i, 0))
    if mode == "plain":
        extra = list(ins)
        extra_specs = [pl.BlockSpec((tm, D), lambda i: (jnp.minimum(i, nl - 1), 0)),
                       pl.BlockSpec((tm, D), lambda i: (jnp.maximum(i - nl, 0), 0))]
    else:
        extra = list(ins) + [norm_g.reshape(1, D)]
        extra_specs = [row] * len(ins) + [pl.BlockSpec((1, D), const)]
    return pl.pallas_call(
        functools.partial(_out_kernel, mode=mode, n_lat_tiles=nl),
        grid=(n_rows // tm,),
        in_specs=[row, _mod_spec(5, tm, n_batch), pl.BlockSpec((D, D), const)] + extra_specs,
        out_specs=row,
        out_shape=jax.ShapeDtypeStruct((n_rows, D), F32),
        compiler_params=_cparams(1),
        name="mixer_out",
    )(h, mod3, w_o, *extra)


def _lane_lo(shape):
    return lax.broadcasted_iota(jnp.int32, shape, 1) < (LANES // 2)


def _half_masked(x, lo, head):
    return jnp.where(lo if head == 0 else jnp.logical_not(lo), x, jnp.zeros_like(x))


def _pair_attend(qp, segs, biases):
    lo = _lane_lo(qp.shape)
    outs = []
    for head in range(2):
        qm = _half_masked(qp, lo, head)
        scores = []
        for si, (k, _) in enumerate(segs):
            s = _dot_nt(qm, k)
            b = biases[head][si]
            scores.append(s if b is None else s + b)
        m = scores[0].max(axis=-1, keepdims=True)
        for s in scores[1:]:
            m = jnp.maximum(m, s.max(axis=-1, keepdims=True))
        den = None
        acc = None
        for s, (_, v) in zip(scores, segs):
            p = jnp.exp(s - m)
            ps = p.sum(axis=-1, keepdims=True)
            pv = _dot(p.astype(BF16), v)
            den = ps if den is None else den + ps
            acc = pv if acc is None else acc + pv
        outs.append(acc / den)
    return jnp.where(lo, outs[0], outs[1])


NA_ROWS = N_LAT // GRID_W
NA_NLOC = NA_WIN_R * GRID_W


def _na_bias_tables(rpb):
    c = np.arange(GRID_W)
    win0 = np.clip(c - NA_WIN_C // 2, 0, GRID_W - NA_WIN_C)
    kc = np.arange(GRID_W)
    in_win = (kc[None, :] >= win0[:, None]) & (kc[None, :] < win0[:, None] + NA_WIN_C)
    col_off = np.clip(kc[None, :] - c[:, None], 1 - NA_WIN_C, NA_WIN_C - 1) + NA_WIN_C - 1
    t = rpb.astype(F32)[:, :, col_off]
    t = jnp.where(in_win[None, None], t, NEG)
    tabs = []
    for s in range(NA_WIN_R):
        blk = t[:, s:s + NA_WIN_R]
        tabs.append(jnp.transpose(blk, (0, 2, 1, 3)).reshape(NA_HEADS, GRID_W, NA_NLOC))
    return jnp.stack(tabs)


def _na_row0(r):
    return jnp.clip(r - NA_WIN_R // 2, 0, NA_ROWS - NA_WIN_R)


def _na_kernel(q_ref, k_ref, v_ref, kc_ref, vc_ref, b_ref, o_ref):
    r = pl.program_id(1)
    start = pl.multiple_of(_na_row0(r) * GRID_W, GRID_W)
    for hp in range(NA_HEADS // 2):
        cols = slice(hp * LANES, (hp + 1) * LANES)
        segs = [(k_ref[pl.ds(start, NA_NLOC), cols], v_ref[pl.ds(start, NA_NLOC), cols]),
                (kc_ref[:, cols], vc_ref[:, cols])]
        biases = [[b_ref[0, 2 * hp + head], None] for head in range(2)]
        o_ref[:, cols] = _pair_attend(q_ref[:, cols], segs, biases).astype(o_ref.dtype)


def _na_attention(q, k, v, bias, n_batch):
    ctx0 = n_batch * LAT_BLKS
    lat = pl.BlockSpec((N_LAT, D), lambda b, r: (b, 0))
    ctx = pl.BlockSpec((N_CTX, D), lambda b, r: (ctx0 + b, 0))
    return pl.pallas_call(
        _na_kernel,
        grid=(n_batch, NA_ROWS),
        in_specs=[
            pl.BlockSpec((GRID_W, D), lambda b, r: (b * NA_ROWS + r, 0)),
            lat, lat, ctx, ctx,
            pl.BlockSpec((1, NA_HEADS, GRID_W, NA_NLOC), lambda b, r: (_na_row0(r) - r + NA_WIN_R - 1, 0, 0, 0)),
        ],
        out_specs=pl.BlockSpec((GRID_W, D), lambda b, r: (b * NA_ROWS + r, 0)),
        out_shape=jax.ShapeDtypeStruct((n_batch * N_LAT, D), BF16),
        compiler_params=_cparams(2),
        name="na_attention",
    )(q, k, v, k, v, bias)


def _ctx_attn_kernel(q_ref, k_ref, v_ref, o_ref):
    for hp in range(NA_HEADS // 2):
        cols = slice(hp * LANES, (hp + 1) * LANES)
        segs = [(k_ref[:, cols], v_ref[:, cols])]
        o_ref[:, cols] = _pair_attend(q_ref[:, cols], segs, [[None], [None]]).astype(o_ref.dtype)


def _ctx_attention(q, k, v, n_batch):
    ctx0 = n_batch * LAT_BLKS
    spec = pl.BlockSpec((N_CTX, D), lambda b: (ctx0 + b, 0))
    return pl.pallas_call(
        _ctx_attn_kernel,
        grid=(n_batch,),
        in_specs=[spec, spec, spec],
        out_specs=pl.BlockSpec((N_CTX, D), lambda b: (b, 0)),
        out_shape=jax.ShapeDtypeStruct((n_batch * N_CTX, D), BF16),
        compiler_params=_cparams(1),
        name="na_ctx_attention",
    )(q, k, v)


def _na_mixer(h, mod3, g, w_kvq, rpb, n_batch):
    segs = [(0, D, 1.0, BF16), (D, 2 * D, 1.0, BF16), (2 * D, 3 * D, NA_HD ** -0.5, BF16)]
    k, v, q = _proj(h, mod3, g, w_kvq.astype(BF16), segs, n_batch)
    return _na_attention(q, k, v, _na_bias_tables(rpb), n_batch), _ctx_attention(q, k, v, n_batch)


DA_TQ = 256
DA_W = 2 * DA_HD


def _rope_tables():
    t = np.arange(N_LAT)
    row = (t // GRID_W).astype(np.float64)
    col = (t % GRID_W).astype(np.float64)
    per_axis = DA_HD // 2
    freqs = ROPE_BASE ** (-np.arange(0, per_axis, 2, dtype=np.float64) / per_axis)
    ar = row[:, None] * freqs
    ac = col[:, None] * freqs
    ang = np.concatenate([ar, ar, ac, ac], axis=-1)
    cos = np.tile(np.cos(ang), (1, 2))
    sin = np.tile(np.sin(ang), (1, 2))
    quarter = (np.arange(DA_W) % DA_HD) // (DA_HD // 4)
    even = (quarter % 2 == 0)[None, :]
    sin_a = np.where(even, -sin, 0.0)
    sin_b = np.where(even, 0.0, sin)
    return tuple(jnp.asarray(a, F32) for a in (cos, sin_a, sin_b))


def _rope(x, cos, sin_a, sin_b):
    q16 = DA_HD // 4
    return x * cos + pltpu.roll(x, DA_W - q16, 1) * sin_a + pltpu.roll(x, q16, 1) * sin_b


def _da_lambda(lp, lam_init):
    a = jnp.sum(lp[0:1] * lp[1:2], axis=-1, keepdims=True)
    b = jnp.sum(lp[2:3] * lp[3:4], axis=-1, keepdims=True)
    return jnp.exp(a) - jnp.exp(b) + lam_init


def _diff_attend(q, segs, lam, ng, lam_init):
    lo = _lane_lo(q.shape)
    weights = [None] * len(segs)
    for j in range(2):
        qm = _half_masked(q, lo, j)
        scores = [_dot_nt(qm, k) for k, _ in segs]
        m = scores[0].max(axis=-1, keepdims=True)
        for s in scores[1:]:
            m = jnp.maximum(m, s.max(axis=-1, keepdims=True))
        es = [jnp.exp(s - m) for s in scores]
        den = es[0].sum(axis=-1, keepdims=True)
        for e in es[1:]:
            den = den + e.sum(axis=-1, keepdims=True)
        coef = (1.0 / den) if j == 0 else (-lam / den)
        for si, e in enumerate(es):
            weights[si] = e * coef if j == 0 else weights[si] + e * coef
    o = None
    for w, (_, v) in zip(weights, segs):
        pv = _dot(w.astype(BF16), v)
        o = pv if o is None else o + pv
    return o * lax.rsqrt(jnp.mean(o * o, axis=-1, keepdims=True) + EPS) * ng * (1.0 - lam_init)


def _da_kernel(lam_ref, q_ref, k_ref, v_ref, kc_ref, vc_ref, cq_ref, saq_ref, sbq_ref,
               ck_ref, sak_ref, sbk_ref, ng_ref, o_ref, kr_ref, *, lam_init):
    @pl.when(pl.program_id(2) == 0)
    def _():
        kr_ref[...] = _rope(k_ref[...], ck_ref[...], sak_ref[...], sbk_ref[...]).astype(BF16)

    lam = _da_lambda(lam_ref[...], lam_init)
    q = _rope(q_ref[...], cq_ref[...], saq_ref[...], sbq_ref[...]).astype(BF16)
    segs = [(kr_ref[...], v_ref[...]), (kc_ref[...].astype(BF16), vc_ref[...])]
    o_ref[...] = _diff_attend(q, segs, lam, ng_ref[...], lam_init).astype(o_ref.dtype)


def _da_attention(q, k, v, lam_p, norm_g, lam_init, n_batch):
    cos, sin_a, sin_b = _rope_tables()
    nqb = N_LAT // DA_TQ
    ctx0 = n_batch * LAT_BLKS
    tab_q = pl.BlockSpec((DA_TQ, DA_W), lambda b, h, i: (i, 0))
    tab_k = pl.BlockSpec((N_LAT, DA_W), lambda b, h, i: (0, 0))
    lat = pl.BlockSpec((N_LAT, DA_W), lambda b, h, i: (b, h))
    ctx = pl.BlockSpec((N_CTX, DA_W), lambda b, h, i: (ctx0 + b, h))
    return pl.pallas_call(
        functools.partial(_da_kernel, lam_init=lam_init),
        grid=(n_batch, DA_HEADS, nqb),
        in_specs=[
            pl.BlockSpec((4, DA_HD), lambda b, h, i: (0, 0)),
            pl.BlockSpec((DA_TQ, DA_W), lambda b, h, i: (b * nqb + i, h)),
            lat, lat, ctx, ctx,
            tab_q, tab_q, tab_q, tab_k, tab_k, tab_k,
            pl.BlockSpec((1, DA_W), lambda b, h, i: (0, h)),
        ],
        out_specs=pl.BlockSpec((DA_TQ, DA_W), lambda b, h, i: (b * nqb + i, h)),
        out_shape=jax.ShapeDtypeStruct((n_batch * N_LAT, D), BF16),
        scratch_shapes=[pltpu.VMEM((N_LAT, DA_W), BF16)],
        compiler_params=_cparams(3),
        name="diff_attention",
    )(lam_p, q, k, v, k, v, cos, sin_a, sin_b, cos, sin_a, sin_b, norm_g.reshape(1, D))


def _da_ctx_kernel(lam_ref, q_ref, k_ref, v_ref, ng_ref, o_ref, *, lam_init):
    lam = _da_lambda(lam_ref[...], lam_init)
    segs = [(k_ref[...].astype(BF16), v_ref[...])]
    o_ref[...] = _diff_attend(q_ref[...].astype(BF16), segs, lam, ng_ref[...], lam_init).astype(o_ref.dtype)


def _da_ctx_attention(q, k, v, lam_p, norm_g, lam_init, n_batch):
    ctx0 = n_batch * LAT_BLKS
    spec = pl.BlockSpec((N_CTX, DA_W), lambda b, h: (ctx0 + b, h))
    return pl.pallas_call(
        functools.partial(_da_ctx_kernel, lam_init=lam_init),
        grid=(n_batch, DA_HEADS),
        in_specs=[pl.BlockSpec((4, DA_HD), lambda b, h: (0, 0)), spec, spec, spec,
                  pl.BlockSpec((1, DA_W), lambda b, h: (0, h))],
        out_specs=pl.BlockSpec((N_CTX, DA_W), lambda b, h: (b, h)),
        out_shape=jax.ShapeDtypeStruct((n_batch * N_CTX, D), BF16),
        compiler_params=_cparams(2),
        name="diff_ctx_attention",
    )(lam_p, q, k, v, norm_g.reshape(1, D))


def _da_mixer(h, mod3, g, w_kvq, lam_p, norm_g, layer_idx, n_batch):
    lam_init = 0.8 - 0.6 * float(np.exp(-0.3 * layer_idx))
    segs = [(0, D, 1.0, F32), (D, 2 * D, 1.0, BF16), (2 * D, 3 * D, DA_HD ** -0.5, F32)]
    k, v, q = _proj(h, mod3, g, w_kvq.astype(BF16), segs, n_batch)
    lam_p = lam_p.astype(F32)
    return (_da_attention(q, k, v, lam_p, norm_g, lam_init, n_batch),
            _da_ctx_attention(q, k, v, lam_p, norm_g, lam_init, n_batch))


SCAN_STEPS = 1 + LAT_BLKS


def _scan_row_blk(direction, n_batch):
    ctx0 = n_batch * LAT_BLKS

    def blk(b, s):
        lat = b * LAT_BLKS + (s - 1 if direction == 0 else LAT_BLKS - s)
        return jnp.where(s == 0, ctx0 + b, lat)
    return blk


def _scan_spec(width, direction, n_batch):
    blk = _scan_row_blk(direction, n_batch)
    return pl.BlockSpec((ROW_BLK, width), lambda b, h, s: (blk(b, s), h))


def _scan_spec_all(width, direction, n_batch):
    blk = _scan_row_blk(direction, n_batch)
    return pl.BlockSpec((ROW_BLK, width), lambda b, h, s: (blk(b, s), 0))


def _ml_conv_kernel(xk_ref, xkp_ref, xkn_ref, xq_ref, xqp_ref, xqn_ref, w_ref, ok_ref, oq_ref, *, n_lat_blks):
    i = pl.program_id(0)
    is_ctx = i >= n_lat_blks
    first = jnp.logical_or(is_ctx, i % LAT_BLKS == 0)
    last = jnp.logical_or(is_ctx, i % LAT_BLKS == LAT_BLKS - 1)
    rid = lax.broadcasted_iota(jnp.int32, (ROW_BLK, D), 0)
    groups = ((xk_ref, xkp_ref, xkn_ref, 0, ok_ref, ML_HD ** -0.5), (xq_ref, xqp_ref, xqn_ref, D, oq_ref, 1.0))
    for x_ref, xp_ref, xn_ref, c0, o_ref, scale in groups:
        x = x_ref[...]
        prev_row = jnp.where(first, 0.0, xp_ref[7:8, :])
        next_row = jnp.where(last, 0.0, xn_ref[0:1, :])
        x_prev = jnp.where(rid == 0, prev_row, pltpu.roll(x, 1, 0))
        x_next = jnp.where(rid == ROW_BLK - 1, next_row, pltpu.roll(x, ROW_BLK - 1, 0))
        y = w_ref[0:1, c0:c0 + D] * x_prev + w_ref[1:2, c0:c0 + D] * x + w_ref[2:3, c0:c0 + D] * x_next
        o_ref[...] = (_silu(y) * scale).astype(o_ref.dtype)


def _ml_conv(pk, pq, conv_w, n_batch):
    n_rows = pk.shape[0]
    n_blk = n_rows // ROW_BLK
    sub = ROW_BLK // 8
    main = pl.BlockSpec((ROW_BLK, D), lambda i: (i, 0))
    prev = pl.BlockSpec((8, D), lambda i: (jnp.maximum(i * sub - 1, 0), 0))
    nxt = pl.BlockSpec((8, D), lambda i: (jnp.minimum((i + 1) * sub, n_blk * sub - 1), 0))
    return pl.pallas_call(
        functools.partial(_ml_conv_kernel, n_lat_blks=n_batch * LAT_BLKS),
        grid=(n_blk,),
        in_specs=[main, prev, nxt, main, prev, nxt, pl.BlockSpec((3, 2 * D), lambda i: (0, 0))],
        out_specs=[main, main],
        out_shape=[jax.ShapeDtypeStruct((n_rows, D), BF16)] * 2,
        compiler_params=_cparams(1),
        name="mlstm_conv",
    )(pk, pk, pk, pq, pq, pq, conv_w)


def _pick_lane(x, idx):
    lane = lax.broadcasted_iota(jnp.int32, x.shape, 1)
    return jnp.sum(jnp.where(lane == idx, x, 0.0), axis=1, keepdims=True)


def _pick_sublane(x, idx):
    sub = lax.broadcasted_iota(jnp.int32, x.shape, 0)
    return jnp.sum(jnp.where(sub == idx, x, 0.0), axis=0, keepdims=True)


def _ml_scan_kernel(gb_ref, g0_ref, q0_ref, k0_ref, v0_ref, g1_ref, q1_ref, k1_ref, v1_ref,
                    h0_ref, h1_ref, c_ref, n_ref, m_ref):
    head = pl.program_id(1)

    @pl.when(pl.program_id(2) == 0)
    def _():
        c_ref[...] = jnp.zeros_like(c_ref)
        n_ref[...] = jnp.zeros_like(n_ref)
        m_ref[...] = jnp.zeros_like(m_ref)

    r = lax.broadcasted_iota(jnp.int32, (ROW_BLK, ROW_BLK), 0)
    c = lax.broadcasted_iota(jnp.int32, (ROW_BLK, ROW_BLK), 1)
    dirs = ((g0_ref, q0_ref, k0_ref, v0_ref, h0_ref), (g1_ref, q1_ref, k1_ref, v1_ref, h1_ref))
    for d, (g_ref, q_ref, k_ref, v_ref, o_ref) in enumerate(dirs):
        valid = (r >= c) if d == 0 else (r <= c)
        other = (r <= c) if d == 0 else (r >= c)
        g = g_ref[...] + gb_ref[...]
        gt = g.T
        ii = d * 2 * ML_HEADS + head
        fi = ii + ML_HEADS
        i_col = _pick_lane(g, ii)
        f_col = _log_sigmoid(_pick_lane(g, fi))
        i_row = _pick_sublane(gt, ii)
        f_row = _log_sigmoid(_pick_sublane(gt, fi))
        b_col = jnp.sum(jnp.where(valid, f_row, 0.0), axis=1, keepdims=True)
        b_row = jnp.sum(jnp.where(other, f_col, 0.0), axis=0, keepdims=True)
        total = jnp.sum(f_col, axis=0, keepdims=True)
        m_prev = m_ref[d][0:1, 0:1]
        n_prev = n_ref[d][0:1, :]
        c_prev = c_ref[d]
        qc = q_ref[...]
        kc = k_ref[...]
        vc = v_ref[...]

        dmat = jnp.where(valid, b_col - b_row + i_row, NEG)
        inter = b_col + m_prev
        m_t = jnp.maximum(inter, dmat.max(axis=1, keepdims=True))
        w_inter = jnp.exp(inter - m_t)
        s = _dot_nt(qc, kc) * jnp.exp(dmat - m_t)
        num = w_inter * _dot(qc, c_prev.astype(BF16)) + _dot(s.astype(BF16), vc)
        den = w_inter * jnp.sum(qc.astype(F32) * n_prev, axis=1, keepdims=True) + s.sum(axis=1, keepdims=True)
        o_ref[...] = num / jnp.maximum(jnp.abs(den), jnp.exp(-m_t))

        g_col = total - b_col + i_col
        m_new = jnp.maximum(total + m_prev, g_col.max(axis=0, keepdims=True))
        w_old = jnp.exp(total + m_prev - m_new)
        kw = kc.astype(F32) * jnp.exp(g_col - m_new)
        c_ref[d] = w_old * c_prev + _dot_tn(kw.astype(BF16), vc)
        n_ref[d] = jnp.broadcast_to(w_old * n_prev + kw.sum(axis=0, keepdims=True), n_ref.shape[1:])
        m_ref[d] = jnp.broadcast_to(m_new, m_ref.shape[1:])


def _ml_scan(q, k, v, gates, gate_b, n_batch):
    n_rows = q.shape[0]
    in_specs = [pl.BlockSpec((1, LANES), lambda b, h, s: (0, 0))]
    for d in range(2):
        in_specs += [_scan_spec_all(LANES, d, n_batch)] + [_scan_spec(ML_HD, d, n_batch)] * 3
    return pl.pallas_call(
        _ml_scan_kernel,
        grid=(n_batch, ML_HEADS, SCAN_STEPS),
        in_specs=in_specs,
        out_specs=[_scan_spec(ML_HD, 0, n_batch), _scan_spec(ML_HD, 1, n_batch)],
        out_shape=[jax.ShapeDtypeStruct((n_rows, D), F32)] * 2,
        scratch_shapes=[pltpu.VMEM((2, ML_HD, ML_HD), F32), pltpu.VMEM((2, 8, ML_HD), F32),
                        pltpu.VMEM((2, 8, LANES), F32)],
        compiler_params=_cparams(3),
        name="mlstm_scan",
    )(gate_b, gates, q, k, v, gates, q, k, v)


def _ml_mixer(h, mod3, g, w_in, gate_b, conv_w, n_batch):
    n_g = 4 * ML_HEADS
    wk, wv, wg, wq, wo = (w_in[:, :D], w_in[:, D:2 * D], w_in[:, 2 * D:2 * D + n_g],
                          w_in[:, 2 * D + n_g:3 * D + n_g], w_in[:, 3 * D + n_g:])
    w = jnp.concatenate([wk, wv, wq, wo, jnp.pad(wg, ((0, 0), (0, LANES - n_g)))], axis=1).astype(BF16)
    segs = [(0, D, 1.0, F32), (D, 2 * D, 1.0, BF16), (2 * D, 3 * D, 1.0, F32), (3 * D, 4 * D, 1.0, F32),
            (4 * D, 4 * D + LANES, 1.0, F32)]
    pk, v, pq, og, gates = _proj(h, mod3, g, w, segs, n_batch)
    k, q = _ml_conv(pk, pq, conv_w.astype(F32), n_batch)
    gb = jnp.pad(gate_b.astype(F32), (0, LANES - n_g)).reshape(1, LANES)
    hf, hb = _ml_scan(q, k, v, gates, gb, n_batch)
    return hf, hb, og


GLA_L = 64
GLA_LEVELS = (64, 32, 16, 8, 4, 2)


def _gla_level_tables():
    L = GLA_L
    u = np.arange(L)
    nl = len(GLA_LEVELS)
    mats = np.zeros((2, (1 + 2 * nl) * L, L), np.float32)
    masks = np.zeros((2, nl + 1, L, L), np.float32)
    tt, ss = np.meshgrid(u, u, indexing="ij")
    for d in range(2):
        for t in range(L):
            mats[d, t] = (u <= t) if d == 0 else (u >= t)
        for l, n in enumerate(GLA_LEVELS):
            half = n // 2
            for t in range(L):
                b0 = t - t % n
                later = ((t % n) >= half) if d == 0 else ((t % n) < half)
                if d == 0:
                    mid = b0 + half - 1
                    if later:
                        mats[d, (1 + l) * L + t] = (u > mid) & (u <= t)
                    else:
                        mats[d, (1 + nl + l) * L + t] = (u > t) & (u <= mid)
                else:
                    mid = b0 + half
                    if later:
                        mats[d, (1 + l) * L + t] = (u >= t) & (u < mid)
                    else:
                        mats[d, (1 + nl + l) * L + t] = (u >= mid) & (u < t)
            same = (tt // n) == (ss // n)
            if d == 0:
                masks[d, l] = same & ((tt % n) >= half) & ((ss % n) < half)
            else:
                masks[d, l] = same & ((tt % n) < half) & ((ss % n) >= half)
        masks[d, nl] = np.eye(L)
    return jnp.asarray(mats, BF16), jnp.asarray(masks, F32)


def _split3(x):
    hi = x.astype(BF16)
    r1 = x - hi.astype(F32)
    mid = r1.astype(BF16)
    lo = (r1 - mid.astype(F32)).astype(BF16)
    return hi, mid, lo


def _gla_kernel(lvl_ref, msk_ref, wup_ref, bg_ref, q0_ref, k0_ref, v0_ref, l0_ref, q1_ref, k1_ref, v1_ref, l1_ref,
                o0_ref, o1_ref, st_ref):
    @pl.when(pl.program_id(2) == 0)
    def _():
        st_ref[...] = jnp.zeros_like(st_ref)

    L = GLA_L
    nl = len(GLA_LEVELS)
    n_chunks = ROW_BLK // L
    dirs = ((q0_ref, k0_ref, v0_ref, l0_ref, o0_ref), (q1_ref, k1_ref, v1_ref, l1_ref, o1_ref))
    for d, (q_ref, k_ref, v_ref, low_ref, o_ref) in enumerate(dirs):
        pre = _dot(low_ref[...].astype(BF16), wup_ref[d]) + bg_ref[d]
        la_all = _log_sigmoid(pre) * (1.0 / GLA_TAU)
        mats = lvl_ref[d]
        st = st_ref[d]
        for ci in (range(n_chunks) if d == 0 else reversed(range(n_chunks))):
            rows = slice(ci * L, (ci + 1) * L)
            q = q_ref[rows, :]
            k = k_ref[rows, :]
            v = v_ref[rows, :]
            hi, mid, lo = _split3(la_all[rows, :])
            e = _dot(mats, hi) + _dot(mats, mid) + _dot(mats, lo)
            bc = e[0:L]
            att = msk_ref[d, nl] * _dot_nt(q.astype(BF16), k.astype(BF16))
            for l in range(nl):
                eq = e[(1 + l) * L:(2 + l) * L]
                ek = e[(1 + nl + l) * L:(2 + nl + l) * L]
                ql = (q * jnp.exp(eq)).astype(BF16)
                kl = (k * jnp.exp(ek)).astype(BF16)
                att = att + msk_ref[d, l] * _dot_nt(ql, kl)
            qd = (q * jnp.exp(bc)).astype(BF16)
            o_ref[rows, :] = _dot(att.astype(BF16), v) + _dot_nt(qd, st.astype(BF16))
            b_end = bc[L - 1:L] if d == 0 else bc[0:1]
            kd = (k * jnp.exp(b_end - bc)).astype(BF16)
            st = st * jnp.exp(b_end) + _dot_tn(v, kd)
        st_ref[d] = st


def _gla_scan(q, k, v, low, wup, bg, n_batch):
    n_rows = q.shape[0]
    mats, masks = _gla_level_tables()
    const3 = lambda b, h, s: (0, 0, 0)
    in_specs = [
        pl.BlockSpec(mats.shape, const3),
        pl.BlockSpec(masks.shape, lambda b, h, s: (0, 0, 0, 0)),
        pl.BlockSpec((2, LANES, GLA_DK), lambda b, h, s: (0, 0, h)),
        pl.BlockSpec((2, 1, GLA_DK), lambda b, h, s: (0, 0, h)),
    ]
    for d in range(2):
        in_specs += [_scan_spec(GLA_DK, d, n_batch), _scan_spec(GLA_DK, d, n_batch),
                     _scan_spec(GLA_DV, d, n_batch), _scan_spec_all(LANES, d, n_batch)]
    return pl.pallas_call(
        _gla_kernel,
        grid=(n_batch, GLA_HEADS, SCAN_STEPS),
        in_specs=in_specs,
        out_specs=[_scan_spec(GLA_DV, 0, n_batch), _scan_spec(GLA_DV, 1, n_batch)],
        out_shape=[jax.ShapeDtypeStruct((n_rows, D), F32)] * 2,
        scratch_shapes=[pltpu.VMEM((2, GLA_DV, GLA_DK), F32)],
        compiler_params=_cparams(3),
        name="gla_scan",
    )(mats, masks, wup, bg, q, k, v, low, q, k, v, low)


def _gla_mixer(h, mod3, g, w_in, w_gate_up, b_gate, n_batch):
    dk_t = GLA_HEADS * GLA_DK
    n_low = 2 * GLA_RANK
    wk, wv, wl, wq, wr = (w_in[:, :dk_t], w_in[:, dk_t:dk_t + D], w_in[:, dk_t + D:dk_t + D + n_low],
                          w_in[:, dk_t + D + n_low:2 * dk_t + D + n_low], w_in[:, 2 * dk_t + D + n_low:])
    w = jnp.concatenate([wk, wq, wv, wr, jnp.pad(wl, ((0, 0), (0, LANES - n_low)))], axis=1).astype(BF16)
    c = [0, dk_t, 2 * dk_t, 2 * dk_t + D, 2 * dk_t + 2 * D, 2 * dk_t + 2 * D + LANES]
    segs = [(c[0], c[1], 1.0, F32), (c[1], c[2], GLA_DK ** -0.5, F32), (c[2], c[3], 1.0, BF16),
            (c[3], c[4], 1.0, F32), (c[4], c[5], 1.0, F32)]
    k, q, v, rg, low = _proj(h, mod3, g, w, segs, n_batch)
    wup = jnp.zeros((2, LANES, dk_t), F32)
    for r in range(2):
        wup = wup.at[r, r * GLA_RANK:(r + 1) * GLA_RANK].set(w_gate_up[r].astype(F32))
    of, ob = _gla_scan(q, k, v, low, wup.astype(BF16), b_gate.astype(F32).reshape(2, 1, dk_t), n_batch)
    return of, ob, rg


def kernel(x, c, ctx, c_ctx, ada_w, ada_b, norm_g, ffn_w13, ffn_w2, final_g, na_w_kvq, na_rpb, na_w_o, ml_w_in, ml_gate_b, ml_conv_w, ml_norm_g, ml_w_o, da_w_kvq, da_lam, da_norm_g, da_w_o, gla_w_in, gla_w_gate_up, gla_b_gate, gla_norm_g, gla_w_o):
    nb = x.shape[0]
    assert x.shape[1:] == (N_LAT, D) and ctx.shape[1:] == (N_CTX, D) and nb < MOD_ROWS
    n_lat = nb * N_LAT
    n_tok = n_lat + nb * N_CTX

    s = jnp.zeros((MOD_ROWS, D), F32).at[:nb].set(c).at[nb].set(c_ctx)
    modtab = _mod_tables(s, ada_w, ada_b)
    h = jnp.concatenate([x.reshape(n_lat, D), ctx.reshape(nb * N_CTX, D)], axis=0)

    for i in range(DEPTH):
        kind, j = i % 4, i // 4
        last = i == DEPTH - 1
        mod3 = modtab[i].reshape(MOD_ROWS * N_MOD, 1, D)
        h = _ffn(h, mod3, norm_g[i, 0], ffn_w13[i, 0].astype(BF16), ffn_w2[i, 0].astype(BF16), final_g, 0, n_tok, nb)
        n_out = n_lat if last else n_tok
        if kind == 0:
            ins = _na_mixer(h, mod3, norm_g[i, 1], na_w_kvq[j], na_rpb[j], nb)
            h = _out_proj(h, mod3, na_w_o[j].astype(BF16), ins, n_out, nb, "plain")
        elif kind == 1:
            ins = _ml_mixer(h, mod3, norm_g[i, 1], ml_w_in[j], ml_gate_b[j], ml_conv_w[j], nb)
            h = _out_proj(h, mod3, ml_w_o[j].astype(BF16), ins, n_out, nb, "ml", ml_norm_g[j])
        elif kind == 2:
            ins = _da_mixer(h, mod3, norm_g[i, 1], da_w_kvq[j], da_lam[j], da_norm_g[j], i, nb)
            h = _out_proj(h, mod3, da_w_o[j].astype(BF16), ins, n_out, nb, "plain")
        else:
            ins = _gla_mixer(h, mod3, norm_g[i, 1], gla_w_in[j], gla_w_gate_up[j], gla_b_gate[j], nb)
            h = _out_proj(h, mod3, gla_w_o[j].astype(BF16), ins, n_out, nb, "gla", gla_norm_g[j])
        h = _ffn(h, mod3, norm_g[i, 2], ffn_w13[i, 1].astype(BF16), ffn_w2[i, 1].astype(BF16), final_g, 6, n_out, nb,
                 final=last)
    return h.reshape(nb, N_LAT, D)
```

```python
import functools

import numpy as np
import jax
import jax.numpy as jnp
from jax import lax
from jax.experimental import pallas as pl
from jax.experimental.pallas import tpu as pltpu

F32 = jnp.float32
BF16 = jnp.bfloat16

D = 1024
N_LAT = 2048
N_CTX = 256
DEPTH = 4
N_MOD = 9
D_FF = 2816
EPS = 1e-6
NEG = -1e30
GRID_W = 64

NA_HEADS = 16
NA_HD = 64
NA_WIN_R = 8
NA_WIN_C = 16

ML_HEADS = 4
ML_HD = 256

DA_HEADS = 8
DA_HD = 64
ROPE_BASE = 10000.0

GLA_HEADS = 4
GLA_DK = 128
GLA_DV = 256
GLA_RANK = 16
GLA_TAU = 16.0

ROW_BLK = 256
LAT_BLKS = N_LAT // ROW_BLK
MOD_ROWS = 16
LANES = 128
VMEM_LIMIT = 56 * 1024 * 1024


def _cparams(n_axes, vmem=VMEM_LIMIT):
    return pltpu.CompilerParams(dimension_semantics=("arbitrary",) * n_axes, vmem_limit_bytes=vmem)


def _sigmoid(x):
    return 1.0 / (1.0 + jnp.exp(-x))


def _silu(x):
    return x * _sigmoid(x)


def _log_sigmoid(x):
    return jnp.minimum(x, 0.0) - jnp.log1p(jnp.exp(-jnp.abs(x)))


def _rms(x, g):
    return x * lax.rsqrt(jnp.mean(x * x, axis=-1, keepdims=True) + EPS) * g


def _dot(a, b):
    return jnp.dot(a, b, preferred_element_type=F32)


def _dot_nt(a, b):
    return lax.dot_general(a, b, (((1,), (1,)), ((), ())), preferred_element_type=F32)


def _dot_tn(a, b):
    return lax.dot_general(a, b, (((0,), (0,)), ((), ())), preferred_element_type=F32)


def _mod_row(i, tm, n_batch):
    return jnp.minimum(i // (N_LAT // tm), n_batch)


def _mod_spec(k, tm, n_batch):
    return pl.BlockSpec((1, 1, D), lambda i: (_mod_row(i, tm, n_batch) * N_MOD + k, 0, 0))


def _mod_kernel(s_ref, w_ref, b_ref, o_ref):
    a = _silu(s_ref[...]).astype(BF16)
    o_ref[0] = _dot(a, w_ref[0].astype(BF16)) + b_ref[0]


def _mod_tables(s, ada_w, ada_b):
    return pl.pallas_call(
        _mod_kernel,
        grid=(DEPTH, N_MOD),
        in_specs=[
            pl.BlockSpec((MOD_ROWS, D), lambda i, k: (0, 0)),
            pl.BlockSpec((1, D, D), lambda i, k: (i, 0, k)),
            pl.BlockSpec((1, 1, D), lambda i, k: (i, 0, k)),
        ],
        out_specs=pl.BlockSpec((1, MOD_ROWS, D), lambda i, k: (i, 0, k)),
        out_shape=jax.ShapeDtypeStruct((DEPTH, MOD_ROWS, N_MOD * D), F32),
        compiler_params=_cparams(2),
        name="mod_tables",
    )(s, ada_w, ada_b.reshape(DEPTH, 1, N_MOD * D))


FFN_TM = 512
FFN_CHUNK = D_FF // 2


def _ffn_kernel(h_ref, g_ref, sh_ref, sc_ref, gt_ref, w13_ref, w2_ref, fg_ref, o_ref, *, final):
    x = h_ref[...]
    a = (_rms(x, g_ref[...]) * (1.0 + sc_ref[0]) + sh_ref[0]).astype(BF16)
    acc = jnp.zeros(x.shape, F32)
    for c0 in range(0, D_FF, FFN_CHUNK):
        gate = _dot(a, w13_ref[:, c0:c0 + FFN_CHUNK])
        up = _dot(a, w13_ref[:, D_FF + c0:D_FF + c0 + FFN_CHUNK])
        act = (_silu(gate) * up).astype(BF16)
        acc = acc + _dot(act, w2_ref[c0:c0 + FFN_CHUNK, :])
    y = x + 0.5 * gt_ref[0] * acc
    if final:
        y = _rms(y, fg_ref[...])
    o_ref[...] = y


def _ffn(h, mod3, g, w13, w2, final_g, k0, n_rows, n_batch, final=False):
    tm = FFN_TM
    const = lambda i: (0, 0)
    return pl.pallas_call(
        functools.partial(_ffn_kernel, final=final),
        grid=(n_rows // tm,),
        in_specs=[
            pl.BlockSpec((tm, D), lambda i: (i, 0)),
            pl.BlockSpec((1, D), const),
            _mod_spec(k0, tm, n_batch),
            _mod_spec(k0 + 1, tm, n_batch),
            _mod_spec(k0 + 2, tm, n_batch),
            pl.BlockSpec((D, 2 * D_FF), const, pipeline_mode=pl.Buffered(1)),
            pl.BlockSpec((D_FF, D), const, pipeline_mode=pl.Buffered(1)),
            pl.BlockSpec((1, D), const),
        ],
        out_specs=pl.BlockSpec((tm, D), lambda i: (i, 0)),
        out_shape=jax.ShapeDtypeStruct((n_rows, D), F32),
        compiler_params=_cparams(1),
        name="ffn",
    )(h, g.reshape(1, D), mod3, mod3, mod3, w13, w2, final_g.reshape(1, D))


PROJ_TM = 256


def _proj_kernel(h_ref, g_ref, sh_ref, sc_ref, w_ref, *o_refs, segs):
    x = h_ref[...]
    a = (_rms(x, g_ref[...]) * (1.0 + sc_ref[0]) + sh_ref[0]).astype(BF16)
    for o_ref, (c0, c1, scale) in zip(o_refs, segs):
        y = _dot(a, w_ref[:, c0:c1])
        if scale != 1.0:
            y = y * scale
        o_ref[...] = y.astype(o_ref.dtype)


def _proj(h, mod3, g, w, segs, n_batch):
    tm = PROJ_TM
    n_rows = h.shape[0]
    const = lambda i: (0, 0)
    return pl.pallas_call(
        functools.partial(_proj_kernel, segs=tuple((c0, c1, sc) for c0, c1, sc, _ in segs)),
        grid=(n_rows // tm,),
        in_specs=[
            pl.BlockSpec((tm, D), lambda i: (i, 0)),
            pl.BlockSpec((1, D), const),
            _mod_spec(3, tm, n_batch),
            _mod_spec(4, tm, n_batch),
            pl.BlockSpec(w.shape, const, pipeline_mode=pl.Buffered(1)),
        ],
        out_specs=[pl.BlockSpec((tm, c1 - c0), lambda i: (i, 0)) for c0, c1, _, _ in segs],
        out_shape=[jax.ShapeDtypeStruct((n_rows, c1 - c0), dt) for c0, c1, _, dt in segs],
        compiler_params=_cparams(1),
        name="mixer_proj",
    )(h, g.reshape(1, D), mod3, mod3, w)


OUT_TM = 256


def _head_rms(x, g, n_heads):
    hd = D // n_heads
    parts = []
    for i in range(n_heads):
        xi = x[:, i * hd:(i + 1) * hd]
        parts.append(xi * lax.rsqrt(jnp.mean(xi * xi, axis=-1, keepdims=True) + EPS))
    return jnp.concatenate(parts, axis=-1) * g


def _out_kernel(h_ref, gt_ref, w_ref, *refs, mode, n_lat_tiles):
    o_ref = refs[-1]
    if mode == "plain":
        y_lat, y_ctx = refs[:2]
        y = jnp.where(pl.program_id(0) < n_lat_tiles, y_lat[...], y_ctx[...])
    elif mode == "ml":
        hf, hb, og, ng = refs[:4]
        y = _sigmoid(og[...]) * _head_rms(hf[...] + hb[...], ng[...], ML_HEADS)
    else:
        of, ob, rg, ng = refs[:4]
        y = _head_rms(of[...] + ob[...], ng[...], GLA_HEADS) * _silu(rg[...])
    o_ref[...] = h_ref[...] + gt_ref[0] * _dot(y.astype(BF16), w_ref[...])


def _out_proj(h, mod3, w_o, ins, n_rows, n_batch, mode, norm_g=None):
    tm = OUT_TM
    nl = n_batch * N_LAT // tm
    const = lambda i: (0, 0)
    row = pl.BlockSpec((tm, D), lambda i: (i, 0))
    if mode == "plain":
        extra = list(ins)
        extra_specs = [pl.BlockSpec((tm, D), lambda i: (jnp.minimum(i, nl - 1), 0)),
                       pl.BlockSpec((tm, D), lambda i: (jnp.maximum(i - nl, 0), 0))]
    else:
        extra = list(ins) + [norm_g.reshape(1, D)]
        extra_specs = [row] * len(ins) + [pl.BlockSpec((1, D), const)]
    return pl.pallas_call(
        functools.partial(_out_kernel, mode=mode, n_lat_tiles=nl),
        grid=(n_rows // tm,),
        in_specs=[row, _mod_spec(5, tm, n_batch), pl.BlockSpec((D, D), const)] + extra_specs,
        out_specs=row,
        out_shape=jax.ShapeDtypeStruct((n_rows, D), F32),
        compiler_params=_cparams(1),
        name="mixer_out",
    )(h, mod3, w_o, *extra)


def _lane_lo(shape):
    return lax.broadcasted_iota(jnp.int32, shape, 1) < (LANES // 2)


def _half_masked(x, lo, head):
    return jnp.where(lo if head == 0 else jnp.logical_not(lo), x, jnp.zeros_like(x))


def _pair_attend(qp, segs, biases):
    tq = qp.shape[0]
    lo = _lane_lo(qp.shape)
    q2 = jnp.concatenate([_half_masked(qp, lo, 0), _half_masked(qp, lo, 1)], axis=0)
    scores = []
    for (k, _), b in zip(segs, biases):
        s = _dot_nt(q2, k)
        scores.append(s if b is None else s + b)
    m = scores[0].max(axis=-1, keepdims=True)
    for s in scores[1:]:
        m = jnp.maximum(m, s.max(axis=-1, keepdims=True))
    den = None
    acc = None
    for s, (_, v) in zip(scores, segs):
        p = jnp.exp(s - m)
        ps = p.sum(axis=-1, keepdims=True)
        pv = _dot(p.astype(BF16), v)
        den = ps if den is None else den + ps
        acc = pv if acc is None else acc + pv
    o2 = acc / den
    return jnp.where(lo, o2[:tq], o2[tq:])


NA_ROWS = N_LAT // GRID_W
NA_QROWS = 4
NA_KROWS = 12
NA_NKEY = NA_KROWS * GRID_W
NA_NOFF = 2 * NA_WIN_R - 1
NA_NENT = NA_NOFF + 1


def _na_bias_tables(rpb):
    c = np.arange(GRID_W)
    win0 = np.clip(c - NA_WIN_C // 2, 0, GRID_W - NA_WIN_C)
    kc = np.arange(GRID_W)
    in_win = (kc[None, :] >= win0[:, None]) & (kc[None, :] < win0[:, None] + NA_WIN_C)
    col_off = np.clip(kc[None, :] - c[:, None], 1 - NA_WIN_C, NA_WIN_C - 1) + NA_WIN_C - 1
    t = rpb.astype(F32)[:, :, col_off]
    t = jnp.where(in_win[None, None], t, NEG)
    t = jnp.pad(t, ((0, 0), (1, 1), (0, 0), (0, 0)))
    return jnp.concatenate([t[:, :NA_NENT], t[:, 1:]], axis=-1)


def _na_key_row0(j):
    return jnp.clip(NA_QROWS * j - NA_WIN_R // 2, 0, NA_ROWS - NA_KROWS)


def _na_kernel(q_ref, k_ref, v_ref, kc_ref, vc_ref, t_ref, o_ref):
    j = pl.program_id(1)
    u0 = _na_key_row0(j)
    start = pl.multiple_of(u0 * GRID_W, GRID_W)
    lo = _lane_lo((1, LANES))
    entry = []
    rowmask = []
    for i in range(NA_QROWS):
        r = NA_QROWS * j + i
        r0 = jnp.clip(r - NA_WIN_R // 2, 0, NA_ROWS - NA_WIN_R)
        entry.append([])
        rowmask.append([])
        for p in range(NA_KROWS // 2):
            a = u0 + 2 * p
            entry[i].append(jnp.clip(a - r + NA_WIN_R, 0, NA_NENT - 1))
            out_a = jnp.logical_or(a < r0, a >= r0 + NA_WIN_R)
            out_b = jnp.logical_or(a + 1 < r0, a + 1 >= r0 + NA_WIN_R)
            rowmask[i].append(jnp.where(lo, jnp.where(out_a, NEG, 0.0), jnp.where(out_b, NEG, 0.0)))
    for hp in range(NA_HEADS // 2):
        cols = slice(hp * LANES, (hp + 1) * LANES)
        bias_rows = []
        for head in range(2):
            for i in range(NA_QROWS):
                tiles = [t_ref[2 * hp + head, entry[i][p]] + rowmask[i][p] for p in range(NA_KROWS // 2)]
                bias_rows.append(jnp.concatenate(tiles, axis=1))
        bias = jnp.concatenate(bias_rows, axis=0)
        segs = [(k_ref[pl.ds(start, NA_NKEY), cols], v_ref[pl.ds(start, NA_NKEY), cols]),
                (kc_ref[:, cols], vc_ref[:, cols])]
        o_ref[:, cols] = _pair_attend(q_ref[:, cols], segs, [bias, None]).astype(o_ref.dtype)


def _na_attention(q, k, v, table, n_batch):
    ctx0 = n_batch * LAT_BLKS
    n_steps = NA_ROWS // NA_QROWS
    tq = NA_QROWS * GRID_W
    lat = pl.BlockSpec((N_LAT, D), lambda b, j: (b, 0))
    ctx = pl.BlockSpec((N_CTX, D), lambda b, j: (ctx0 + b, 0))
    return pl.pallas_call(
        _na_kernel,
        grid=(n_batch, n_steps),
        in_specs=[
            pl.BlockSpec((tq, D), lambda b, j: (b * n_steps + j, 0)),
            lat, lat, ctx, ctx,
            pl.BlockSpec(table.shape, lambda b, j: (0, 0, 0, 0), pipeline_mode=pl.Buffered(1)),
        ],
        out_specs=pl.BlockSpec((tq, D), lambda b, j: (b * n_steps + j, 0)),
        out_shape=jax.ShapeDtypeStruct((n_batch * N_LAT, D), BF16),
        compiler_params=_cparams(2),
        name="na_attention",
    )(q, k, v, k, v, table)


def _ctx_attn_kernel(q_ref, k_ref, v_ref, o_ref):
    for hp in range(NA_HEADS // 2):
        cols = slice(hp * LANES, (hp + 1) * LANES)
        segs = [(k_ref[:, cols], v_ref[:, cols])]
        o_ref[:, cols] = _pair_attend(q_ref[:, cols], segs, [None]).astype(o_ref.dtype)


def _ctx_attention(q, k, v, n_batch):
    ctx0 = n_batch * LAT_BLKS
    spec = pl.BlockSpec((N_CTX, D), lambda b: (ctx0 + b, 0))
    return pl.pallas_call(
        _ctx_attn_kernel,
        grid=(n_batch,),
        in_specs=[spec, spec, spec],
        out_specs=pl.BlockSpec((N_CTX, D), lambda b: (b, 0)),
        out_shape=jax.ShapeDtypeStruct((n_batch * N_CTX, D), BF16),
        compiler_params=_cparams(1),
        name="na_ctx_attention",
    )(q, k, v)


def _na_mixer(h, mod3, g, w_kvq, rpb, n_batch):
    segs = [(0, D, 1.0, BF16), (D, 2 * D, 1.0, BF16), (2 * D, 3 * D, NA_HD ** -0.5, BF16)]
    k, v, q = _proj(h, mod3, g, w_kvq.astype(BF16), segs, n_batch)
    return _na_attention(q, k, v, _na_bias_tables(rpb), n_batch), _ctx_attention(q, k, v, n_batch)


DA_TQ = 256
DA_W = 2 * DA_HD


def _rope_tables():
    t = np.arange(N_LAT)
    row = (t // GRID_W).astype(np.float64)
    col = (t % GRID_W).astype(np.float64)
    per_axis = DA_HD // 2
    freqs = ROPE_BASE ** (-np.arange(0, per_axis, 2, dtype=np.float64) / per_axis)
    ar = row[:, None] * freqs
    ac = col[:, None] * freqs
    ang = np.concatenate([ar, ar, ac, ac], axis=-1)
    cos = np.tile(np.cos(ang), (1, 2))
    sin = np.tile(np.sin(ang), (1, 2))
    quarter = (np.arange(DA_W) % DA_HD) // (DA_HD // 4)
    even = (quarter % 2 == 0)[None, :]
    sin_a = np.where(even, -sin, 0.0)
    sin_b = np.where(even, 0.0, sin)
    return tuple(jnp.asarray(a, F32) for a in (cos, sin_a, sin_b))


def _rope(x, cos, sin_a, sin_b):
    q16 = DA_HD // 4
    return x * cos + pltpu.roll(x, DA_W - q16, 1) * sin_a + pltpu.roll(x, q16, 1) * sin_b


def _da_lambda(lp, lam_init):
    a = jnp.sum(lp[0:1] * lp[1:2], axis=-1, keepdims=True)
    b = jnp.sum(lp[2:3] * lp[3:4], axis=-1, keepdims=True)
    return jnp.exp(a) - jnp.exp(b) + lam_init


def _diff_attend(q, segs, lam, ng, lam_init):
    lo = _lane_lo(q.shape)
    weights = [None] * len(segs)
    for j in range(2):
        qm = _half_masked(q, lo, j)
        scores = [_dot_nt(qm, k) for k, _ in segs]
        m = scores[0].max(axis=-1, keepdims=True)
        for s in scores[1:]:
            m = jnp.maximum(m, s.max(axis=-1, keepdims=True))
        es = [jnp.exp(s - m) for s in scores]
        den = es[0].sum(axis=-1, keepdims=True)
        for e in es[1:]:
            den = den + e.sum(axis=-1, keepdims=True)
        coef = (1.0 / den) if j == 0 else (-lam / den)
        for si, e in enumerate(es):
            weights[si] = e * coef if j == 0 else weights[si] + e * coef
    o = None
    for w, (_, v) in zip(weights, segs):
        pv = _dot(w.astype(BF16), v)
        o = pv if o is None else o + pv
    return o * lax.rsqrt(jnp.mean(o * o, axis=-1, keepdims=True) + EPS) * ng * (1.0 - lam_init)


def _da_kernel(lam_ref, q_ref, k_ref, v_ref, kc_ref, vc_ref, cq_ref, saq_ref, sbq_ref,
               ck_ref, sak_ref, sbk_ref, ng_ref, o_ref, kr_ref, *, lam_init):
    @pl.when(pl.program_id(2) == 0)
    def _():
        kr_ref[...] = _rope(k_ref[...], ck_ref[...], sak_ref[...], sbk_ref[...]).astype(BF16)

    lam = _da_lambda(lam_ref[...], lam_init)
    q = _rope(q_ref[...], cq_ref[...], saq_ref[...], sbq_ref[...]).astype(BF16)
    segs = [(kr_ref[...], v_ref[...]), (kc_ref[...].astype(BF16), vc_ref[...])]
    o_ref[...] = _diff_attend(q, segs, lam, ng_ref[...], lam_init).astype(o_ref.dtype)


def _da_attention(q, k, v, lam_p, norm_g, lam_init, n_batch):
    cos, sin_a, sin_b = _rope_tables()
    nqb = N_LAT // DA_TQ
    ctx0 = n_batch * LAT_BLKS
    tab_q = pl.BlockSpec((DA_TQ, DA_W), lambda b, h, i: (i, 0))
    tab_k = pl.BlockSpec((N_LAT, DA_W), lambda b, h, i: (0, 0))
    lat = pl.BlockSpec((N_LAT, DA_W), lambda b, h, i: (b, h))
    ctx = pl.BlockSpec((N_CTX, DA_W), lambda b, h, i: (ctx0 + b, h))
    return pl.pallas_call(
        functools.partial(_da_kernel, lam_init=lam_init),
        grid=(n_batch, DA_HEADS, nqb),
        in_specs=[
            pl.BlockSpec((4, DA_HD), lambda b, h, i: (0, 0)),
            pl.BlockSpec((DA_TQ, DA_W), lambda b, h, i: (b * nqb + i, h)),
            lat, lat, ctx, ctx,
            tab_q, tab_q, tab_q, tab_k, tab_k, tab_k,
            pl.BlockSpec((1, DA_W), lambda b, h, i: (0, h)),
        ],
        out_specs=pl.BlockSpec((DA_TQ, DA_W), lambda b, h, i: (b * nqb + i, h)),
        out_shape=jax.ShapeDtypeStruct((n_batch * N_LAT, D), BF16),
        scratch_shapes=[pltpu.VMEM((N_LAT, DA_W), BF16)],
        compiler_params=_cparams(3),
        name="diff_attention",
    )(lam_p, q, k, v, k, v, cos, sin_a, sin_b, cos, sin_a, sin_b, norm_g.reshape(1, D))


def _da_ctx_kernel(lam_ref, q_ref, k_ref, v_ref, ng_ref, o_ref, *, lam_init):
    lam = _da_lambda(lam_ref[...], lam_init)
    segs = [(k_ref[...].astype(BF16), v_ref[...])]
    o_ref[...] = _diff_attend(q_ref[...].astype(BF16), segs, lam, ng_ref[...], lam_init).astype(o_ref.dtype)


def _da_ctx_attention(q, k, v, lam_p, norm_g, lam_init, n_batch):
    ctx0 = n_batch * LAT_BLKS
    spec = pl.BlockSpec((N_CTX, DA_W), lambda b, h: (ctx0 + b, h))
    return pl.pallas_call(
        functools.partial(_da_ctx_kernel, lam_init=lam_init),
        grid=(n_batch, DA_HEADS),
        in_specs=[pl.BlockSpec((4, DA_HD), lambda b, h: (0, 0)), spec, spec, spec,
                  pl.BlockSpec((1, DA_W), lambda b, h: (0, h))],
        out_specs=pl.BlockSpec((N_CTX, DA_W), lambda b, h: (b, h)),
        out_shape=jax.ShapeDtypeStruct((n_batch * N_CTX, D), BF16),
        compiler_params=_cparams(2),
        name="diff_ctx_attention",
    )(lam_p, q, k, v, norm_g.reshape(1, D))


def _da_mixer(h, mod3, g, w_kvq, lam_p, norm_g, layer_idx, n_batch):
    lam_init = 0.8 - 0.6 * float(np.exp(-0.3 * layer_idx))
    segs = [(0, D, 1.0, F32), (D, 2 * D, 1.0, BF16), (2 * D, 3 * D, DA_HD ** -0.5, F32)]
    k, v, q = _proj(h, mod3, g, w_kvq.astype(BF16), segs, n_batch)
    lam_p = lam_p.astype(F32)
    return (_da_attention(q, k, v, lam_p, norm_g, lam_init, n_batch),
            _da_ctx_attention(q, k, v, lam_p, norm_g, lam_init, n_batch))


SCAN_STEPS = 1 + LAT_BLKS


def _scan_row_blk(direction, n_batch):
    ctx0 = n_batch * LAT_BLKS

    def blk(b, s):
        lat = b * LAT_BLKS + (s - 1 if direction == 0 else LAT_BLKS - s)
        return jnp.where(s == 0, ctx0 + b, lat)
    return blk


def _scan_spec(width, direction, n_batch):
    blk = _scan_row_blk(direction, n_batch)
    return pl.BlockSpec((ROW_BLK, width), lambda b, h, s: (blk(b, s), h))


def _scan_spec_all(width, direction, n_batch):
    blk = _scan_row_blk(direction, n_batch)
    return pl.BlockSpec((ROW_BLK, width), lambda b, h, s: (blk(b, s), 0))


def _ml_conv_kernel(xk_ref, xkp_ref, xkn_ref, xq_ref, xqp_ref, xqn_ref, w_ref, ok_ref, oq_ref, *, n_lat_blks):
    i = pl.program_id(0)
    is_ctx = i >= n_lat_blks
    first = jnp.logical_or(is_ctx, i % LAT_BLKS == 0)
    last = jnp.logical_or(is_ctx, i % LAT_BLKS == LAT_BLKS - 1)
    rid = lax.broadcasted_iota(jnp.int32, (ROW_BLK, D), 0)
    groups = ((xk_ref, xkp_ref, xkn_ref, 0, ok_ref, ML_HD ** -0.5), (xq_ref, xqp_ref, xqn_ref, D, oq_ref, 1.0))
    for x_ref, xp_ref, xn_ref, c0, o_ref, scale in groups:
        x = x_ref[...]
        prev_row = jnp.where(first, 0.0, xp_ref[7:8, :])
        next_row = jnp.where(last, 0.0, xn_ref[0:1, :])
        x_prev = jnp.where(rid == 0, prev_row, pltpu.roll(x, 1, 0))
        x_next = jnp.where(rid == ROW_BLK - 1, next_row, pltpu.roll(x, ROW_BLK - 1, 0))
        y = w_ref[0:1, c0:c0 + D] * x_prev + w_ref[1:2, c0:c0 + D] * x + w_ref[2:3, c0:c0 + D] * x_next
        o_ref[...] = (_silu(y) * scale).astype(o_ref.dtype)


def _ml_conv(pk, pq, conv_w, n_batch):
    n_rows = pk.shape[0]
    n_blk = n_rows // ROW_BLK
    sub = ROW_BLK // 8
    main = pl.BlockSpec((ROW_BLK, D), lambda i: (i, 0))
    prev = pl.BlockSpec((8, D), lambda i: (jnp.maximum(i * sub - 1, 0), 0))
    nxt = pl.BlockSpec((8, D), lambda i: (jnp.minimum((i + 1) * sub, n_blk * sub - 1), 0))
    return pl.pallas_call(
        functools.partial(_ml_conv_kernel, n_lat_blks=n_batch * LAT_BLKS),
        grid=(n_blk,),
        in_specs=[main, prev, nxt, main, prev, nxt, pl.BlockSpec((3, 2 * D), lambda i: (0, 0))],
        out_specs=[main, main],
        out_shape=[jax.ShapeDtypeStruct((n_rows, D), BF16)] * 2,
        compiler_params=_cparams(1),
        name="mlstm_conv",
    )(pk, pk, pk, pq, pq, pq, conv_w)


def _pick_lane(x, idx):
    lane = lax.broadcasted_iota(jnp.int32, x.shape, 1)
    return jnp.sum(jnp.where(lane == idx, x, 0.0), axis=1, keepdims=True)


def _pick_sublane(x, idx):
    sub = lax.broadcasted_iota(jnp.int32, x.shape, 0)
    return jnp.sum(jnp.where(sub == idx, x, 0.0), axis=0, keepdims=True)


def _ml_scan_kernel(gb_ref, g0_ref, q0_ref, k0_ref, v0_ref, g1_ref, q1_ref, k1_ref, v1_ref,
                    h0_ref, h1_ref, c_ref, n_ref, m_ref):
    head = pl.program_id(1)

    @pl.when(pl.program_id(2) == 0)
    def _():
        c_ref[...] = jnp.zeros_like(c_ref)
        n_ref[...] = jnp.zeros_like(n_ref)
        m_ref[...] = jnp.zeros_like(m_ref)

    r = lax.broadcasted_iota(jnp.int32, (ROW_BLK, ROW_BLK), 0)
    c = lax.broadcasted_iota(jnp.int32, (ROW_BLK, ROW_BLK), 1)
    dirs = ((g0_ref, q0_ref, k0_ref, v0_ref, h0_ref), (g1_ref, q1_ref, k1_ref, v1_ref, h1_ref))
    for d, (g_ref, q_ref, k_ref, v_ref, o_ref) in enumerate(dirs):
        valid = (r >= c) if d == 0 else (r <= c)
        other = (r <= c) if d == 0 else (r >= c)
        g = g_ref[...] + gb_ref[...]
        gt = g.T
        ii = d * 2 * ML_HEADS + head
        fi = ii + ML_HEADS
        i_col = _pick_lane(g, ii)
        f_col = _log_sigmoid(_pick_lane(g, fi))
        i_row = _pick_sublane(gt, ii)
        f_row = _log_sigmoid(_pick_sublane(gt, fi))
        b_col = jnp.sum(jnp.where(valid, f_row, 0.0), axis=1, keepdims=True)
        b_row = jnp.sum(jnp.where(other, f_col, 0.0), axis=0, keepdims=True)
        total = jnp.sum(f_col, axis=0, keepdims=True)
        m_prev = m_ref[d][0:1, 0:1]
        n_prev = n_ref[d][0:1, :]
        c_prev = c_ref[d]
        qc = q_ref[...]
        kc = k_ref[...]
        vc = v_ref[...]

        dmat = jnp.where(valid, b_col - b_row + i_row, NEG)
        inter = b_col + m_prev
        m_t = jnp.maximum(inter, dmat.max(axis=1, keepdims=True))
        w_inter = jnp.exp(inter - m_t)
        s = _dot_nt(qc, kc) * jnp.exp(dmat - m_t)
        num = w_inter * _dot(qc, c_prev.astype(BF16)) + _dot(s.astype(BF16), vc)
        den = w_inter * jnp.sum(qc.astype(F32) * n_prev, axis=1, keepdims=True) + s.sum(axis=1, keepdims=True)
        o_ref[...] = num / jnp.maximum(jnp.abs(den), jnp.exp(-m_t))

        g_col = total - b_col + i_col
        m_new = jnp.maximum(total + m_prev, g_col.max(axis=0, keepdims=True))
        w_old = jnp.exp(total + m_prev - m_new)
        kw = kc.astype(F32) * jnp.exp(g_col - m_new)
        c_ref[d] = w_old * c_prev + _dot_tn(kw.astype(BF16), vc)
        n_ref[d] = jnp.broadcast_to(w_old * n_prev + kw.sum(axis=0, keepdims=True), n_ref.shape[1:])
        m_ref[d] = jnp.broadcast_to(m_new, m_ref.shape[1:])


def _ml_scan(q, k, v, gates, gate_b, n_batch):
    n_rows = q.shape[0]
    in_specs = [pl.BlockSpec((1, LANES), lambda b, h, s: (0, 0))]
    for d in range(2):
        in_specs += [_scan_spec_all(LANES, d, n_batch)] + [_scan_spec(ML_HD, d, n_batch)] * 3
    return pl.pallas_call(
        _ml_scan_kernel,
        grid=(n_batch, ML_HEADS, SCAN_STEPS),
        in_specs=in_specs,
        out_specs=[_scan_spec(ML_HD, 0, n_batch), _scan_spec(ML_HD, 1, n_batch)],
        out_shape=[jax.ShapeDtypeStruct((n_rows, D), F32)] * 2,
        scratch_shapes=[pltpu.VMEM((2, ML_HD, ML_HD), F32), pltpu.VMEM((2, 8, ML_HD), F32),
                        pltpu.VMEM((2, 8, LANES), F32)],
        compiler_params=_cparams(3),
        name="mlstm_scan",
    )(gate_b, gates, q, k, v, gates, q, k, v)


def _ml_mixer(h, mod3, g, w_in, gate_b, conv_w, n_batch):
    n_g = 4 * ML_HEADS
    wk, wv, wg, wq, wo = (w_in[:, :D], w_in[:, D:2 * D], w_in[:, 2 * D:2 * D + n_g],
                          w_in[:, 2 * D + n_g:3 * D + n_g], w_in[:, 3 * D + n_g:])
    w = jnp.concatenate([wk, wv, wq, wo, jnp.pad(wg, ((0, 0), (0, LANES - n_g)))], axis=1).astype(BF16)
    segs = [(0, D, 1.0, F32), (D, 2 * D, 1.0, BF16), (2 * D, 3 * D, 1.0, F32), (3 * D, 4 * D, 1.0, F32),
            (4 * D, 4 * D + LANES, 1.0, F32)]
    pk, v, pq, og, gates = _proj(h, mod3, g, w, segs, n_batch)
    k, q = _ml_conv(pk, pq, conv_w.astype(F32), n_batch)
    gb = jnp.pad(gate_b.astype(F32), (0, LANES - n_g)).reshape(1, LANES)
    hf, hb = _ml_scan(q, k, v, gates, gb, n_batch)
    return hf, hb, og


GLA_LEVELS = (256, 128, 64, 32, 16, 8, 4, 2)
GLA_BCAST_LEVELS = 6
GLA_DIAG = len(GLA_LEVELS)


def _gla_tables():
    u = np.arange(ROW_BLK)
    tt, ss = np.meshgrid(u, u, indexing="ij")
    hb = np.floor(np.log2(np.maximum(tt ^ ss, 1))).astype(np.int64)
    level = len(GLA_LEVELS) - 1 - hb
    tri = np.stack([ss <= tt, ss >= tt]).astype(np.float32)
    lid = np.stack([np.where(tt == ss, GLA_DIAG, np.where(tt > ss, level, -1)),
                    np.where(tt == ss, GLA_DIAG, np.where(tt < ss, level, -1))]).astype(np.int32)
    return jnp.asarray(tri, BF16), jnp.asarray(lid)


def _split3(x):
    hi = x.astype(BF16)
    r1 = x - hi.astype(F32)
    mid = r1.astype(BF16)
    lo = (r1 - mid.astype(F32)).astype(BF16)
    return hi, mid, lo


def _gla_level_exponents(level, d, bc, bc_ref, la, la_prev, la_next, row):
    n = GLA_LEVELS[level]
    half = n // 2
    if level < GLA_BCAST_LEVELS:
        parts = []
        for j in range(ROW_BLK // n):
            mid = j * n + (half - 1 if d == 0 else half)
            parts.append(jnp.broadcast_to(bc_ref[mid:mid + 1, :], (n, GLA_DK)))
        x = bc - (parts[0] if len(parts) == 1 else jnp.concatenate(parts, axis=0))
        return jnp.minimum(x, 0.0), jnp.minimum(-x, 0.0)
    o = row & (n - 1)
    if n == 4:
        if d == 0:
            return (jnp.where(o == 2, la, jnp.where(o == 3, la + la_prev, 0.0)), jnp.where(o == 0, la_next, 0.0))
        return (jnp.where(o == 0, la + la_next, jnp.where(o == 1, la, 0.0)), jnp.where(o == 3, la_prev, 0.0))
    return jnp.where(o == (1 if d == 0 else 0), la, 0.0), None


def _gla_kernel(tri_ref, lid_ref, wup_ref, bg_ref, q0_ref, k0_ref, v0_ref, l0_ref, q1_ref, k1_ref, v1_ref, l1_ref,
                o0_ref, o1_ref, st_ref, bc_ref, att_ref):
    @pl.when(pl.program_id(2) == 0)
    def _():
        st_ref[...] = jnp.zeros_like(st_ref)

    row = lax.broadcasted_iota(jnp.int32, (ROW_BLK, GLA_DK), 0)
    dirs = ((q0_ref, k0_ref, v0_ref, l0_ref, o0_ref), (q1_ref, k1_ref, v1_ref, l1_ref, o1_ref))
    for d, (q_ref, k_ref, v_ref, low_ref, o_ref) in enumerate(dirs):
        pre = _dot(low_ref[...].astype(BF16), wup_ref[d]) + bg_ref[d]
        la = _log_sigmoid(pre) * (1.0 / GLA_TAU)
        hi, mid, lo = _split3(la)
        tri = tri_ref[d]
        bc = _dot(tri, hi) + _dot(tri, mid) + _dot(tri, lo)
        bc_ref[...] = bc
        la_prev = pltpu.roll(la, 1, 0)
        la_next = pltpu.roll(la, ROW_BLK - 1, 0)
        q = q_ref[...]
        k = k_ref[...]
        v = v_ref[...]
        lid = lid_ref[d]
        kb = k.astype(BF16)
        att_ref[...] = jnp.where(lid == GLA_DIAG, _dot_nt(q.astype(BF16), kb), 0.0)
        for level in range(len(GLA_LEVELS)):
            eq, ek = _gla_level_exponents(level, d, bc, bc_ref, la, la_prev, la_next, row)
            ql = (q * jnp.exp(eq)).astype(BF16)
            kl = kb if ek is None else (k * jnp.exp(ek)).astype(BF16)
            att_ref[...] = jnp.where(lid == level, _dot_nt(ql, kl), att_ref[...])
        st = st_ref[d]
        qd = (q * jnp.exp(bc)).astype(BF16)
        o_ref[...] = _dot(att_ref[...].astype(BF16), v) + _dot_nt(qd, st.astype(BF16))
        b_end = bc[ROW_BLK - 1:ROW_BLK] if d == 0 else bc[0:1]
        kd = (k * jnp.exp(b_end - bc)).astype(BF16)
        st_ref[d] = st * jnp.exp(b_end) + _dot_tn(v, kd)


def _gla_scan(q, k, v, low, wup, bg, n_batch):
    n_rows = q.shape[0]
    tri, lid = _gla_tables()
    const3 = lambda b, h, s: (0, 0, 0)
    in_specs = [
        pl.BlockSpec(tri.shape, const3),
        pl.BlockSpec(lid.shape, const3),
        pl.BlockSpec((2, LANES, GLA_DK), lambda b, h, s: (0, 0, h)),
        pl.BlockSpec((2, 1, GLA_DK), lambda b, h, s: (0, 0, h)),
    ]
    for d in range(2):
        in_specs += [_scan_spec(GLA_DK, d, n_batch), _scan_spec(GLA_DK, d, n_batch),
                     _scan_spec(GLA_DV, d, n_batch), _scan_spec_all(LANES, d, n_batch)]
    return pl.pallas_call(
        _gla_kernel,
        grid=(n_batch, GLA_HEADS, SCAN_STEPS),
        in_specs=in_specs,
        out_specs=[_scan_spec(GLA_DV, 0, n_batch), _scan_spec(GLA_DV, 1, n_batch)],
        out_shape=[jax.ShapeDtypeStruct((n_rows, D), F32)] * 2,
        scratch_shapes=[pltpu.VMEM((2, GLA_DV, GLA_DK), F32), pltpu.VMEM((ROW_BLK, GLA_DK), F32),
                        pltpu.VMEM((ROW_BLK, ROW_BLK), F32)],
        compiler_params=_cparams(3),
        name="gla_scan",
    )(tri, lid, wup, bg, q, k, v, low, q, k, v, low)


def _gla_mixer(h, mod3, g, w_in, w_gate_up, b_gate, n_batch):
    dk_t = GLA_HEADS * GLA_DK
    n_low = 2 * GLA_RANK
    wk, wv, wl, wq, wr = (w_in[:, :dk_t], w_in[:, dk_t:dk_t + D], w_in[:, dk_t + D:dk_t + D + n_low],
                          w_in[:, dk_t + D + n_low:2 * dk_t + D + n_low], w_in[:, 2 * dk_t + D + n_low:])
    w = jnp.concatenate([wk, wq, wv, wr, jnp.pad(wl, ((0, 0), (0, LANES - n_low)))], axis=1).astype(BF16)
    c = [0, dk_t, 2 * dk_t, 2 * dk_t + D, 2 * dk_t + 2 * D, 2 * dk_t + 2 * D + LANES]
    segs = [(c[0], c[1], 1.0, F32), (c[1], c[2], GLA_DK ** -0.5, F32), (c[2], c[3], 1.0, BF16),
            (c[3], c[4], 1.0, F32), (c[4], c[5], 1.0, F32)]
    k, q, v, rg, low = _proj(h, mod3, g, w, segs, n_batch)
    wup = jnp.zeros((2, LANES, dk_t), F32)
    for r in range(2):
        wup = wup.at[r, r * GLA_RANK:(r + 1) * GLA_RANK].set(w_gate_up[r].astype(F32))
    of, ob = _gla_scan(q, k, v, low, wup.astype(BF16), b_gate.astype(F32).reshape(2, 1, dk_t), n_batch)
    return of, ob, rg


def kernel(x, c, ctx, c_ctx, ada_w, ada_b, norm_g, ffn_w13, ffn_w2, final_g, na_w_kvq, na_rpb, na_w_o, ml_w_in, ml_gate_b, ml_conv_w, ml_norm_g, ml_w_o, da_w_kvq, da_lam, da_norm_g, da_w_o, gla_w_in, gla_w_gate_up, gla_b_gate, gla_norm_g, gla_w_o):
    nb = x.shape[0]
    assert x.shape[1:] == (N_LAT, D) and ctx.shape[1:] == (N_CTX, D) and nb < MOD_ROWS
    n_lat = nb * N_LAT
    n_tok = n_lat + nb * N_CTX

    s = jnp.zeros((MOD_ROWS, D), F32).at[:nb].set(c).at[nb].set(c_ctx)
    modtab = _mod_tables(s, ada_w, ada_b)
    h = jnp.concatenate([x.reshape(n_lat, D), ctx.reshape(nb * N_CTX, D)], axis=0)

    for i in range(DEPTH):
        kind, j = i % 4, i // 4
        last = i == DEPTH - 1
        mod3 = modtab[i].reshape(MOD_ROWS * N_MOD, 1, D)
        h = _ffn(h, mod3, norm_g[i, 0], ffn_w13[i, 0].astype(BF16), ffn_w2[i, 0].astype(BF16), final_g, 0, n_tok, nb)
        n_out = n_lat if last else n_tok
        if kind == 0:
            ins = _na_mixer(h, mod3, norm_g[i, 1], na_w_kvq[j], na_rpb[j], nb)
            h = _out_proj(h, mod3, na_w_o[j].astype(BF16), ins, n_out, nb, "plain")
        elif kind == 1:
            ins = _ml_mixer(h, mod3, norm_g[i, 1], ml_w_in[j], ml_gate_b[j], ml_conv_w[j], nb)
            h = _out_proj(h, mod3, ml_w_o[j].astype(BF16), ins, n_out, nb, "ml", ml_norm_g[j])
        elif kind == 2:
            ins = _da_mixer(h, mod3, norm_g[i, 1], da_w_kvq[j], da_lam[j], da_norm_g[j], i, nb)
            h = _out_proj(h, mod3, da_w_o[j].astype(BF16), ins, n_out, nb, "plain")
        else:
            ins = _gla_mixer(h, mod3, norm_g[i, 1], gla_w_in[j], gla_w_gate_up[j], gla_b_gate[j], nb)
            h = _out_proj(h, mod3, gla_w_o[j].astype(BF16), ins, n_out, nb, "gla", gla_norm_g[j])
        h = _ffn(h, mod3, norm_g[i, 2], ffn_w13[i, 1].astype(BF16), ffn_w2[i, 1].astype(BF16), final_g, 6, n_out, nb,
                 final=last)
    return h.reshape(nb, N_LAT, D)
```

```python
import functools

import numpy as np
import jax
import jax.numpy as jnp
from jax import lax
from jax.experimental import pallas as pl
from jax.experimental.pallas import tpu as pltpu

F32 = jnp.float32
BF16 = jnp.bfloat16

D = 1024
N_LAT = 2048
N_CTX = 256
DEPTH = 4
N_MOD = 9
D_FF = 2816
EPS = 1e-6
NEG = -1e30
GRID_W = 64

NA_HEADS = 16
NA_HD = 64
NA_WIN_R = 8
NA_WIN_C = 16

ML_HEADS = 4
ML_HD = 256

DA_HEADS = 8
DA_HD = 64
ROPE_BASE = 10000.0

GLA_HEADS = 4
GLA_DK = 128
GLA_DV = 256
GLA_RANK = 16
GLA_TAU = 16.0

ROW_BLK = 256
LAT_BLKS = N_LAT // ROW_BLK
MOD_ROWS = 16
LANES = 128
VMEM_LIMIT = 56 * 1024 * 1024


def _cparams(n_axes, vmem=VMEM_LIMIT):
    return pltpu.CompilerParams(dimension_semantics=("arbitrary",) * n_axes, vmem_limit_bytes=vmem)


def _sigmoid(x):
    return 1.0 / (1.0 + jnp.exp(-x))


def _silu(x):
    return x * _sigmoid(x)


def _log_sigmoid(x):
    return jnp.minimum(x, 0.0) - jnp.log1p(jnp.exp(-jnp.abs(x)))


def _rms(x, g):
    return x * lax.rsqrt(jnp.mean(x * x, axis=-1, keepdims=True) + EPS) * g


def _dot(a, b):
    return jnp.dot(a, b, preferred_element_type=F32)


def _dot_nt(a, b):
    return lax.dot_general(a, b, (((1,), (1,)), ((), ())), preferred_element_type=F32)


def _dot_tn(a, b):
    return lax.dot_general(a, b, (((0,), (0,)), ((), ())), preferred_element_type=F32)


def _mod_row(i, tm, n_batch):
    return jnp.minimum(i // (N_LAT // tm), n_batch)


def _mod_spec(k, tm, n_batch):
    return pl.BlockSpec((1, 1, D), lambda i: (_mod_row(i, tm, n_batch) * N_MOD + k, 0, 0))


def _mod_kernel(s_ref, w_ref, b_ref, o_ref):
    a = _silu(s_ref[...]).astype(BF16)
    o_ref[0] = _dot(a, w_ref[0].astype(BF16)) + b_ref[0]


def _mod_tables(s, ada_w, ada_b):
    return pl.pallas_call(
        _mod_kernel,
        grid=(DEPTH, N_MOD),
        in_specs=[
            pl.BlockSpec((MOD_ROWS, D), lambda i, k: (0, 0)),
            pl.BlockSpec((1, D, D), lambda i, k: (i, 0, k)),
            pl.BlockSpec((1, 1, D), lambda i, k: (i, 0, k)),
        ],
        out_specs=pl.BlockSpec((1, MOD_ROWS, D), lambda i, k: (i, 0, k)),
        out_shape=jax.ShapeDtypeStruct((DEPTH, MOD_ROWS, N_MOD * D), F32),
        compiler_params=_cparams(2),
        name="mod_tables",
    )(s, ada_w, ada_b.reshape(DEPTH, 1, N_MOD * D))


FFN_TM = 512
MXU_TILE = 256
FFN_SPLIT = (D_FF // MXU_TILE + 1) // 2 * MXU_TILE
FFN_CHUNKS = ((0, FFN_SPLIT), (FFN_SPLIT, D_FF))


def _head_rms(x, g, n_heads):
    hd = D // n_heads
    parts = []
    for i in range(n_heads):
        xi = x[:, i * hd:(i + 1) * hd]
        parts.append(xi * lax.rsqrt(jnp.mean(xi * xi, axis=-1, keepdims=True) + EPS))
    return jnp.concatenate(parts, axis=-1) * g


def _mixer_readout(mode, refs, is_lat):
    y = jnp.where(is_lat, refs[0][...], refs[1][...])
    if mode == "ml":
        return _sigmoid(refs[2][...]) * _head_rms(y, refs[3][...], ML_HEADS)
    if mode == "gla":
        return _head_rms(y, refs[3][...], GLA_HEADS) * _silu(refs[2][...])
    return y


def _ffn_kernel(g_ref, sh_ref, sc_ref, gt_ref, w13_ref, w2_ref, fg_ref, *refs, final, mixer, split_h, n_lat_tiles):
    o_ref = refs[-1]
    if split_h:
        x = jnp.where(pl.program_id(0) < n_lat_tiles, refs[0][...], refs[1][...])
        refs = refs[2:]
    else:
        x = refs[0][...]
        refs = refs[1:]
    if mixer is not None:
        mg_ref, wo_ref = refs[:2]
        y = _mixer_readout(mixer, refs[2:-1], pl.program_id(0) < n_lat_tiles)
        x = x + mg_ref[0] * _dot(y.astype(BF16), wo_ref[...])
    a = (_rms(x, g_ref[...]) * (1.0 + sc_ref[0]) + sh_ref[0]).astype(BF16)
    acc = jnp.zeros(x.shape, F32)
    for c0, c1 in FFN_CHUNKS:
        gate = _dot(a, w13_ref[:, c0:c1])
        up = _dot(a, w13_ref[:, D_FF + c0:D_FF + c1])
        act = (_silu(gate) * up).astype(BF16)
        acc = acc + _dot(act, w2_ref[c0:c1, :])
    y = x + 0.5 * gt_ref[0] * acc
    if final:
        y = _rms(y, fg_ref[...])
    o_ref[...] = y


def _ffn(h, mod3, g, w13, w2, final_g, k0, n_rows, n_batch, final=False, mixer=None, mixer_ins=(), w_o=None,
         mixer_g=None):
    tm = FFN_TM
    nl = n_batch * N_LAT // tm
    const = lambda i: (0, 0)
    row = pl.BlockSpec((tm, D), lambda i: (i, 0))
    lat_ctx = [pl.BlockSpec((tm, D), lambda i: (jnp.minimum(i, nl - 1), 0)),
               pl.BlockSpec((tm, D), lambda i: (jnp.maximum(i - nl, 0), 0))]
    split_h = isinstance(h, tuple)
    in_specs = [
        pl.BlockSpec((1, D), const),
        _mod_spec(k0, tm, n_batch),
        _mod_spec(k0 + 1, tm, n_batch),
        _mod_spec(k0 + 2, tm, n_batch),
        pl.BlockSpec((D, 2 * D_FF), const, pipeline_mode=pl.Buffered(1)),
        pl.BlockSpec((D_FF, D), const, pipeline_mode=pl.Buffered(1)),
        pl.BlockSpec((1, D), const),
    ] + (lat_ctx if split_h else [row])
    args = [g.reshape(1, D), mod3, mod3, mod3, w13, w2, final_g.reshape(1, D)] + (list(h) if split_h else [h])
    if mixer is not None:
        in_specs += [_mod_spec(5, tm, n_batch), pl.BlockSpec((D, D), const, pipeline_mode=pl.Buffered(1))] + lat_ctx
        args += [mod3, w_o, mixer_ins[0], mixer_ins[1]]
        if mixer != "plain":
            in_specs += [row, pl.BlockSpec((1, D), const)]
            args += [mixer_ins[2], mixer_g.reshape(1, D)]
    return pl.pallas_call(
        functools.partial(_ffn_kernel, final=final, mixer=mixer, split_h=split_h, n_lat_tiles=nl),
        grid=(n_rows // tm,),
        in_specs=in_specs,
        out_specs=row,
        out_shape=jax.ShapeDtypeStruct((n_rows, D), F32),
        compiler_params=_cparams(1),
        name="ffn" if mixer is None else "mixer_out_ffn",
    )(*args)


PROJ_TM = 512


def _proj_kernel(h_ref, g_ref, sh_ref, sc_ref, w_ref, *o_refs, segs):
    x = h_ref[...]
    a = (_rms(x, g_ref[...]) * (1.0 + sc_ref[0]) + sh_ref[0]).astype(BF16)
    for o_ref, (c0, c1, scale) in zip(o_refs, segs):
        y = _dot(a, w_ref[:, c0:c1])
        if scale != 1.0:
            y = y * scale
        o_ref[...] = y.astype(o_ref.dtype)


def _proj(h, mod3, g, w, segs, n_batch):
    tm = PROJ_TM
    n_rows = h.shape[0]
    const = lambda i: (0, 0)
    return pl.pallas_call(
        functools.partial(_proj_kernel, segs=tuple((c0, c1, sc) for c0, c1, sc, _ in segs)),
        grid=(n_rows // tm,),
        in_specs=[
            pl.BlockSpec((tm, D), lambda i: (i, 0)),
            pl.BlockSpec((1, D), const),
            _mod_spec(3, tm, n_batch),
            _mod_spec(4, tm, n_batch),
            pl.BlockSpec(w.shape, const, pipeline_mode=pl.Buffered(1)),
        ],
        out_specs=[pl.BlockSpec((tm, c1 - c0), lambda i: (i, 0)) for c0, c1, _, _ in segs],
        out_shape=[jax.ShapeDtypeStruct((n_rows, c1 - c0), dt) for c0, c1, _, dt in segs],
        compiler_params=_cparams(1),
        name="mixer_proj",
    )(h, g.reshape(1, D), mod3, mod3, w)


def _lane_lo(shape):
    return lax.broadcasted_iota(jnp.int32, shape, 1) < (LANES // 2)


def _half_masked(x, lo, head):
    return jnp.where(lo if head == 0 else jnp.logical_not(lo), x, jnp.zeros_like(x))


def _pair_attend(qp, segs, biases):
    tq = qp.shape[0]
    lo = _lane_lo(qp.shape)
    q2 = jnp.concatenate([_half_masked(qp, lo, 0), _half_masked(qp, lo, 1)], axis=0)
    scores = []
    for (k, _), b in zip(segs, biases):
        s = _dot_nt(q2, k)
        scores.append(s if b is None else s + b)
    m = scores[0].max(axis=-1, keepdims=True)
    for s in scores[1:]:
        m = jnp.maximum(m, s.max(axis=-1, keepdims=True))
    den = None
    acc = None
    for s, (_, v) in zip(scores, segs):
        p = jnp.exp(s - m)
        ps = p.sum(axis=-1, keepdims=True)
        pv = _dot(p.astype(BF16), v)
        den = ps if den is None else den + ps
        acc = pv if acc is None else acc + pv
    o2 = acc / den
    return jnp.where(lo, o2[:tq], o2[tq:])


NA_ROWS = N_LAT // GRID_W
NA_QROWS = 4
NA_KROWS = 12
NA_NKEY = NA_KROWS * GRID_W
NA_NOFF = 2 * NA_WIN_R - 1
NA_NENT = NA_NOFF + 1


def _na_bias_tables(rpb):
    c = np.arange(GRID_W)
    win0 = np.clip(c - NA_WIN_C // 2, 0, GRID_W - NA_WIN_C)
    kc = np.arange(GRID_W)
    in_win = (kc[None, :] >= win0[:, None]) & (kc[None, :] < win0[:, None] + NA_WIN_C)
    col_off = np.clip(kc[None, :] - c[:, None], 1 - NA_WIN_C, NA_WIN_C - 1) + NA_WIN_C - 1
    t = rpb.astype(F32)[:, :, col_off]
    t = jnp.where(in_win[None, None], t, NEG)
    t = jnp.pad(t, ((0, 0), (1, 1), (0, 0), (0, 0)))
    return jnp.concatenate([t[:, :NA_NENT], t[:, 1:]], axis=-1)


def _na_key_row0(j):
    return jnp.clip(NA_QROWS * j - NA_WIN_R // 2, 0, NA_ROWS - NA_KROWS)


def _na_kernel(q_ref, k_ref, v_ref, kc_ref, vc_ref, t_ref, o_ref):
    j = pl.program_id(1)
    u0 = _na_key_row0(j)
    start = pl.multiple_of(u0 * GRID_W, GRID_W)
    lo = _lane_lo((1, LANES))
    entry = []
    rowmask = []
    for i in range(NA_QROWS):
        r = NA_QROWS * j + i
        r0 = jnp.clip(r - NA_WIN_R // 2, 0, NA_ROWS - NA_WIN_R)
        entry.append([])
        rowmask.append([])
        for p in range(NA_KROWS // 2):
            a = u0 + 2 * p
            entry[i].append(jnp.clip(a - r + NA_WIN_R, 0, NA_NENT - 1))
            out_a = jnp.logical_or(a < r0, a >= r0 + NA_WIN_R)
            out_b = jnp.logical_or(a + 1 < r0, a + 1 >= r0 + NA_WIN_R)
            rowmask[i].append(jnp.where(lo, jnp.where(out_a, NEG, 0.0), jnp.where(out_b, NEG, 0.0)))
    for hp in range(NA_HEADS // 2):
        cols = slice(hp * LANES, (hp + 1) * LANES)
        bias_rows = []
        for head in range(2):
            for i in range(NA_QROWS):
                tiles = [t_ref[2 * hp + head, entry[i][p]] + rowmask[i][p] for p in range(NA_KROWS // 2)]
                bias_rows.append(jnp.concatenate(tiles, axis=1))
        bias = jnp.concatenate(bias_rows, axis=0)
        segs = [(k_ref[pl.ds(start, NA_NKEY), cols], v_ref[pl.ds(start, NA_NKEY), cols]),
                (kc_ref[:, cols], vc_ref[:, cols])]
        o_ref[:, cols] = _pair_attend(q_ref[:, cols], segs, [bias, None]).astype(o_ref.dtype)


def _na_attention(q, k, v, table, n_batch):
    ctx0 = n_batch * LAT_BLKS
    n_steps = NA_ROWS // NA_QROWS
    tq = NA_QROWS * GRID_W
    lat = pl.BlockSpec((N_LAT, D), lambda b, j: (b, 0))
    ctx = pl.BlockSpec((N_CTX, D), lambda b, j: (ctx0 + b, 0))
    return pl.pallas_call(
        _na_kernel,
        grid=(n_batch, n_steps),
        in_specs=[
            pl.BlockSpec((tq, D), lambda b, j: (b * n_steps + j, 0)),
            lat, lat, ctx, ctx,
            pl.BlockSpec(table.shape, lambda b, j: (0, 0, 0, 0), pipeline_mode=pl.Buffered(1)),
        ],
        out_specs=pl.BlockSpec((tq, D), lambda b, j: (b * n_steps + j, 0)),
        out_shape=jax.ShapeDtypeStruct((n_batch * N_LAT, D), BF16),
        compiler_params=_cparams(2),
        name="na_attention",
    )(q, k, v, k, v, table)


def _ctx_attn_kernel(q_ref, k_ref, v_ref, o_ref):
    for hp in range(NA_HEADS // 2):
        cols = slice(hp * LANES, (hp + 1) * LANES)
        segs = [(k_ref[:, cols], v_ref[:, cols])]
        o_ref[:, cols] = _pair_attend(q_ref[:, cols], segs, [None]).astype(o_ref.dtype)


def _ctx_attention(q, k, v, n_batch):
    ctx0 = n_batch * LAT_BLKS
    spec = pl.BlockSpec((N_CTX, D), lambda b: (ctx0 + b, 0))
    return pl.pallas_call(
        _ctx_attn_kernel,
        grid=(n_batch,),
        in_specs=[spec, spec, spec],
        out_specs=pl.BlockSpec((N_CTX, D), lambda b: (b, 0)),
        out_shape=jax.ShapeDtypeStruct((n_batch * N_CTX, D), BF16),
        compiler_params=_cparams(1),
        name="na_ctx_attention",
    )(q, k, v)


def _na_mixer(h, mod3, g, w_kvq, rpb, n_batch):
    segs = [(0, D, 1.0, BF16), (D, 2 * D, 1.0, BF16), (2 * D, 3 * D, NA_HD ** -0.5, BF16)]
    k, v, q = _proj(h, mod3, g, w_kvq.astype(BF16), segs, n_batch)
    return _na_attention(q, k, v, _na_bias_tables(rpb), n_batch), _ctx_attention(q, k, v, n_batch)


DA_TQ = 512
DA_W = 2 * DA_HD


def _rope_tables():
    t = np.arange(N_LAT)
    row = (t // GRID_W).astype(np.float64)
    col = (t % GRID_W).astype(np.float64)
    per_axis = DA_HD // 2
    freqs = ROPE_BASE ** (-np.arange(0, per_axis, 2, dtype=np.float64) / per_axis)
    ar = row[:, None] * freqs
    ac = col[:, None] * freqs
    ang = np.concatenate([ar, ar, ac, ac], axis=-1)
    cos = np.tile(np.cos(ang), (1, 2))
    sin = np.tile(np.sin(ang), (1, 2))
    quarter = (np.arange(DA_W) % DA_HD) // (DA_HD // 4)
    even = (quarter % 2 == 0)[None, :]
    sin_a = np.where(even, -sin, 0.0)
    sin_b = np.where(even, 0.0, sin)
    return tuple(jnp.asarray(a, F32) for a in (cos, sin_a, sin_b))


def _rope(x, cos, sin_a, sin_b):
    q16 = DA_HD // 4
    return x * cos + pltpu.roll(x, DA_W - q16, 1) * sin_a + pltpu.roll(x, q16, 1) * sin_b


def _da_lambda(lp, lam_init):
    a = jnp.sum(lp[0:1] * lp[1:2], axis=-1, keepdims=True)
    b = jnp.sum(lp[2:3] * lp[3:4], axis=-1, keepdims=True)
    return jnp.exp(a) - jnp.exp(b) + lam_init


LOG2E = 1.4426950408889634
DA_KCHUNK = 512


def _diff_attend(q, k, vx, lam, ng, lam_init):
    lo = _lane_lo(q.shape)
    qm = [_half_masked(q, lo, 0), _half_masked(q, lo, 1)]
    n_keys = k.shape[0]
    chunks = [(c0, min(c0 + DA_KCHUNK, n_keys)) for c0 in range(0, n_keys, DA_KCHUNK)]
    s = [[_dot_nt(qm[j], k[c0:c1, :]) for j in range(2)] for c0, c1 in chunks]
    m = [None, None]
    for sc in s:
        for j in range(2):
            mc = sc[j].max(axis=-1, keepdims=True)
            m[j] = mc if m[j] is None else jnp.maximum(m[j], mc)
    r = [None, None]
    for sc, (c0, c1) in zip(s, chunks):
        for j in range(2):
            pv = _dot(jnp.exp2(sc[j] - m[j]).astype(BF16), vx[c0:c1, :])
            r[j] = pv if r[j] is None else r[j] + pv
    o = r[0][:, :DA_W] / r[0][:, DA_W:DA_W + 1] - lam * (r[1][:, :DA_W] / r[1][:, DA_W:DA_W + 1])
    return o * lax.rsqrt(jnp.mean(o * o, axis=-1, keepdims=True) + EPS) * ng * (1.0 - lam_init)


def _da_kernel(lam_ref, q_ref, k_ref, v_ref, kc_ref, vc_ref, cq_ref, saq_ref, sbq_ref,
               ck_ref, sak_ref, sbk_ref, ng_ref, o_ref, kr_ref, vx_ref, *, lam_init):
    @pl.when(pl.program_id(2) == 0)
    def _():
        kr_ref[:N_LAT, :] = _rope(k_ref[...], ck_ref[...], sak_ref[...], sbk_ref[...]).astype(BF16)
        kr_ref[N_LAT:, :] = kc_ref[...].astype(BF16)
        vx_ref[:N_LAT, :DA_W] = v_ref[...]
        vx_ref[N_LAT:, :DA_W] = vc_ref[...]
        vx_ref[:, DA_W:] = jnp.ones((N_LAT + N_CTX, DA_W), BF16)

    lam = _da_lambda(lam_ref[...], lam_init)
    q = (_rope(q_ref[...], cq_ref[...], saq_ref[...], sbq_ref[...]) * LOG2E).astype(BF16)
    o_ref[...] = _diff_attend(q, kr_ref, vx_ref, lam, ng_ref[...], lam_init).astype(o_ref.dtype)


def _da_attention(q, k, v, lam_p, norm_g, lam_init, n_batch):
    cos, sin_a, sin_b = _rope_tables()
    nqb = N_LAT // DA_TQ
    ctx0 = n_batch * LAT_BLKS
    tab_q = pl.BlockSpec((DA_TQ, DA_W), lambda b, h, i: (i, 0))
    tab_k = pl.BlockSpec((N_LAT, DA_W), lambda b, h, i: (0, 0))
    lat = pl.BlockSpec((N_LAT, DA_W), lambda b, h, i: (b, h))
    ctx = pl.BlockSpec((N_CTX, DA_W), lambda b, h, i: (ctx0 + b, h))
    return pl.pallas_call(
        functools.partial(_da_kernel, lam_init=lam_init),
        grid=(n_batch, DA_HEADS, nqb),
        in_specs=[
            pl.BlockSpec((4, DA_HD), lambda b, h, i: (0, 0)),
            pl.BlockSpec((DA_TQ, DA_W), lambda b, h, i: (b * nqb + i, h)),
            lat, lat, ctx, ctx,
            tab_q, tab_q, tab_q, tab_k, tab_k, tab_k,
            pl.BlockSpec((1, DA_W), lambda b, h, i: (0, h)),
        ],
        out_specs=pl.BlockSpec((DA_TQ, DA_W), lambda b, h, i: (b * nqb + i, h)),
        out_shape=jax.ShapeDtypeStruct((n_batch * N_LAT, D), BF16),
        scratch_shapes=[pltpu.VMEM((N_LAT + N_CTX, DA_W), BF16), pltpu.VMEM((N_LAT + N_CTX, 2 * DA_W), BF16)],
        compiler_params=_cparams(3),
        name="diff_attention",
    )(lam_p, q, k, v, k, v, cos, sin_a, sin_b, cos, sin_a, sin_b, norm_g.reshape(1, D))


def _da_ctx_kernel(lam_ref, q_ref, k_ref, v_ref, ng_ref, o_ref, *, lam_init):
    lam = _da_lambda(lam_ref[...], lam_init)
    v = v_ref[...]
    vx = jnp.concatenate([v, jnp.ones_like(v)], axis=1)
    q = (q_ref[...] * LOG2E).astype(BF16)
    o_ref[...] = _diff_attend(q, k_ref[...].astype(BF16), vx, lam, ng_ref[...], lam_init).astype(o_ref.dtype)


def _da_ctx_attention(q, k, v, lam_p, norm_g, lam_init, n_batch):
    ctx0 = n_batch * LAT_BLKS
    spec = pl.BlockSpec((N_CTX, DA_W), lambda b, h: (ctx0 + b, h))
    return pl.pallas_call(
        functools.partial(_da_ctx_kernel, lam_init=lam_init),
        grid=(n_batch, DA_HEADS),
        in_specs=[pl.BlockSpec((4, DA_HD), lambda b, h: (0, 0)), spec, spec, spec,
                  pl.BlockSpec((1, DA_W), lambda b, h: (0, h))],
        out_specs=pl.BlockSpec((N_CTX, DA_W), lambda b, h: (b, h)),
        out_shape=jax.ShapeDtypeStruct((n_batch * N_CTX, D), BF16),
        compiler_params=_cparams(2),
        name="diff_ctx_attention",
    )(lam_p, q, k, v, norm_g.reshape(1, D))


def _da_mixer(h, mod3, g, w_kvq, lam_p, norm_g, layer_idx, n_batch):
    lam_init = 0.8 - 0.6 * float(np.exp(-0.3 * layer_idx))
    segs = [(0, D, 1.0, F32), (D, 2 * D, 1.0, BF16), (2 * D, 3 * D, DA_HD ** -0.5, F32)]
    k, v, q = _proj(h, mod3, g, w_kvq.astype(BF16), segs, n_batch)
    lam_p = lam_p.astype(F32)
    return (_da_attention(q, k, v, lam_p, norm_g, lam_init, n_batch),
            _da_ctx_attention(q, k, v, lam_p, norm_g, lam_init, n_batch))


SCAN_STEPS = 1 + LAT_BLKS


def _scan_row_blk(direction, n_batch):
    ctx0 = n_batch * LAT_BLKS

    def blk(b, s):
        lat = b * LAT_BLKS + (s - 1 if direction == 0 else LAT_BLKS - s)
        return jnp.where(s == 0, ctx0 + b, lat)
    return blk


def _scan_spec(width, direction, n_batch):
    blk = _scan_row_blk(direction, n_batch)
    return pl.BlockSpec((ROW_BLK, width), lambda b, h, s: (blk(b, s), h))


def _scan_spec_all(width, direction, n_batch):
    blk = _scan_row_blk(direction, n_batch)
    return pl.BlockSpec((ROW_BLK, width), lambda b, h, s: (blk(b, s), 0))


def _scan_out_specs(width):
    return [pl.BlockSpec((N_LAT, width), lambda b, h, s: (b, h)), pl.BlockSpec((N_CTX, width), lambda b, h, s: (b, h))]


def _scan_write_sum(step, out_fwd, out_bwd, lat_ref, ctx_ref):
    @pl.when(step == 0)
    def _():
        ctx_ref[...] = out_fwd + out_bwd

    rows_f = pl.ds(pl.multiple_of((step - 1) * ROW_BLK, ROW_BLK), ROW_BLK)
    rows_b = pl.ds(pl.multiple_of((LAT_BLKS - step) * ROW_BLK, ROW_BLK), ROW_BLK)

    @pl.when(jnp.logical_and(step >= 1, step <= LAT_BLKS // 2))
    def _():
        lat_ref[rows_f, :] = out_fwd
        lat_ref[rows_b, :] = out_bwd

    @pl.when(step > LAT_BLKS // 2)
    def _():
        lat_ref[rows_f, :] += out_fwd
        lat_ref[rows_b, :] += out_bwd


def _ml_conv_kernel(xk_ref, xkp_ref, xkn_ref, xq_ref, xqp_ref, xqn_ref, w_ref, ok_ref, oq_ref, *, n_lat_blks):
    i = pl.program_id(0)
    is_ctx = i >= n_lat_blks
    first = jnp.logical_or(is_ctx, i % LAT_BLKS == 0)
    last = jnp.logical_or(is_ctx, i % LAT_BLKS == LAT_BLKS - 1)
    rid = lax.broadcasted_iota(jnp.int32, (ROW_BLK, D), 0)
    groups = ((xk_ref, xkp_ref, xkn_ref, 0, ok_ref, ML_HD ** -0.5), (xq_ref, xqp_ref, xqn_ref, D, oq_ref, 1.0))
    for x_ref, xp_ref, xn_ref, c0, o_ref, scale in groups:
        x = x_ref[...]
        prev_row = jnp.where(first, 0.0, xp_ref[7:8, :])
        next_row = jnp.where(last, 0.0, xn_ref[0:1, :])
        x_prev = jnp.where(rid == 0, prev_row, pltpu.roll(x, 1, 0))
        x_next = jnp.where(rid == ROW_BLK - 1, next_row, pltpu.roll(x, ROW_BLK - 1, 0))
        y = w_ref[0:1, c0:c0 + D] * x_prev + w_ref[1:2, c0:c0 + D] * x + w_ref[2:3, c0:c0 + D] * x_next
        o_ref[...] = (_silu(y) * scale).astype(o_ref.dtype)


def _ml_conv(pk, pq, conv_w, n_batch):
    n_rows = pk.shape[0]
    n_blk = n_rows // ROW_BLK
    sub = ROW_BLK // 8
    main = pl.BlockSpec((ROW_BLK, D), lambda i: (i, 0))
    prev = pl.BlockSpec((8, D), lambda i: (jnp.maximum(i * sub - 1, 0), 0))
    nxt = pl.BlockSpec((8, D), lambda i: (jnp.minimum((i + 1) * sub, n_blk * sub - 1), 0))
    return pl.pallas_call(
        functools.partial(_ml_conv_kernel, n_lat_blks=n_batch * LAT_BLKS),
        grid=(n_blk,),
        in_specs=[main, prev, nxt, main, prev, nxt, pl.BlockSpec((3, 2 * D), lambda i: (0, 0))],
        out_specs=[main, main],
        out_shape=[jax.ShapeDtypeStruct((n_rows, D), BF16)] * 2,
        compiler_params=_cparams(1),
        name="mlstm_conv",
    )(pk, pk, pk, pq, pq, pq, conv_w)


def _pick_lane(x, idx):
    lane = lax.broadcasted_iota(jnp.int32, x.shape, 1)
    return jnp.sum(jnp.where(lane == idx, x, 0.0), axis=1, keepdims=True)


def _pick_sublane(x, idx):
    sub = lax.broadcasted_iota(jnp.int32, x.shape, 0)
    return jnp.sum(jnp.where(sub == idx, x, 0.0), axis=0, keepdims=True)


def _ml_scan_kernel(gb_ref, g0_ref, q0_ref, k0_ref, v0_ref, g1_ref, q1_ref, k1_ref, v1_ref,
                    lat_ref, ctx_ref, c_ref, n_ref, m_ref):
    head = pl.program_id(1)

    @pl.when(pl.program_id(2) == 0)
    def _():
        c_ref[...] = jnp.zeros_like(c_ref)
        n_ref[...] = jnp.zeros_like(n_ref)
        m_ref[...] = jnp.zeros_like(m_ref)

    r = lax.broadcasted_iota(jnp.int32, (ROW_BLK, ROW_BLK), 0)
    c = lax.broadcasted_iota(jnp.int32, (ROW_BLK, ROW_BLK), 1)
    dirs = ((g0_ref, q0_ref, k0_ref, v0_ref), (g1_ref, q1_ref, k1_ref, v1_ref))
    outs = []
    for d, (g_ref, q_ref, k_ref, v_ref) in enumerate(dirs):
        valid = (r >= c) if d == 0 else (r <= c)
        other = (r <= c) if d == 0 else (r >= c)
        g = g_ref[...] + gb_ref[...]
        gt = g.T
        ii = d * 2 * ML_HEADS + head
        fi = ii + ML_HEADS
        i_col = _pick_lane(g, ii)
        f_col = _log_sigmoid(_pick_lane(g, fi))
        i_row = _pick_sublane(gt, ii)
        f_row = _log_sigmoid(_pick_sublane(gt, fi))
        b_col = jnp.sum(jnp.where(valid, f_row, 0.0), axis=1, keepdims=True)
        b_row = jnp.sum(jnp.where(other, f_col, 0.0), axis=0, keepdims=True)
        total = jnp.sum(f_col, axis=0, keepdims=True)
        m_prev = m_ref[d][0:1, 0:1]
        n_prev = n_ref[d][0:1, :]
        c_prev = c_ref[d]
        qc = q_ref[...]
        kc = k_ref[...]
        vc = v_ref[...]

        dmat = jnp.where(valid, b_col - b_row + i_row, NEG)
        inter = b_col + m_prev
        m_t = jnp.maximum(inter, dmat.max(axis=1, keepdims=True))
        w_inter = jnp.exp(inter - m_t)
        s = _dot_nt(qc, kc) * jnp.exp(dmat - m_t)
        num = w_inter * _dot(qc, c_prev.astype(BF16)) + _dot(s.astype(BF16), vc)
        den = w_inter * jnp.sum(qc.astype(F32) * n_prev, axis=1, keepdims=True) + s.sum(axis=1, keepdims=True)
        outs.append(num / jnp.maximum(jnp.abs(den), jnp.exp(-m_t)))

        g_col = total - b_col + i_col
        m_new = jnp.maximum(total + m_prev, g_col.max(axis=0, keepdims=True))
        w_old = jnp.exp(total + m_prev - m_new)
        kw = kc.astype(F32) * jnp.exp(g_col - m_new)
        c_ref[d] = w_old * c_prev + _dot_tn(kw.astype(BF16), vc)
        n_ref[d] = jnp.broadcast_to(w_old * n_prev + kw.sum(axis=0, keepdims=True), n_ref.shape[1:])
        m_ref[d] = jnp.broadcast_to(m_new, m_ref.shape[1:])
    _scan_write_sum(pl.program_id(2), outs[0], outs[1], lat_ref, ctx_ref)


def _ml_scan(q, k, v, gates, gate_b, n_batch):
    in_specs = [pl.BlockSpec((1, LANES), lambda b, h, s: (0, 0))]
    for d in range(2):
        in_specs += [_scan_spec_all(LANES, d, n_batch)] + [_scan_spec(ML_HD, d, n_batch)] * 3
    return pl.pallas_call(
        _ml_scan_kernel,
        grid=(n_batch, ML_HEADS, SCAN_STEPS),
        in_specs=in_specs,
        out_specs=_scan_out_specs(ML_HD),
        out_shape=[jax.ShapeDtypeStruct((n_batch * N_LAT, D), F32), jax.ShapeDtypeStruct((n_batch * N_CTX, D), F32)],
        scratch_shapes=[pltpu.VMEM((2, ML_HD, ML_HD), F32), pltpu.VMEM((2, 8, ML_HD), F32),
                        pltpu.VMEM((2, 8, LANES), F32)],
        compiler_params=_cparams(3),
        name="mlstm_scan",
    )(gate_b, gates, q, k, v, gates, q, k, v)


def _ml_mixer(h, mod3, g, w_in, gate_b, conv_w, n_batch):
    n_g = 4 * ML_HEADS
    wk, wv, wg, wq, wo = (w_in[:, :D], w_in[:, D:2 * D], w_in[:, 2 * D:2 * D + n_g],
                          w_in[:, 2 * D + n_g:3 * D + n_g], w_in[:, 3 * D + n_g:])
    w = jnp.concatenate([wk, wv, wq, wo, jnp.pad(wg, ((0, 0), (0, LANES - n_g)))], axis=1).astype(BF16)
    segs = [(0, D, 1.0, F32), (D, 2 * D, 1.0, BF16), (2 * D, 3 * D, 1.0, F32), (3 * D, 4 * D, 1.0, F32),
            (4 * D, 4 * D + LANES, 1.0, F32)]
    pk, v, pq, og, gates = _proj(h, mod3, g, w, segs, n_batch)
    k, q = _ml_conv(pk, pq, conv_w.astype(F32), n_batch)
    gb = jnp.pad(gate_b.astype(F32), (0, LANES - n_g)).reshape(1, LANES)
    h_lat, h_ctx = _ml_scan(q, k, v, gates, gb, n_batch)
    return h_lat, h_ctx, og


GLA_LEVELS = (256, 128, 64, 32, 16, 8, 4, 2)
GLA_BCAST_LEVELS = 6
GLA_DIAG = len(GLA_LEVELS)


def _gla_tables():
    u = np.arange(ROW_BLK)
    tt, ss = np.meshgrid(u, u, indexing="ij")
    hb = np.floor(np.log2(np.maximum(tt ^ ss, 1))).astype(np.int64)
    level = len(GLA_LEVELS) - 1 - hb
    tri = np.stack([ss <= tt, ss >= tt]).astype(np.float32)
    lid = np.stack([np.where(tt == ss, GLA_DIAG, np.where(tt > ss, level, -1)),
                    np.where(tt == ss, GLA_DIAG, np.where(tt < ss, level, -1))]).astype(np.int32)
    return jnp.asarray(tri, BF16), jnp.asarray(lid)


def _split3(x):
    hi = x.astype(BF16)
    r1 = x - hi.astype(F32)
    mid = r1.astype(BF16)
    lo = (r1 - mid.astype(F32)).astype(BF16)
    return hi, mid, lo


def _gla_level_exponents(level, d, bc, bc_ref, la, la_prev, la_next, row):
    n = GLA_LEVELS[level]
    half = n // 2
    if level < GLA_BCAST_LEVELS:
        parts = []
        for j in range(ROW_BLK // n):
            mid = j * n + (half - 1 if d == 0 else half)
            parts.append(jnp.broadcast_to(bc_ref[mid:mid + 1, :], (n, GLA_DK)))
        x = bc - (parts[0] if len(parts) == 1 else jnp.concatenate(parts, axis=0))
        return jnp.minimum(x, 0.0), jnp.minimum(-x, 0.0)
    o = row & (n - 1)
    if n == 4:
        if d == 0:
            return (jnp.where(o == 2, la, jnp.where(o == 3, la + la_prev, 0.0)), jnp.where(o == 0, la_next, 0.0))
        return (jnp.where(o == 0, la + la_next, jnp.where(o == 1, la, 0.0)), jnp.where(o == 3, la_prev, 0.0))
    return jnp.where(o == (1 if d == 0 else 0), la, 0.0), None


def _gla_kernel(tri_ref, lid_ref, wup_ref, bg_ref, q0_ref, k0_ref, v0_ref, l0_ref, q1_ref, k1_ref, v1_ref, l1_ref,
                lat_ref, ctx_ref, st_ref, bc_ref, att_ref):
    @pl.when(pl.program_id(2) == 0)
    def _():
        st_ref[...] = jnp.zeros_like(st_ref)

    row = lax.broadcasted_iota(jnp.int32, (ROW_BLK, GLA_DK), 0)
    dirs = ((q0_ref, k0_ref, v0_ref, l0_ref), (q1_ref, k1_ref, v1_ref, l1_ref))
    outs = []
    for d, (q_ref, k_ref, v_ref, low_ref) in enumerate(dirs):
        pre = _dot(low_ref[...].astype(BF16), wup_ref[d]) + bg_ref[d]
        la = _log_sigmoid(pre) * (1.0 / GLA_TAU)
        hi, mid, lo = _split3(la)
        tri = tri_ref[d]
        bc = _dot(tri, hi) + _dot(tri, mid) + _dot(tri, lo)
        bc_ref[...] = bc
        la_prev = pltpu.roll(la, 1, 0)
        la_next = pltpu.roll(la, ROW_BLK - 1, 0)
        q = q_ref[...]
        k = k_ref[...]
        v = v_ref[...]
        lid = lid_ref[d]
        kb = k.astype(BF16)
        att_ref[...] = jnp.where(lid == GLA_DIAG, _dot_nt(q.astype(BF16), kb), 0.0)
        for level in range(len(GLA_LEVELS)):
            eq, ek = _gla_level_exponents(level, d, bc, bc_ref, la, la_prev, la_next, row)
            ql = (q * jnp.exp(eq)).astype(BF16)
            kl = kb if ek is None else (k * jnp.exp(ek)).astype(BF16)
            att_ref[...] = jnp.where(lid == level, _dot_nt(ql, kl), att_ref[...])
        st = st_ref[d]
        qd = (q * jnp.exp(bc)).astype(BF16)
        outs.append(_dot(att_ref[...].astype(BF16), v) + _dot_nt(qd, st.astype(BF16)))
        b_end = bc[ROW_BLK - 1:ROW_BLK] if d == 0 else bc[0:1]
        kd = (k * jnp.exp(b_end - bc)).astype(BF16)
        st_ref[d] = st * jnp.exp(b_end) + _dot_tn(v, kd)
    _scan_write_sum(pl.program_id(2), outs[0], outs[1], lat_ref, ctx_ref)


def _gla_scan(q, k, v, low, wup, bg, n_batch):
    tri, lid = _gla_tables()
    const3 = lambda b, h, s: (0, 0, 0)
    in_specs = [
        pl.BlockSpec(tri.shape, const3),
        pl.BlockSpec(lid.shape, const3),
        pl.BlockSpec((2, LANES, GLA_DK), lambda b, h, s: (0, 0, h)),
        pl.BlockSpec((2, 1, GLA_DK), lambda b, h, s: (0, 0, h)),
    ]
    for d in range(2):
        in_specs += [_scan_spec(GLA_DK, d, n_batch), _scan_spec(GLA_DK, d, n_batch),
                     _scan_spec(GLA_DV, d, n_batch), _scan_spec_all(LANES, d, n_batch)]
    return pl.pallas_call(
        _gla_kernel,
        grid=(n_batch, GLA_HEADS, SCAN_STEPS),
        in_specs=in_specs,
        out_specs=_scan_out_specs(GLA_DV),
        out_shape=[jax.ShapeDtypeStruct((n_batch * N_LAT, D), F32), jax.ShapeDtypeStruct((n_batch * N_CTX, D), F32)],
        scratch_shapes=[pltpu.VMEM((2, GLA_DV, GLA_DK), F32), pltpu.VMEM((ROW_BLK, GLA_DK), F32),
                        pltpu.VMEM((ROW_BLK, ROW_BLK), F32)],
        compiler_params=_cparams(3),
        name="gla_scan",
    )(tri, lid, wup, bg, q, k, v, low, q, k, v, low)


def _gla_mixer(h, mod3, g, w_in, w_gate_up, b_gate, n_batch):
    dk_t = GLA_HEADS * GLA_DK
    n_low = 2 * GLA_RANK
    wk, wv, wl, wq, wr = (w_in[:, :dk_t], w_in[:, dk_t:dk_t + D], w_in[:, dk_t + D:dk_t + D + n_low],
                          w_in[:, dk_t + D + n_low:2 * dk_t + D + n_low], w_in[:, 2 * dk_t + D + n_low:])
    w = jnp.concatenate([wk, wq, wv, wr, jnp.pad(wl, ((0, 0), (0, LANES - n_low)))], axis=1).astype(BF16)
    c = [0, dk_t, 2 * dk_t, 2 * dk_t + D, 2 * dk_t + 2 * D, 2 * dk_t + 2 * D + LANES]
    segs = [(c[0], c[1], 1.0, F32), (c[1], c[2], GLA_DK ** -0.5, F32), (c[2], c[3], 1.0, BF16),
            (c[3], c[4], 1.0, F32), (c[4], c[5], 1.0, F32)]
    k, q, v, rg, low = _proj(h, mod3, g, w, segs, n_batch)
    wup = jnp.zeros((2, LANES, dk_t), F32)
    for r in range(2):
        wup = wup.at[r, r * GLA_RANK:(r + 1) * GLA_RANK].set(w_gate_up[r].astype(F32))
    o_lat, o_ctx = _gla_scan(q, k, v, low, wup.astype(BF16), b_gate.astype(F32).reshape(2, 1, dk_t), n_batch)
    return o_lat, o_ctx, rg


def kernel(x, c, ctx, c_ctx, ada_w, ada_b, norm_g, ffn_w13, ffn_w2, final_g, na_w_kvq, na_rpb, na_w_o, ml_w_in, ml_gate_b, ml_conv_w, ml_norm_g, ml_w_o, da_w_kvq, da_lam, da_norm_g, da_w_o, gla_w_in, gla_w_gate_up, gla_b_gate, gla_norm_g, gla_w_o):
    nb = x.shape[0]
    assert x.shape[1:] == (N_LAT, D) and ctx.shape[1:] == (N_CTX, D) and nb < MOD_ROWS
    n_lat = nb * N_LAT
    n_tok = n_lat + nb * N_CTX

    s = jnp.zeros((MOD_ROWS, D), F32).at[:nb].set(c).at[nb].set(c_ctx)
    modtab = _mod_tables(s, ada_w, ada_b)
    h = (x.reshape(n_lat, D), ctx.reshape(nb * N_CTX, D))

    for i in range(DEPTH):
        kind, j = i % 4, i // 4
        last = i == DEPTH - 1
        mod3 = modtab[i].reshape(MOD_ROWS * N_MOD, 1, D)
        h = _ffn(h, mod3, norm_g[i, 0], ffn_w13[i, 0].astype(BF16), ffn_w2[i, 0].astype(BF16), final_g, 0, n_tok, nb)
        n_out = n_lat if last else n_tok
        if kind == 0:
            ins = _na_mixer(h, mod3, norm_g[i, 1], na_w_kvq[j], na_rpb[j], nb)
            mixer, w_o, mixer_g = "plain", na_w_o[j], None
        elif kind == 1:
            ins = _ml_mixer(h, mod3, norm_g[i, 1], ml_w_in[j], ml_gate_b[j], ml_conv_w[j], nb)
            mixer, w_o, mixer_g = "ml", ml_w_o[j], ml_norm_g[j]
        elif kind == 2:
            ins = _da_mixer(h, mod3, norm_g[i, 1], da_w_kvq[j], da_lam[j], da_norm_g[j], i, nb)
            mixer, w_o, mixer_g = "plain", da_w_o[j], None
        else:
            ins = _gla_mixer(h, mod3, norm_g[i, 1], gla_w_in[j], gla_w_gate_up[j], gla_b_gate[j], nb)
            mixer, w_o, mixer_g = "gla", gla_w_o[j], gla_norm_g[j]
        h = _ffn(h, mod3, norm_g[i, 2], ffn_w13[i, 1].astype(BF16), ffn_w2[i, 1].astype(BF16), final_g, 6, n_out, nb,
                 final=last, mixer=mixer, mixer_ins=ins, w_o=w_o.astype(BF16), mixer_g=mixer_g)
    return h.reshape(nb, N_LAT, D)
```

```python
import functools

import numpy as np
import jax
import jax.numpy as jnp
from jax import lax
from jax.experimental import pallas as pl
from jax.experimental.pallas import tpu as pltpu

F32 = jnp.float32
BF16 = jnp.bfloat16

D = 1024
N_LAT = 2048
N_CTX = 256
DEPTH = 4
N_MOD = 9
D_FF = 2816
EPS = 1e-6
NEG = -1e30
LOG2E = 1.4426950408889634
GRID_W = 64

NA_HEADS = 16
NA_HD = 64
NA_WIN_R = 8
NA_WIN_C = 16

ML_HEADS = 4
ML_HD = 256

DA_HEADS = 8
DA_HD = 64
ROPE_BASE = 10000.0

GLA_HEADS = 4
GLA_DK = 128
GLA_DV = 256
GLA_RANK = 16
GLA_TAU = 16.0

ROW_BLK = 256
LAT_BLKS = N_LAT // ROW_BLK
MOD_ROWS = 16
LANES = 128
VMEM_LIMIT = 56 * 1024 * 1024


def _cparams(n_axes, vmem=VMEM_LIMIT):
    return pltpu.CompilerParams(dimension_semantics=("arbitrary",) * n_axes, vmem_limit_bytes=vmem)


def _sigmoid(x):
    return 1.0 / (1.0 + jnp.exp(-x))


def _silu(x):
    return x * _sigmoid(x)


def _log_sigmoid(x):
    return jnp.minimum(x, 0.0) - jnp.log1p(jnp.exp(-jnp.abs(x)))


def _rms(x, g):
    return x * lax.rsqrt(jnp.mean(x * x, axis=-1, keepdims=True) + EPS) * g


def _dot(a, b):
    return jnp.dot(a, b, preferred_element_type=F32)


def _dot_nt(a, b):
    return lax.dot_general(a, b, (((1,), (1,)), ((), ())), preferred_element_type=F32)


def _dot_tn(a, b):
    return lax.dot_general(a, b, (((0,), (0,)), ((), ())), preferred_element_type=F32)


def _mod_row(i, tm, n_batch):
    return jnp.minimum(i // (N_LAT // tm), n_batch)


def _mod_spec(k, tm, n_batch):
    return pl.BlockSpec((1, 1, D), lambda i: (_mod_row(i, tm, n_batch) * N_MOD + k, 0, 0))


def _mod_kernel(s_ref, w_ref, b_ref, o_ref):
    a = _silu(s_ref[...]).astype(BF16)
    o_ref[0] = _dot(a, w_ref[0].astype(BF16)) + b_ref[0]


def _mod_tables(s, ada_w, ada_b):
    return pl.pallas_call(
        _mod_kernel,
        grid=(DEPTH, N_MOD),
        in_specs=[
            pl.BlockSpec((MOD_ROWS, D), lambda i, k: (0, 0)),
            pl.BlockSpec((1, D, D), lambda i, k: (i, 0, k)),
            pl.BlockSpec((1, 1, D), lambda i, k: (i, 0, k)),
        ],
        out_specs=pl.BlockSpec((1, MOD_ROWS, D), lambda i, k: (i, 0, k)),
        out_shape=jax.ShapeDtypeStruct((DEPTH, MOD_ROWS, N_MOD * D), F32),
        compiler_params=_cparams(2),
        name="mod_tables",
    )(s, ada_w, ada_b.reshape(DEPTH, 1, N_MOD * D))


FFN_TM = 512
MXU_TILE = 256
FFN_SPLIT = (D_FF // MXU_TILE + 1) // 2 * MXU_TILE
FFN_CHUNKS = ((0, FFN_SPLIT), (FFN_SPLIT, D_FF))


def _head_rms(x, g, n_heads):
    hd = D // n_heads
    parts = []
    for i in range(n_heads):
        xi = x[:, i * hd:(i + 1) * hd]
        parts.append(xi * lax.rsqrt(jnp.mean(xi * xi, axis=-1, keepdims=True) + EPS))
    return jnp.concatenate(parts, axis=-1) * g


def _mixer_readout(mode, refs, is_lat):
    y = jnp.where(is_lat, refs[0][...], refs[1][...])
    if mode == "ml":
        return _sigmoid(refs[2][...]) * _head_rms(y, refs[3][...], ML_HEADS)
    if mode == "gla":
        return _head_rms(y, refs[3][...], GLA_HEADS) * _silu(refs[2][...])
    return y


def _ffn_kernel(g_ref, sh_ref, sc_ref, gt_ref, w13_ref, w2_ref, fg_ref, *refs, final, mixer, split_h, n_lat_tiles):
    o_ref = refs[-1]
    if split_h:
        x = jnp.where(pl.program_id(0) < n_lat_tiles, refs[0][...], refs[1][...])
        refs = refs[2:]
    else:
        x = refs[0][...]
        refs = refs[1:]
    if mixer is not None:
        mg_ref, wo_ref = refs[:2]
        y = _mixer_readout(mixer, refs[2:-1], pl.program_id(0) < n_lat_tiles)
        x = x + mg_ref[0] * _dot(y.astype(BF16), wo_ref[...])
    a = (_rms(x, g_ref[...]) * (1.0 + sc_ref[0]) + sh_ref[0]).astype(BF16)
    acc = jnp.zeros(x.shape, F32)
    for c0, c1 in FFN_CHUNKS:
        gate = _dot(a, w13_ref[:, c0:c1])
        up = _dot(a, w13_ref[:, D_FF + c0:D_FF + c1])
        act = (_silu(gate) * up).astype(BF16)
        acc = acc + _dot(act, w2_ref[c0:c1, :])
    y = x + 0.5 * gt_ref[0] * acc
    if final:
        y = _rms(y, fg_ref[...])
    o_ref[...] = y


def _ffn(h, mod3, g, w13, w2, final_g, k0, n_rows, n_batch, final=False, mixer=None, mixer_ins=(), w_o=None,
         mixer_g=None):
    tm = FFN_TM
    nl = n_batch * N_LAT // tm
    const = lambda i: (0, 0)
    row = pl.BlockSpec((tm, D), lambda i: (i, 0))
    lat_ctx = [pl.BlockSpec((tm, D), lambda i: (jnp.minimum(i, nl - 1), 0)),
               pl.BlockSpec((tm, D), lambda i: (jnp.maximum(i - nl, 0), 0))]
    split_h = isinstance(h, tuple)
    in_specs = [
        pl.BlockSpec((1, D), const),
        _mod_spec(k0, tm, n_batch),
        _mod_spec(k0 + 1, tm, n_batch),
        _mod_spec(k0 + 2, tm, n_batch),
        pl.BlockSpec((D, 2 * D_FF), const, pipeline_mode=pl.Buffered(1)),
        pl.BlockSpec((D_FF, D), const, pipeline_mode=pl.Buffered(1)),
        pl.BlockSpec((1, D), const),
    ] + (lat_ctx if split_h else [row])
    args = [g.reshape(1, D), mod3, mod3, mod3, w13, w2, final_g.reshape(1, D)] + (list(h) if split_h else [h])
    if mixer is not None:
        in_specs += [_mod_spec(5, tm, n_batch), pl.BlockSpec((D, D), const, pipeline_mode=pl.Buffered(1))] + lat_ctx
        args += [mod3, w_o, mixer_ins[0], mixer_ins[1]]
        if mixer != "plain":
            in_specs += [row, pl.BlockSpec((1, D), const)]
            args += [mixer_ins[2], mixer_g.reshape(1, D)]
    return pl.pallas_call(
        functools.partial(_ffn_kernel, final=final, mixer=mixer, split_h=split_h, n_lat_tiles=nl),
        grid=(n_rows // tm,),
        in_specs=in_specs,
        out_specs=row,
        out_shape=jax.ShapeDtypeStruct((n_rows, D), F32),
        compiler_params=_cparams(1),
        name="ffn" if mixer is None else "mixer_out_ffn",
    )(*args)


PROJ_TM = 512


def _proj_kernel(h_ref, g_ref, sh_ref, sc_ref, w_ref, *o_refs, segs):
    x = h_ref[...]
    a = (_rms(x, g_ref[...]) * (1.0 + sc_ref[0]) + sh_ref[0]).astype(BF16)
    for o_ref, (c0, c1, scale) in zip(o_refs, segs):
        y = _dot(a, w_ref[:, c0:c1])
        if scale != 1.0:
            y = y * scale
        o_ref[...] = y.astype(o_ref.dtype)


def _proj(h, mod3, g, w, segs, n_batch):
    tm = PROJ_TM
    n_rows = h.shape[0]
    const = lambda i: (0, 0)
    return pl.pallas_call(
        functools.partial(_proj_kernel, segs=tuple((c0, c1, sc) for c0, c1, sc, _ in segs)),
        grid=(n_rows // tm,),
        in_specs=[
            pl.BlockSpec((tm, D), lambda i: (i, 0)),
            pl.BlockSpec((1, D), const),
            _mod_spec(3, tm, n_batch),
            _mod_spec(4, tm, n_batch),
            pl.BlockSpec(w.shape, const, pipeline_mode=pl.Buffered(1)),
        ],
        out_specs=[pl.BlockSpec((tm, c1 - c0), lambda i: (i, 0)) for c0, c1, _, _ in segs],
        out_shape=[jax.ShapeDtypeStruct((n_rows, c1 - c0), dt) for c0, c1, _, dt in segs],
        compiler_params=_cparams(1),
        name="mixer_proj",
    )(h, g.reshape(1, D), mod3, mod3, w)


def _lane_lo(shape):
    return lax.broadcasted_iota(jnp.int32, shape, 1) < (LANES // 2)


def _half_masked(x, lo, head):
    return jnp.where(lo if head == 0 else jnp.logical_not(lo), x, jnp.zeros_like(x))


def _pair_attend(qp, segs, biases):
    tq = qp.shape[0]
    lo = _lane_lo(qp.shape)
    q2 = jnp.concatenate([_half_masked(qp, lo, 0), _half_masked(qp, lo, 1)], axis=0)
    scores = []
    for (k, _), b in zip(segs, biases):
        s = _dot_nt(q2, k)
        scores.append(s if b is None else s + b)
    m = scores[0].max(axis=-1, keepdims=True)
    for s in scores[1:]:
        m = jnp.maximum(m, s.max(axis=-1, keepdims=True))
    acc = None
    for s, (_, v) in zip(scores, segs):
        vx = jnp.concatenate([v, jnp.ones_like(v)], axis=1)
        pv = _dot(jnp.exp2(s - m).astype(BF16), vx)
        acc = pv if acc is None else acc + pv
    o2 = acc[:, :LANES] / acc[:, LANES:]
    return jnp.where(lo, o2[:tq], o2[tq:])


NA_ROWS = N_LAT // GRID_W
NA_QROWS = 4
NA_KROWS = 12
NA_NKEY = NA_KROWS * GRID_W
NA_NOFF = 2 * NA_WIN_R - 1
NA_NENT = NA_NOFF + 1


NA_NCOFF = 2 * NA_WIN_C - 1


def _na_table_kernel(rpb_ref, o_ref):
    h = pl.program_id(0)
    c = lax.broadcasted_iota(jnp.int32, (GRID_W, LANES), 0)
    lane = lax.broadcasted_iota(jnp.int32, (GRID_W, LANES), 1)
    kc = lane & (GRID_W - 1)
    second = lane >= GRID_W
    col_off = jnp.clip(kc - c, 1 - NA_WIN_C, NA_WIN_C - 1) + NA_WIN_C - 1
    win0 = jnp.clip(c - NA_WIN_C // 2, 0, GRID_W - NA_WIN_C)
    in_win = jnp.logical_and(kc >= win0, kc < win0 + NA_WIN_C)
    for e in range(NA_NENT):
        acc = jnp.zeros((GRID_W, LANES), F32)
        for o in range(NA_NCOFF):
            first_v = rpb_ref[(h * NA_NOFF + e - 1) * NA_NCOFF + o] if e >= 1 else 0.0
            second_v = rpb_ref[(h * NA_NOFF + e) * NA_NCOFF + o] if e < NA_NOFF else 0.0
            acc = jnp.where(col_off == o, jnp.where(second, second_v, first_v), acc)
        o_ref[0, e] = jnp.where(in_win, acc * LOG2E, NEG)


def _na_bias_tables(rpb):
    return pl.pallas_call(
        _na_table_kernel,
        grid=(NA_HEADS,),
        in_specs=[pl.BlockSpec(memory_space=pltpu.SMEM)],
        out_specs=pl.BlockSpec((1, NA_NENT, GRID_W, LANES), lambda h: (h, 0, 0, 0)),
        out_shape=jax.ShapeDtypeStruct((NA_HEADS, NA_NENT, GRID_W, LANES), F32),
        compiler_params=_cparams(1),
        name="na_bias_table",
    )(rpb.astype(F32).reshape(-1))


def _na_key_row0(j):
    return jnp.clip(NA_QROWS * j - NA_WIN_R // 2, 0, NA_ROWS - NA_KROWS)


def _na_kernel(q_ref, k_ref, v_ref, kc_ref, vc_ref, t_ref, o_ref):
    j = pl.program_id(1)
    u0 = _na_key_row0(j)
    start = pl.multiple_of(u0 * GRID_W, GRID_W)
    lo = _lane_lo((1, LANES))
    entry = []
    rowmask = []
    for i in range(NA_QROWS):
        r = NA_QROWS * j + i
        r0 = jnp.clip(r - NA_WIN_R // 2, 0, NA_ROWS - NA_WIN_R)
        entry.append([])
        rowmask.append([])
        for p in range(NA_KROWS // 2):
            a = u0 + 2 * p
            entry[i].append(jnp.clip(a - r + NA_WIN_R, 0, NA_NENT - 1))
            out_a = jnp.logical_or(a < r0, a >= r0 + NA_WIN_R)
            out_b = jnp.logical_or(a + 1 < r0, a + 1 >= r0 + NA_WIN_R)
            rowmask[i].append(jnp.where(lo, jnp.where(out_a, NEG, 0.0), jnp.where(out_b, NEG, 0.0)))
    for hp in range(NA_HEADS // 2):
        cols = slice(hp * LANES, (hp + 1) * LANES)
        bias_rows = []
        for head in range(2):
            for i in range(NA_QROWS):
                tiles = [t_ref[2 * hp + head, entry[i][p]] + rowmask[i][p] for p in range(NA_KROWS // 2)]
                bias_rows.append(jnp.concatenate(tiles, axis=1))
        bias = jnp.concatenate(bias_rows, axis=0)
        segs = [(k_ref[pl.ds(start, NA_NKEY), cols], v_ref[pl.ds(start, NA_NKEY), cols]),
                (kc_ref[:, cols], vc_ref[:, cols])]
        o_ref[:, cols] = _pair_attend(q_ref[:, cols], segs, [bias, None]).astype(o_ref.dtype)


def _na_attention(q, k, v, table, n_batch):
    ctx0 = n_batch * LAT_BLKS
    n_steps = NA_ROWS // NA_QROWS
    tq = NA_QROWS * GRID_W
    lat = pl.BlockSpec((N_LAT, D), lambda b, j: (b, 0))
    ctx = pl.BlockSpec((N_CTX, D), lambda b, j: (ctx0 + b, 0))
    return pl.pallas_call(
        _na_kernel,
        grid=(n_batch, n_steps),
        in_specs=[
            pl.BlockSpec((tq, D), lambda b, j: (b * n_steps + j, 0)),
            lat, lat, ctx, ctx,
            pl.BlockSpec(table.shape, lambda b, j: (0, 0, 0, 0), pipeline_mode=pl.Buffered(1)),
        ],
        out_specs=pl.BlockSpec((tq, D), lambda b, j: (b * n_steps + j, 0)),
        out_shape=jax.ShapeDtypeStruct((n_batch * N_LAT, D), BF16),
        compiler_params=_cparams(2),
        name="na_attention",
    )(q, k, v, k, v, table)


def _ctx_attn_kernel(q_ref, k_ref, v_ref, o_ref):
    for hp in range(NA_HEADS // 2):
        cols = slice(hp * LANES, (hp + 1) * LANES)
        segs = [(k_ref[:, cols], v_ref[:, cols])]
        o_ref[:, cols] = _pair_attend(q_ref[:, cols], segs, [None]).astype(o_ref.dtype)


def _ctx_attention(q, k, v, n_batch):
    ctx0 = n_batch * LAT_BLKS
    spec = pl.BlockSpec((N_CTX, D), lambda b: (ctx0 + b, 0))
    return pl.pallas_call(
        _ctx_attn_kernel,
        grid=(n_batch,),
        in_specs=[spec, spec, spec],
        out_specs=pl.BlockSpec((N_CTX, D), lambda b: (b, 0)),
        out_shape=jax.ShapeDtypeStruct((n_batch * N_CTX, D), BF16),
        compiler_params=_cparams(1),
        name="na_ctx_attention",
    )(q, k, v)


def _na_mixer(h, mod3, g, w_kvq, rpb, n_batch):
    segs = [(0, D, 1.0, BF16), (D, 2 * D, 1.0, BF16), (2 * D, 3 * D, NA_HD ** -0.5 * LOG2E, BF16)]
    k, v, q = _proj(h, mod3, g, w_kvq.astype(BF16), segs, n_batch)
    return _na_attention(q, k, v, _na_bias_tables(rpb), n_batch), _ctx_attention(q, k, v, n_batch)


DA_TQ = 1024
DA_W = 2 * DA_HD


def _rope_tables():
    t = np.arange(N_LAT)
    row = (t // GRID_W).astype(np.float64)
    col = (t % GRID_W).astype(np.float64)
    per_axis = DA_HD // 2
    freqs = ROPE_BASE ** (-np.arange(0, per_axis, 2, dtype=np.float64) / per_axis)
    ar = row[:, None] * freqs
    ac = col[:, None] * freqs
    ang = np.concatenate([ar, ar, ac, ac], axis=-1)
    cos = np.tile(np.cos(ang), (1, 2))
    sin = np.tile(np.sin(ang), (1, 2))
    quarter = (np.arange(DA_W) % DA_HD) // (DA_HD // 4)
    even = (quarter % 2 == 0)[None, :]
    sin_a = np.where(even, -sin, 0.0)
    sin_b = np.where(even, 0.0, sin)
    return tuple(jnp.asarray(a, F32) for a in (cos, sin_a, sin_b))


def _rope(x, cos, sin_a, sin_b):
    q16 = DA_HD // 4
    return x * cos + pltpu.roll(x, DA_W - q16, 1) * sin_a + pltpu.roll(x, q16, 1) * sin_b


def _da_lambda(lp, lam_init):
    a = jnp.sum(lp[0:1] * lp[1:2], axis=-1, keepdims=True)
    b = jnp.sum(lp[2:3] * lp[3:4], axis=-1, keepdims=True)
    return jnp.exp(a) - jnp.exp(b) + lam_init


DA_KCHUNK = 512


def _diff_attend(q, k, vx, lam, ng, lam_init):
    lo = _lane_lo(q.shape)
    qm = [_half_masked(q, lo, 0), _half_masked(q, lo, 1)]
    n_keys = k.shape[0]
    chunks = [(c0, min(c0 + DA_KCHUNK, n_keys)) for c0 in range(0, n_keys, DA_KCHUNK)]
    s = [[_dot_nt(qm[j], k[c0:c1, :]) for j in range(2)] for c0, c1 in chunks]
    m = [None, None]
    for sc in s:
        for j in range(2):
            mc = sc[j].max(axis=-1, keepdims=True)
            m[j] = mc if m[j] is None else jnp.maximum(m[j], mc)
    r = [None, None]
    for sc, (c0, c1) in zip(s, chunks):
        for j in range(2):
            pv = _dot(jnp.exp2(sc[j] - m[j]).astype(BF16), vx[c0:c1, :])
            r[j] = pv if r[j] is None else r[j] + pv
    o = r[0][:, :DA_W] / r[0][:, DA_W:] - lam * (r[1][:, :DA_W] / r[1][:, DA_W:])
    return o * lax.rsqrt(jnp.mean(o * o, axis=-1, keepdims=True) + EPS) * ng * (1.0 - lam_init)


def _da_kernel(lam_ref, q_ref, k_ref, v_ref, kc_ref, vc_ref, cq_ref, saq_ref, sbq_ref,
               ck_ref, sak_ref, sbk_ref, ng_ref, o_ref, kr_ref, vx_ref, *, lam_init):
    @pl.when(pl.program_id(2) == 0)
    def _():
        kr_ref[:N_LAT, :] = _rope(k_ref[...], ck_ref[...], sak_ref[...], sbk_ref[...]).astype(BF16)
        kr_ref[N_LAT:, :] = kc_ref[...].astype(BF16)
        vx_ref[:N_LAT, :DA_W] = v_ref[...]
        vx_ref[N_LAT:, :DA_W] = vc_ref[...]
        vx_ref[:, DA_W:] = jnp.ones((N_LAT + N_CTX, DA_W), BF16)

    lam = _da_lambda(lam_ref[...], lam_init)
    q = (_rope(q_ref[...], cq_ref[...], saq_ref[...], sbq_ref[...]) * LOG2E).astype(BF16)
    o_ref[...] = _diff_attend(q, kr_ref, vx_ref, lam, ng_ref[...], lam_init).astype(o_ref.dtype)


def _da_attention(q, k, v, lam_p, norm_g, lam_init, n_batch):
    cos, sin_a, sin_b = _rope_tables()
    nqb = N_LAT // DA_TQ
    ctx0 = n_batch * LAT_BLKS
    tab_q = pl.BlockSpec((DA_TQ, DA_W), lambda b, h, i: (i, 0))
    tab_k = pl.BlockSpec((N_LAT, DA_W), lambda b, h, i: (0, 0))
    lat = pl.BlockSpec((N_LAT, DA_W), lambda b, h, i: (b, h))
    ctx = pl.BlockSpec((N_CTX, DA_W), lambda b, h, i: (ctx0 + b, h))
    return pl.pallas_call(
        functools.partial(_da_kernel, lam_init=lam_init),
        grid=(n_batch, DA_HEADS, nqb),
        in_specs=[
            pl.BlockSpec((4, DA_HD), lambda b, h, i: (0, 0)),
            pl.BlockSpec((DA_TQ, DA_W), lambda b, h, i: (b * nqb + i, h)),
            lat, lat, ctx, ctx,
            tab_q, tab_q, tab_q, tab_k, tab_k, tab_k,
            pl.BlockSpec((1, DA_W), lambda b, h, i: (0, h)),
        ],
        out_specs=pl.BlockSpec((DA_TQ, DA_W), lambda b, h, i: (b * nqb + i, h)),
        out_shape=jax.ShapeDtypeStruct((n_batch * N_LAT, D), BF16),
        scratch_shapes=[pltpu.VMEM((N_LAT + N_CTX, DA_W), BF16), pltpu.VMEM((N_LAT + N_CTX, 2 * DA_W), BF16)],
        compiler_params=_cparams(3),
        name="diff_attention",
    )(lam_p, q, k, v, k, v, cos, sin_a, sin_b, cos, sin_a, sin_b, norm_g.reshape(1, D))


def _da_ctx_kernel(lam_ref, q_ref, k_ref, v_ref, ng_ref, o_ref, *, lam_init):
    lam = _da_lambda(lam_ref[...], lam_init)
    v = v_ref[...]
    vx = jnp.concatenate([v, jnp.ones_like(v)], axis=1)
    q = (q_ref[...] * LOG2E).astype(BF16)
    o_ref[...] = _diff_attend(q, k_ref[...].astype(BF16), vx, lam, ng_ref[...], lam_init).astype(o_ref.dtype)


def _da_ctx_attention(q, k, v, lam_p, norm_g, lam_init, n_batch):
    ctx0 = n_batch * LAT_BLKS
    spec = pl.BlockSpec((N_CTX, DA_W), lambda b, h: (ctx0 + b, h))
    return pl.pallas_call(
        functools.partial(_da_ctx_kernel, lam_init=lam_init),
        grid=(n_batch, DA_HEADS),
        in_specs=[pl.BlockSpec((4, DA_HD), lambda b, h: (0, 0)), spec, spec, spec,
                  pl.BlockSpec((1, DA_W), lambda b, h: (0, h))],
        out_specs=pl.BlockSpec((N_CTX, DA_W), lambda b, h: (b, h)),
        out_shape=jax.ShapeDtypeStruct((n_batch * N_CTX, D), BF16),
        compiler_params=_cparams(2),
        name="diff_ctx_attention",
    )(lam_p, q, k, v, norm_g.reshape(1, D))


def _da_mixer(h, mod3, g, w_kvq, lam_p, norm_g, layer_idx, n_batch):
    lam_init = 0.8 - 0.6 * float(np.exp(-0.3 * layer_idx))
    segs = [(0, D, 1.0, F32), (D, 2 * D, 1.0, BF16), (2 * D, 3 * D, DA_HD ** -0.5, F32)]
    k, v, q = _proj(h, mod3, g, w_kvq.astype(BF16), segs, n_batch)
    lam_p = lam_p.astype(F32)
    return (_da_attention(q, k, v, lam_p, norm_g, lam_init, n_batch),
            _da_ctx_attention(q, k, v, lam_p, norm_g, lam_init, n_batch))


SCAN_STEPS = 1 + LAT_BLKS


def _scan_row_blk(direction, n_batch):
    ctx0 = n_batch * LAT_BLKS

    def blk(b, s):
        lat = b * LAT_BLKS + (s - 1 if direction == 0 else LAT_BLKS - s)
        return jnp.where(s == 0, ctx0 + b, lat)
    return blk


def _scan_spec(width, direction, n_batch):
    blk = _scan_row_blk(direction, n_batch)
    return pl.BlockSpec((ROW_BLK, width), lambda b, h, s: (blk(b, s), h))


def _scan_spec_all(width, direction, n_batch):
    blk = _scan_row_blk(direction, n_batch)
    return pl.BlockSpec((ROW_BLK, width), lambda b, h, s: (blk(b, s), 0))


def _scan_out_specs(width):
    return [pl.BlockSpec((N_LAT, width), lambda b, h, s: (b, h)), pl.BlockSpec((N_CTX, width), lambda b, h, s: (b, h))]


def _scan_write_sum(step, out_fwd, out_bwd, lat_ref, ctx_ref):
    @pl.when(step == 0)
    def _():
        ctx_ref[...] = out_fwd + out_bwd

    rows_f = pl.ds(pl.multiple_of((step - 1) * ROW_BLK, ROW_BLK), ROW_BLK)
    rows_b = pl.ds(pl.multiple_of((LAT_BLKS - step) * ROW_BLK, ROW_BLK), ROW_BLK)

    @pl.when(jnp.logical_and(step >= 1, step <= LAT_BLKS // 2))
    def _():
        lat_ref[rows_f, :] = out_fwd
        lat_ref[rows_b, :] = out_bwd

    @pl.when(step > LAT_BLKS // 2)
    def _():
        lat_ref[rows_f, :] += out_fwd
        lat_ref[rows_b, :] += out_bwd


def _ml_conv_kernel(xk_ref, xkp_ref, xkn_ref, xq_ref, xqp_ref, xqn_ref, w_ref, ok_ref, oq_ref, *, n_lat_blks):
    i = pl.program_id(0)
    is_ctx = i >= n_lat_blks
    first = jnp.logical_or(is_ctx, i % LAT_BLKS == 0)
    last = jnp.logical_or(is_ctx, i % LAT_BLKS == LAT_BLKS - 1)
    rid = lax.broadcasted_iota(jnp.int32, (ROW_BLK, D), 0)
    groups = ((xk_ref, xkp_ref, xkn_ref, 0, ok_ref, ML_HD ** -0.5), (xq_ref, xqp_ref, xqn_ref, D, oq_ref, 1.0))
    for x_ref, xp_ref, xn_ref, c0, o_ref, scale in groups:
        x = x_ref[...]
        prev_row = jnp.where(first, 0.0, xp_ref[7:8, :])
        next_row = jnp.where(last, 0.0, xn_ref[0:1, :])
        x_prev = jnp.where(rid == 0, prev_row, pltpu.roll(x, 1, 0))
        x_next = jnp.where(rid == ROW_BLK - 1, next_row, pltpu.roll(x, ROW_BLK - 1, 0))
        y = w_ref[0:1, c0:c0 + D] * x_prev + w_ref[1:2, c0:c0 + D] * x + w_ref[2:3, c0:c0 + D] * x_next
        o_ref[...] = (_silu(y) * scale).astype(o_ref.dtype)


def _ml_conv(pk, pq, conv_w, n_batch):
    n_rows = pk.shape[0]
    n_blk = n_rows // ROW_BLK
    sub = ROW_BLK // 8
    main = pl.BlockSpec((ROW_BLK, D), lambda i: (i, 0))
    prev = pl.BlockSpec((8, D), lambda i: (jnp.maximum(i * sub - 1, 0), 0))
    nxt = pl.BlockSpec((8, D), lambda i: (jnp.minimum((i + 1) * sub, n_blk * sub - 1), 0))
    return pl.pallas_call(
        functools.partial(_ml_conv_kernel, n_lat_blks=n_batch * LAT_BLKS),
        grid=(n_blk,),
        in_specs=[main, prev, nxt, main, prev, nxt, pl.BlockSpec((3, 2 * D), lambda i: (0, 0))],
        out_specs=[main, main],
        out_shape=[jax.ShapeDtypeStruct((n_rows, D), BF16)] * 2,
        compiler_params=_cparams(1),
        name="mlstm_conv",
    )(pk, pk, pk, pq, pq, pq, conv_w)


def _pick_lane(x, idx):
    lane = lax.broadcasted_iota(jnp.int32, x.shape, 1)
    return jnp.sum(jnp.where(lane == idx, x, 0.0), axis=1, keepdims=True)


def _pick_sublane(x, idx):
    sub = lax.broadcasted_iota(jnp.int32, x.shape, 0)
    return jnp.sum(jnp.where(sub == idx, x, 0.0), axis=0, keepdims=True)


def _ml_scan_kernel(gb_ref, g0_ref, q0_ref, k0_ref, v0_ref, g1_ref, q1_ref, k1_ref, v1_ref,
                    lat_ref, ctx_ref, c_ref, n_ref, m_ref):
    head = pl.program_id(1)

    @pl.when(pl.program_id(2) == 0)
    def _():
        c_ref[...] = jnp.zeros_like(c_ref)
        n_ref[...] = jnp.zeros_like(n_ref)
        m_ref[...] = jnp.zeros_like(m_ref)

    r = lax.broadcasted_iota(jnp.int32, (ROW_BLK, ROW_BLK), 0)
    c = lax.broadcasted_iota(jnp.int32, (ROW_BLK, ROW_BLK), 1)
    dirs = ((g0_ref, q0_ref, k0_ref, v0_ref), (g1_ref, q1_ref, k1_ref, v1_ref))
    outs = []
    for d, (g_ref, q_ref, k_ref, v_ref) in enumerate(dirs):
        valid = (r >= c) if d == 0 else (r <= c)
        other = (r <= c) if d == 0 else (r >= c)
        g = g_ref[...] + gb_ref[...]
        gt = g.T
        ii = d * 2 * ML_HEADS + head
        fi = ii + ML_HEADS
        i_col = _pick_lane(g, ii)
        f_col = _log_sigmoid(_pick_lane(g, fi))
        i_row = _pick_sublane(gt, ii)
        f_row = _log_sigmoid(_pick_sublane(gt, fi))
        b_col = jnp.sum(jnp.where(valid, f_row, 0.0), axis=1, keepdims=True)
        b_row = jnp.sum(jnp.where(other, f_col, 0.0), axis=0, keepdims=True)
        total = jnp.sum(f_col, axis=0, keepdims=True)
        m_prev = m_ref[d][0:1, 0:1]
        n_prev = n_ref[d][0:1, :]
        c_prev = c_ref[d]
        qc = q_ref[...]
        kc = k_ref[...]
        vc = v_ref[...]

        dmat = jnp.where(valid, b_col - b_row + i_row, NEG)
        inter = b_col + m_prev
        m_t = jnp.maximum(inter, dmat.max(axis=1, keepdims=True))
        w_inter = jnp.exp(inter - m_t)
        s = _dot_nt(qc, kc) * jnp.exp(dmat - m_t)
        num = w_inter * _dot(qc, c_prev.astype(BF16)) + _dot(s.astype(BF16), vc)
        den = w_inter * jnp.sum(qc.astype(F32) * n_prev, axis=1, keepdims=True) + s.sum(axis=1, keepdims=True)
        outs.append(num / jnp.maximum(jnp.abs(den), jnp.exp(-m_t)))

        g_col = total - b_col + i_col
        m_new = jnp.maximum(total + m_prev, g_col.max(axis=0, keepdims=True))
        w_old = jnp.exp(total + m_prev - m_new)
        kw = kc.astype(F32) * jnp.exp(g_col - m_new)
        c_ref[d] = w_old * c_prev + _dot_tn(kw.astype(BF16), vc)
        n_ref[d] = jnp.broadcast_to(w_old * n_prev + kw.sum(axis=0, keepdims=True), n_ref.shape[1:])
        m_ref[d] = jnp.broadcast_to(m_new, m_ref.shape[1:])
    _scan_write_sum(pl.program_id(2), outs[0], outs[1], lat_ref, ctx_ref)


def _ml_scan(q, k, v, gates, gate_b, n_batch):
    in_specs = [pl.BlockSpec((1, LANES), lambda b, h, s: (0, 0))]
    for d in range(2):
        in_specs += [_scan_spec_all(LANES, d, n_batch)] + [_scan_spec(ML_HD, d, n_batch)] * 3
    return pl.pallas_call(
        _ml_scan_kernel,
        grid=(n_batch, ML_HEADS, SCAN_STEPS),
        in_specs=in_specs,
        out_specs=_scan_out_specs(ML_HD),
        out_shape=[jax.ShapeDtypeStruct((n_batch * N_LAT, D), F32), jax.ShapeDtypeStruct((n_batch * N_CTX, D), F32)],
        scratch_shapes=[pltpu.VMEM((2, ML_HD, ML_HD), F32), pltpu.VMEM((2, 8, ML_HD), F32),
                        pltpu.VMEM((2, 8, LANES), F32)],
        compiler_params=_cparams(3),
        name="mlstm_scan",
    )(gate_b, gates, q, k, v, gates, q, k, v)


def _ml_mixer(h, mod3, g, w_in, gate_b, conv_w, n_batch):
    n_g = 4 * ML_HEADS
    wk, wv, wg, wq, wo = (w_in[:, :D], w_in[:, D:2 * D], w_in[:, 2 * D:2 * D + n_g],
                          w_in[:, 2 * D + n_g:3 * D + n_g], w_in[:, 3 * D + n_g:])
    w = jnp.concatenate([wk, wv, wq, wo, jnp.pad(wg, ((0, 0), (0, LANES - n_g)))], axis=1).astype(BF16)
    segs = [(0, D, 1.0, F32), (D, 2 * D, 1.0, BF16), (2 * D, 3 * D, 1.0, F32), (3 * D, 4 * D, 1.0, F32),
            (4 * D, 4 * D + LANES, 1.0, F32)]
    pk, v, pq, og, gates = _proj(h, mod3, g, w, segs, n_batch)
    k, q = _ml_conv(pk, pq, conv_w.astype(F32), n_batch)
    gb = jnp.pad(gate_b.astype(F32), (0, LANES - n_g)).reshape(1, LANES)
    h_lat, h_ctx = _ml_scan(q, k, v, gates, gb, n_batch)
    return h_lat, h_ctx, og


GLA_LEVELS = (256, 128, 64, 32, 16, 8, 4, 2)
GLA_BCAST_LEVELS = 6
GLA_DIAG = len(GLA_LEVELS)


def _gla_tables():
    u = np.arange(ROW_BLK)
    tt, ss = np.meshgrid(u, u, indexing="ij")
    hb = np.floor(np.log2(np.maximum(tt ^ ss, 1))).astype(np.int64)
    level = len(GLA_LEVELS) - 1 - hb
    tri = np.stack([ss <= tt, ss >= tt]).astype(np.float32)
    lid = np.stack([np.where(tt == ss, GLA_DIAG, np.where(tt > ss, level, -1)),
                    np.where(tt == ss, GLA_DIAG, np.where(tt < ss, level, -1))]).astype(np.int32)
    return jnp.asarray(tri, BF16), jnp.asarray(lid)


def _split3(x):
    hi = x.astype(BF16)
    r1 = x - hi.astype(F32)
    mid = r1.astype(BF16)
    lo = (r1 - mid.astype(F32)).astype(BF16)
    return hi, mid, lo


def _gla_level_exponents(level, d, bc, bc_ref, la, la_prev, la_next, row):
    n = GLA_LEVELS[level]
    half = n // 2
    if level < GLA_BCAST_LEVELS:
        parts = []
        for j in range(ROW_BLK // n):
            mid = j * n + (half - 1 if d == 0 else half)
            parts.append(jnp.broadcast_to(bc_ref[mid:mid + 1, :], (n, GLA_DK)))
        x = bc - (parts[0] if len(parts) == 1 else jnp.concatenate(parts, axis=0))
        return jnp.minimum(x, 0.0), jnp.minimum(-x, 0.0)
    o = row & (n - 1)
    if n == 4:
        if d == 0:
            return (jnp.where(o == 2, la, jnp.where(o == 3, la + la_prev, 0.0)), jnp.where(o == 0, la_next, 0.0))
        return (jnp.where(o == 0, la + la_next, jnp.where(o == 1, la, 0.0)), jnp.where(o == 3, la_prev, 0.0))
    return jnp.where(o == (1 if d == 0 else 0), la, 0.0), None


def _gla_kernel(tri_ref, lid_ref, wup_ref, bg_ref, q0_ref, k0_ref, v0_ref, l0_ref, q1_ref, k1_ref, v1_ref, l1_ref,
                lat_ref, ctx_ref, st_ref, bc_ref, att_ref):
    @pl.when(pl.program_id(2) == 0)
    def _():
        st_ref[...] = jnp.zeros_like(st_ref)

    row = lax.broadcasted_iota(jnp.int32, (ROW_BLK, GLA_DK), 0)
    dirs = ((q0_ref, k0_ref, v0_ref, l0_ref), (q1_ref, k1_ref, v1_ref, l1_ref))
    outs = []
    for d, (q_ref, k_ref, v_ref, low_ref) in enumerate(dirs):
        pre = _dot(low_ref[...].astype(BF16), wup_ref[d]) + bg_ref[d]
        la = _log_sigmoid(pre) * (LOG2E / GLA_TAU)
        hi, mid, lo = _split3(la)
        tri = tri_ref[d]
        bc = _dot(tri, hi) + _dot(tri, mid) + _dot(tri, lo)
        bc_ref[...] = bc
        la_prev = pltpu.roll(la, 1, 0)
        la_next = pltpu.roll(la, ROW_BLK - 1, 0)
        q = q_ref[...]
        k = k_ref[...]
        v = v_ref[...]
        lid = lid_ref[d]
        kb = k.astype(BF16)
        att_ref[...] = jnp.where(lid == GLA_DIAG, _dot_nt(q.astype(BF16), kb), 0.0)
        for level in range(len(GLA_LEVELS)):
            eq, ek = _gla_level_exponents(level, d, bc, bc_ref, la, la_prev, la_next, row)
            ql = (q * jnp.exp2(eq)).astype(BF16)
            kl = kb if ek is None else (k * jnp.exp2(ek)).astype(BF16)
            att_ref[...] = jnp.where(lid == level, _dot_nt(ql, kl), att_ref[...])
        st = st_ref[d]
        qd = (q * jnp.exp2(bc)).astype(BF16)
        outs.append(_dot(att_ref[...].astype(BF16), v) + _dot_nt(qd, st.astype(BF16)))
        b_end = bc[ROW_BLK - 1:ROW_BLK] if d == 0 else bc[0:1]
        kd = (k * jnp.exp2(b_end - bc)).astype(BF16)
        st_ref[d] = st * jnp.exp2(b_end) + _dot_tn(v, kd)
    _scan_write_sum(pl.program_id(2), outs[0], outs[1], lat_ref, ctx_ref)


def _gla_scan(q, k, v, low, wup, bg, n_batch):
    tri, lid = _gla_tables()
    const3 = lambda b, h, s: (0, 0, 0)
    in_specs = [
        pl.BlockSpec(tri.shape, const3),
        pl.BlockSpec(lid.shape, const3),
        pl.BlockSpec((2, LANES, GLA_DK), lambda b, h, s: (0, 0, h)),
        pl.BlockSpec((2, 1, GLA_DK), lambda b, h, s: (0, 0, h)),
    ]
    for d in range(2):
        in_specs += [_scan_spec(GLA_DK, d, n_batch), _scan_spec(GLA_DK, d, n_batch),
                     _scan_spec(GLA_DV, d, n_batch), _scan_spec_all(LANES, d, n_batch)]
    return pl.pallas_call(
        _gla_kernel,
        grid=(n_batch, GLA_HEADS, SCAN_STEPS),
        in_specs=in_specs,
        out_specs=_scan_out_specs(GLA_DV),
        out_shape=[jax.ShapeDtypeStruct((n_batch * N_LAT, D), F32), jax.ShapeDtypeStruct((n_batch * N_CTX, D), F32)],
        scratch_shapes=[pltpu.VMEM((2, GLA_DV, GLA_DK), F32), pltpu.VMEM((ROW_BLK, GLA_DK), F32),
                        pltpu.VMEM((ROW_BLK, ROW_BLK), F32)],
        compiler_params=_cparams(3),
        name="gla_scan",
    )(tri, lid, wup, bg, q, k, v, low, q, k, v, low)


def _gla_mixer(h, mod3, g, w_in, w_gate_up, b_gate, n_batch):
    dk_t = GLA_HEADS * GLA_DK
    n_low = 2 * GLA_RANK
    wk, wv, wl, wq, wr = (w_in[:, :dk_t], w_in[:, dk_t:dk_t + D], w_in[:, dk_t + D:dk_t + D + n_low],
                          w_in[:, dk_t + D + n_low:2 * dk_t + D + n_low], w_in[:, 2 * dk_t + D + n_low:])
    w = jnp.concatenate([wk, wq, wv, wr, jnp.pad(wl, ((0, 0), (0, LANES - n_low)))], axis=1).astype(BF16)
    c = [0, dk_t, 2 * dk_t, 2 * dk_t + D, 2 * dk_t + 2 * D, 2 * dk_t + 2 * D + LANES]
    segs = [(c[0], c[1], 1.0, F32), (c[1], c[2], GLA_DK ** -0.5, F32), (c[2], c[3], 1.0, BF16),
            (c[3], c[4], 1.0, F32), (c[4], c[5], 1.0, F32)]
    k, q, v, rg, low = _proj(h, mod3, g, w, segs, n_batch)
    wup = jnp.zeros((2, LANES, dk_t), F32)
    for r in range(2):
        wup = wup.at[r, r * GLA_RANK:(r + 1) * GLA_RANK].set(w_gate_up[r].astype(F32))
    o_lat, o_ctx = _gla_scan(q, k, v, low, wup.astype(BF16), b_gate.astype(F32).reshape(2, 1, dk_t), n_batch)
    return o_lat, o_ctx, rg


def kernel(x, c, ctx, c_ctx, ada_w, ada_b, norm_g, ffn_w13, ffn_w2, final_g, na_w_kvq, na_rpb, na_w_o, ml_w_in, ml_gate_b, ml_conv_w, ml_norm_g, ml_w_o, da_w_kvq, da_lam, da_norm_g, da_w_o, gla_w_in, gla_w_gate_up, gla_b_gate, gla_norm_g, gla_w_o):
    nb = x.shape[0]
    assert x.shape[1:] == (N_LAT, D) and ctx.shape[1:] == (N_CTX, D) and nb < MOD_ROWS
    n_lat = nb * N_LAT
    n_tok = n_lat + nb * N_CTX

    s = jnp.zeros((MOD_ROWS, D), F32).at[:nb].set(c).at[nb].set(c_ctx)
    modtab = _mod_tables(s, ada_w, ada_b)
    h = (x.reshape(n_lat, D), ctx.reshape(nb * N_CTX, D))

    for i in range(DEPTH):
        kind, j = i % 4, i // 4
        last = i == DEPTH - 1
        mod3 = modtab[i].reshape(MOD_ROWS * N_MOD, 1, D)
        h = _ffn(h, mod3, norm_g[i, 0], ffn_w13[i, 0].astype(BF16), ffn_w2[i, 0].astype(BF16), final_g, 0, n_tok, nb)
        n_out = n_lat if last else n_tok
        if kind == 0:
            ins = _na_mixer(h, mod3, norm_g[i, 1], na_w_kvq[j], na_rpb[j], nb)
            mixer, w_o, mixer_g = "plain", na_w_o[j], None
        elif kind == 1:
            ins = _ml_mixer(h, mod3, norm_g[i, 1], ml_w_in[j], ml_gate_b[j], ml_conv_w[j], nb)
            mixer, w_o, mixer_g = "ml", ml_w_o[j], ml_norm_g[j]
        elif kind == 2:
            ins = _da_mixer(h, mod3, norm_g[i, 1], da_w_kvq[j], da_lam[j], da_norm_g[j], i, nb)
            mixer, w_o, mixer_g = "plain", da_w_o[j], None
        else:
            ins = _gla_mixer(h, mod3, norm_g[i, 1], gla_w_in[j], gla_w_gate_up[j], gla_b_gate[j], nb)
            mixer, w_o, mixer_g = "gla", gla_w_o[j], gla_norm_g[j]
        h = _ffn(h, mod3, norm_g[i, 2], ffn_w13[i, 1].astype(BF16), ffn_w2[i, 1].astype(BF16), final_g, 6, n_out, nb,
                 final=last, mixer=mixer, mixer_ins=ins, w_o=w_o.astype(BF16), mixer_g=mixer_g)
    return h.reshape(nb, N_LAT, D)
```

```python
import functools

import numpy as np
import jax
import jax.numpy as jnp
from jax import lax
from jax.experimental import pallas as pl
from jax.experimental.pallas import tpu as pltpu

F32 = jnp.float32
BF16 = jnp.bfloat16

D = 1024
N_LAT = 2048
N_CTX = 256
DEPTH = 4
N_MOD = 9
D_FF = 2816
EPS = 1e-6
NEG = -1e30
LOG2E = 1.4426950408889634
GRID_W = 64

NA_HEADS = 16
NA_HD = 64
NA_WIN_R = 8
NA_WIN_C = 16

ML_HEADS = 4
ML_HD = 256

DA_HEADS = 8
DA_HD = 64
ROPE_BASE = 10000.0

GLA_HEADS = 4
GLA_DK = 128
GLA_DV = 256
GLA_RANK = 16
GLA_TAU = 16.0

ROW_BLK = 256
LAT_BLKS = N_LAT // ROW_BLK
MOD_ROWS = 16
LANES = 128
VMEM_LIMIT = 56 * 1024 * 1024


def _cparams(n_axes, vmem=VMEM_LIMIT):
    return pltpu.CompilerParams(dimension_semantics=("arbitrary",) * n_axes, vmem_limit_bytes=vmem)


def _sigmoid(x):
    return 1.0 / (1.0 + jnp.exp(-x))


def _silu(x):
    return x * _sigmoid(x)


def _log_sigmoid(x):
    return jnp.minimum(x, 0.0) - jnp.log(1.0 + jnp.exp(-jnp.abs(x)))


def _rms(x, g):
    return x * lax.rsqrt(jnp.mean(x * x, axis=-1, keepdims=True) + EPS) * g


def _dot(a, b):
    return jnp.dot(a, b, preferred_element_type=F32)


def _dot_nt(a, b):
    return lax.dot_general(a, b, (((1,), (1,)), ((), ())), preferred_element_type=F32)


def _dot_tn(a, b):
    return lax.dot_general(a, b, (((0,), (0,)), ((), ())), preferred_element_type=F32)


def _mod_row(i, tm, n_batch):
    return jnp.minimum(i // (N_LAT // tm), n_batch)


def _mod_spec(k, tm, n_batch):
    return pl.BlockSpec((1, 1, D), lambda i: (_mod_row(i, tm, n_batch) * N_MOD + k, 0, 0))


def _mod_kernel(s_ref, w_ref, b_ref, o_ref):
    a = _silu(s_ref[...]).astype(BF16)
    o_ref[0] = _dot(a, w_ref[0].astype(BF16)) + b_ref[0]


def _mod_tables(s, ada_w, ada_b):
    return pl.pallas_call(
        _mod_kernel,
        grid=(DEPTH, N_MOD),
        in_specs=[
            pl.BlockSpec((MOD_ROWS, D), lambda i, k: (0, 0)),
            pl.BlockSpec((1, D, D), lambda i, k: (i, 0, k)),
            pl.BlockSpec((1, 1, D), lambda i, k: (i, 0, k)),
        ],
        out_specs=pl.BlockSpec((1, MOD_ROWS, D), lambda i, k: (i, 0, k)),
        out_shape=jax.ShapeDtypeStruct((DEPTH, MOD_ROWS, N_MOD * D), F32),
        compiler_params=_cparams(2),
        name="mod_tables",
    )(s, ada_w, ada_b.reshape(DEPTH, 1, N_MOD * D))


FFN_TM = 512
MXU_TILE = 256
FFN_SPLIT = (D_FF // MXU_TILE + 1) // 2 * MXU_TILE
FFN_CHUNKS = ((0, FFN_SPLIT), (FFN_SPLIT, D_FF))


def _head_rms(x, g, n_heads):
    hd = D // n_heads
    parts = []
    for i in range(n_heads):
        xi = x[:, i * hd:(i + 1) * hd]
        parts.append(xi * lax.rsqrt(jnp.mean(xi * xi, axis=-1, keepdims=True) + EPS))
    return jnp.concatenate(parts, axis=-1) * g


def _mixer_readout(mode, refs, is_lat):
    y = jnp.where(is_lat, refs[0][...], refs[1][...])
    if mode == "ml":
        return _sigmoid(refs[2][...]) * _head_rms(y, refs[3][...], ML_HEADS)
    if mode == "gla":
        return _head_rms(y, refs[3][...], GLA_HEADS) * _silu(refs[2][...])
    return y


def _ffn_kernel(g_ref, sh_ref, sc_ref, gt_ref, w13_ref, w2_ref, fg_ref, *refs, final, mixer, split_h, n_lat_tiles):
    o_ref = refs[-1]
    if split_h:
        x = jnp.where(pl.program_id(0) < n_lat_tiles, refs[0][...], refs[1][...])
        refs = refs[2:]
    else:
        x = refs[0][...]
        refs = refs[1:]
    if mixer is not None:
        mg_ref, wo_ref = refs[:2]
        y = _mixer_readout(mixer, refs[2:-1], pl.program_id(0) < n_lat_tiles)
        x = x + mg_ref[0] * _dot(y.astype(BF16), wo_ref[...])
    a = (_rms(x, g_ref[...]) * (1.0 + sc_ref[0]) + sh_ref[0]).astype(BF16)
    acc = jnp.zeros(x.shape, F32)
    for c0, c1 in FFN_CHUNKS:
        gate = _dot(a, w13_ref[:, c0:c1])
        up = _dot(a, w13_ref[:, D_FF + c0:D_FF + c1])
        act = (_silu(gate) * up).astype(BF16)
        acc = acc + _dot(act, w2_ref[c0:c1, :])
    y = x + 0.5 * gt_ref[0] * acc
    if final:
        y = _rms(y, fg_ref[...])
    o_ref[...] = y


def _ffn(h, mod3, g, w13, w2, final_g, k0, n_rows, n_batch, final=False, mixer=None, mixer_ins=(), w_o=None,
         mixer_g=None):
    tm = FFN_TM
    nl = n_batch * N_LAT // tm
    const = lambda i: (0, 0)
    row = pl.BlockSpec((tm, D), lambda i: (i, 0))
    lat_ctx = [pl.BlockSpec((tm, D), lambda i: (jnp.minimum(i, nl - 1), 0)),
               pl.BlockSpec((tm, D), lambda i: (jnp.maximum(i - nl, 0), 0))]
    split_h = isinstance(h, tuple)
    in_specs = [
        pl.BlockSpec((1, D), const),
        _mod_spec(k0, tm, n_batch),
        _mod_spec(k0 + 1, tm, n_batch),
        _mod_spec(k0 + 2, tm, n_batch),
        pl.BlockSpec((D, 2 * D_FF), const, pipeline_mode=pl.Buffered(1)),
        pl.BlockSpec((D_FF, D), const, pipeline_mode=pl.Buffered(1)),
        pl.BlockSpec((1, D), const),
    ] + (lat_ctx if split_h else [row])
    args = [g.reshape(1, D), mod3, mod3, mod3, w13, w2, final_g.reshape(1, D)] + (list(h) if split_h else [h])
    if mixer is not None:
        in_specs += [_mod_spec(5, tm, n_batch), pl.BlockSpec((D, D), const, pipeline_mode=pl.Buffered(1))] + lat_ctx
        args += [mod3, w_o, mixer_ins[0], mixer_ins[1]]
        if mixer != "plain":
            in_specs += [row, pl.BlockSpec((1, D), const)]
            args += [mixer_ins[2], mixer_g.reshape(1, D)]
    return pl.pallas_call(
        functools.partial(_ffn_kernel, final=final, mixer=mixer, split_h=split_h, n_lat_tiles=nl),
        grid=(n_rows // tm,),
        in_specs=in_specs,
        out_specs=row,
        out_shape=jax.ShapeDtypeStruct((n_rows, D), F32),
        compiler_params=_cparams(1),
        name="ffn" if mixer is None else "mixer_out_ffn",
    )(*args)


PROJ_TM = 512


def _proj_kernel(h_ref, g_ref, sh_ref, sc_ref, w_ref, *o_refs, segs):
    x = h_ref[...]
    a = (_rms(x, g_ref[...]) * (1.0 + sc_ref[0]) + sh_ref[0]).astype(BF16)
    for o_ref, (c0, c1, scale) in zip(o_refs, segs):
        y = _dot(a, w_ref[:, c0:c1])
        if scale != 1.0:
            y = y * scale
        o_ref[...] = y.astype(o_ref.dtype)


def _proj(h, mod3, g, w, segs, n_batch):
    tm = PROJ_TM
    n_rows = h.shape[0]
    const = lambda i: (0, 0)
    return pl.pallas_call(
        functools.partial(_proj_kernel, segs=tuple((c0, c1, sc) for c0, c1, sc, _ in segs)),
        grid=(n_rows // tm,),
        in_specs=[
            pl.BlockSpec((tm, D), lambda i: (i, 0)),
            pl.BlockSpec((1, D), const),
            _mod_spec(3, tm, n_batch),
            _mod_spec(4, tm, n_batch),
            pl.BlockSpec(w.shape, const, pipeline_mode=pl.Buffered(1)),
        ],
        out_specs=[pl.BlockSpec((tm, c1 - c0), lambda i: (i, 0)) for c0, c1, _, _ in segs],
        out_shape=[jax.ShapeDtypeStruct((n_rows, c1 - c0), dt) for c0, c1, _, dt in segs],
        compiler_params=_cparams(1),
        name="mixer_proj",
    )(h, g.reshape(1, D), mod3, mod3, w)


def _lane_lo(shape):
    return lax.broadcasted_iota(jnp.int32, shape, 1) < (LANES // 2)


def _half_masked(x, lo, head):
    return jnp.where(lo if head == 0 else jnp.logical_not(lo), x, jnp.zeros_like(x))


def _pair_attend(qp, segs, biases):
    tq = qp.shape[0]
    lo = _lane_lo(qp.shape)
    q2 = jnp.concatenate([_half_masked(qp, lo, 0), _half_masked(qp, lo, 1)], axis=0)
    scores = []
    for (k, _), b in zip(segs, biases):
        s = _dot_nt(q2, k)
        scores.append(s if b is None else s + b)
    m = scores[0].max(axis=-1, keepdims=True)
    for s in scores[1:]:
        m = jnp.maximum(m, s.max(axis=-1, keepdims=True))
    acc = None
    for s, (_, v) in zip(scores, segs):
        vx = jnp.concatenate([v, jnp.ones_like(v)], axis=1)
        pv = _dot(jnp.exp2(s - m).astype(BF16), vx)
        acc = pv if acc is None else acc + pv
    o2 = acc[:, :LANES] / acc[:, LANES:]
    return jnp.where(lo, o2[:tq], o2[tq:])


NA_ROWS = N_LAT // GRID_W
NA_QROWS = 4
NA_KROWS = 12
NA_NKEY = NA_KROWS * GRID_W
NA_NOFF = 2 * NA_WIN_R - 1
NA_NENT = NA_NOFF + 1


NA_NCOFF = 2 * NA_WIN_C - 1


def _na_table_kernel(rpb_ref, o_ref):
    h = pl.program_id(0)
    c = lax.broadcasted_iota(jnp.int32, (GRID_W, LANES), 0)
    lane = lax.broadcasted_iota(jnp.int32, (GRID_W, LANES), 1)
    kc = lane & (GRID_W - 1)
    second = lane >= GRID_W
    col_off = jnp.clip(kc - c, 1 - NA_WIN_C, NA_WIN_C - 1) + NA_WIN_C - 1
    win0 = jnp.clip(c - NA_WIN_C // 2, 0, GRID_W - NA_WIN_C)
    in_win = jnp.logical_and(kc >= win0, kc < win0 + NA_WIN_C)
    for e in range(NA_NENT):
        acc = jnp.zeros((GRID_W, LANES), F32)
        for o in range(NA_NCOFF):
            first_v = rpb_ref[(h * NA_NOFF + e - 1) * NA_NCOFF + o] if e >= 1 else 0.0
            second_v = rpb_ref[(h * NA_NOFF + e) * NA_NCOFF + o] if e < NA_NOFF else 0.0
            acc = jnp.where(col_off == o, jnp.where(second, second_v, first_v), acc)
        o_ref[0, e] = jnp.where(in_win, acc * LOG2E, NEG)


def _na_bias_tables(rpb):
    return pl.pallas_call(
        _na_table_kernel,
        grid=(NA_HEADS,),
        in_specs=[pl.BlockSpec(memory_space=pltpu.SMEM)],
        out_specs=pl.BlockSpec((1, NA_NENT, GRID_W, LANES), lambda h: (h, 0, 0, 0)),
        out_shape=jax.ShapeDtypeStruct((NA_HEADS, NA_NENT, GRID_W, LANES), F32),
        compiler_params=_cparams(1),
        name="na_bias_table",
    )(rpb.astype(F32).reshape(-1))


def _na_key_row0(j):
    return jnp.clip(NA_QROWS * j - NA_WIN_R // 2, 0, NA_ROWS - NA_KROWS)


def _na_kernel(q_ref, k_ref, v_ref, kc_ref, vc_ref, t_ref, o_ref):
    j = pl.program_id(1)
    u0 = _na_key_row0(j)
    start = pl.multiple_of(u0 * GRID_W, GRID_W)
    lo = _lane_lo((1, LANES))
    entry = []
    rowmask = []
    for i in range(NA_QROWS):
        r = NA_QROWS * j + i
        r0 = jnp.clip(r - NA_WIN_R // 2, 0, NA_ROWS - NA_WIN_R)
        entry.append([])
        rowmask.append([])
        for p in range(NA_KROWS // 2):
            a = u0 + 2 * p
            entry[i].append(jnp.clip(a - r + NA_WIN_R, 0, NA_NENT - 1))
            out_a = jnp.logical_or(a < r0, a >= r0 + NA_WIN_R)
            out_b = jnp.logical_or(a + 1 < r0, a + 1 >= r0 + NA_WIN_R)
            rowmask[i].append(jnp.where(lo, jnp.where(out_a, NEG, 0.0), jnp.where(out_b, NEG, 0.0)))
    for hp in range(NA_HEADS // 2):
        cols = slice(hp * LANES, (hp + 1) * LANES)
        bias_rows = []
        for head in range(2):
            for i in range(NA_QROWS):
                tiles = [t_ref[2 * hp + head, entry[i][p]] + rowmask[i][p] for p in range(NA_KROWS // 2)]
                bias_rows.append(jnp.concatenate(tiles, axis=1))
        bias = jnp.concatenate(bias_rows, axis=0)
        segs = [(k_ref[pl.ds(start, NA_NKEY), cols], v_ref[pl.ds(start, NA_NKEY), cols]),
                (kc_ref[:, cols], vc_ref[:, cols])]
        o_ref[:, cols] = _pair_attend(q_ref[:, cols], segs, [bias, None]).astype(o_ref.dtype)


def _na_attention(q, k, v, table, n_batch):
    ctx0 = n_batch * LAT_BLKS
    n_steps = NA_ROWS // NA_QROWS
    tq = NA_QROWS * GRID_W
    lat = pl.BlockSpec((N_LAT, D), lambda b, j: (b, 0))
    ctx = pl.BlockSpec((N_CTX, D), lambda b, j: (ctx0 + b, 0))
    return pl.pallas_call(
        _na_kernel,
        grid=(n_batch, n_steps),
        in_specs=[
            pl.BlockSpec((tq, D), lambda b, j: (b * n_steps + j, 0)),
            lat, lat, ctx, ctx,
            pl.BlockSpec(table.shape, lambda b, j: (0, 0, 0, 0), pipeline_mode=pl.Buffered(1)),
        ],
        out_specs=pl.BlockSpec((tq, D), lambda b, j: (b * n_steps + j, 0)),
        out_shape=jax.ShapeDtypeStruct((n_batch * N_LAT, D), BF16),
        compiler_params=_cparams(2),
        name="na_attention",
    )(q, k, v, k, v, table)


def _ctx_attn_kernel(q_ref, k_ref, v_ref, o_ref):
    for hp in range(NA_HEADS // 2):
        cols = slice(hp * LANES, (hp + 1) * LANES)
        segs = [(k_ref[:, cols], v_ref[:, cols])]
        o_ref[:, cols] = _pair_attend(q_ref[:, cols], segs, [None]).astype(o_ref.dtype)


def _ctx_attention(q, k, v, n_batch):
    ctx0 = n_batch * LAT_BLKS
    spec = pl.BlockSpec((N_CTX, D), lambda b: (ctx0 + b, 0))
    return pl.pallas_call(
        _ctx_attn_kernel,
        grid=(n_batch,),
        in_specs=[spec, spec, spec],
        out_specs=pl.BlockSpec((N_CTX, D), lambda b: (b, 0)),
        out_shape=jax.ShapeDtypeStruct((n_batch * N_CTX, D), BF16),
        compiler_params=_cparams(1),
        name="na_ctx_attention",
    )(q, k, v)


def _na_mixer(h, mod3, g, w_kvq, rpb, n_batch):
    segs = [(0, D, 1.0, BF16), (D, 2 * D, 1.0, BF16), (2 * D, 3 * D, NA_HD ** -0.5 * LOG2E, BF16)]
    k, v, q = _proj(h, mod3, g, w_kvq.astype(BF16), segs, n_batch)
    return _na_attention(q, k, v, _na_bias_tables(rpb), n_batch), _ctx_attention(q, k, v, n_batch)


DA_TQ = 1024
DA_W = 2 * DA_HD


def _rope_tables():
    t = np.arange(N_LAT)
    row = (t // GRID_W).astype(np.float64)
    col = (t % GRID_W).astype(np.float64)
    per_axis = DA_HD // 2
    freqs = ROPE_BASE ** (-np.arange(0, per_axis, 2, dtype=np.float64) / per_axis)
    ar = row[:, None] * freqs
    ac = col[:, None] * freqs
    ang = np.concatenate([ar, ar, ac, ac], axis=-1)
    cos = np.tile(np.cos(ang), (1, 2))
    sin = np.tile(np.sin(ang), (1, 2))
    quarter = (np.arange(DA_W) % DA_HD) // (DA_HD // 4)
    even = (quarter % 2 == 0)[None, :]
    sin_a = np.where(even, -sin, 0.0)
    sin_b = np.where(even, 0.0, sin)
    return tuple(jnp.asarray(a, F32) for a in (cos, sin_a, sin_b))


def _rope(x, cos, sin_a, sin_b):
    q16 = DA_HD // 4
    return x * cos + pltpu.roll(x, DA_W - q16, 1) * sin_a + pltpu.roll(x, q16, 1) * sin_b


def _da_lambda(lp, lam_init):
    a = jnp.sum(lp[0:1] * lp[1:2], axis=-1, keepdims=True)
    b = jnp.sum(lp[2:3] * lp[3:4], axis=-1, keepdims=True)
    return jnp.exp(a) - jnp.exp(b) + lam_init


DA_KCHUNK = 512


def _diff_attend(q, k, vx, lam, ng, lam_init):
    lo = _lane_lo(q.shape)
    qm = [_half_masked(q, lo, 0), _half_masked(q, lo, 1)]
    n_keys = k.shape[0]
    chunks = [(c0, min(c0 + DA_KCHUNK, n_keys)) for c0 in range(0, n_keys, DA_KCHUNK)]
    s = [[_dot_nt(qm[j], k[c0:c1, :]) for j in range(2)] for c0, c1 in chunks]
    m = [None, None]
    for sc in s:
        for j in range(2):
            mc = sc[j].max(axis=-1, keepdims=True)
            m[j] = mc if m[j] is None else jnp.maximum(m[j], mc)
    r = [None, None]
    for sc, (c0, c1) in zip(s, chunks):
        for j in range(2):
            pv = _dot(jnp.exp2(sc[j] - m[j]).astype(BF16), vx[c0:c1, :])
            r[j] = pv if r[j] is None else r[j] + pv
    o = r[0][:, :DA_W] / r[0][:, DA_W:] - lam * (r[1][:, :DA_W] / r[1][:, DA_W:])
    return o * lax.rsqrt(jnp.mean(o * o, axis=-1, keepdims=True) + EPS) * ng * (1.0 - lam_init)


def _da_kernel(lam_ref, q_ref, k_ref, v_ref, kc_ref, vc_ref, cq_ref, saq_ref, sbq_ref,
               ck_ref, sak_ref, sbk_ref, ng_ref, o_ref, kr_ref, vx_ref, *, lam_init):
    @pl.when(pl.program_id(2) == 0)
    def _():
        kr_ref[:N_LAT, :] = _rope(k_ref[...], ck_ref[...], sak_ref[...], sbk_ref[...]).astype(BF16)
        kr_ref[N_LAT:, :] = kc_ref[...].astype(BF16)
        vx_ref[:N_LAT, :DA_W] = v_ref[...]
        vx_ref[N_LAT:, :DA_W] = vc_ref[...]
        vx_ref[:, DA_W:] = jnp.ones((N_LAT + N_CTX, DA_W), BF16)

    lam = _da_lambda(lam_ref[...], lam_init)
    q = (_rope(q_ref[...], cq_ref[...], saq_ref[...], sbq_ref[...]) * LOG2E).astype(BF16)
    o_ref[...] = _diff_attend(q, kr_ref, vx_ref, lam, ng_ref[...], lam_init).astype(o_ref.dtype)


def _da_attention(q, k, v, lam_p, norm_g, lam_init, n_batch):
    cos, sin_a, sin_b = _rope_tables()
    nqb = N_LAT // DA_TQ
    ctx0 = n_batch * LAT_BLKS
    tab_q = pl.BlockSpec((DA_TQ, DA_W), lambda b, h, i: (i, 0))
    tab_k = pl.BlockSpec((N_LAT, DA_W), lambda b, h, i: (0, 0))
    lat = pl.BlockSpec((N_LAT, DA_W), lambda b, h, i: (b, h))
    ctx = pl.BlockSpec((N_CTX, DA_W), lambda b, h, i: (ctx0 + b, h))
    return pl.pallas_call(
        functools.partial(_da_kernel, lam_init=lam_init),
        grid=(n_batch, DA_HEADS, nqb),
        in_specs=[
            pl.BlockSpec((4, DA_HD), lambda b, h, i: (0, 0)),
            pl.BlockSpec((DA_TQ, DA_W), lambda b, h, i: (b * nqb + i, h)),
            lat, lat, ctx, ctx,
            tab_q, tab_q, tab_q, tab_k, tab_k, tab_k,
            pl.BlockSpec((1, DA_W), lambda b, h, i: (0, h)),
        ],
        out_specs=pl.BlockSpec((DA_TQ, DA_W), lambda b, h, i: (b * nqb + i, h)),
        out_shape=jax.ShapeDtypeStruct((n_batch * N_LAT, D), BF16),
        scratch_shapes=[pltpu.VMEM((N_LAT + N_CTX, DA_W), BF16), pltpu.VMEM((N_LAT + N_CTX, 2 * DA_W), BF16)],
        compiler_params=_cparams(3),
        name="diff_attention",
    )(lam_p, q, k, v, k, v, cos, sin_a, sin_b, cos, sin_a, sin_b, norm_g.reshape(1, D))


def _da_ctx_kernel(lam_ref, q_ref, k_ref, v_ref, ng_ref, o_ref, *, lam_init):
    lam = _da_lambda(lam_ref[...], lam_init)
    for h in range(DA_HEADS):
        cols = slice(h * DA_W, (h + 1) * DA_W)
        v = v_ref[:, cols]
        vx = jnp.concatenate([v, jnp.ones_like(v)], axis=1)
        q = (q_ref[:, cols] * LOG2E).astype(BF16)
        o = _diff_attend(q, k_ref[:, cols].astype(BF16), vx, lam, ng_ref[:, cols], lam_init)
        o_ref[:, cols] = o.astype(o_ref.dtype)


def _da_ctx_attention(q, k, v, lam_p, norm_g, lam_init, n_batch):
    ctx0 = n_batch * LAT_BLKS
    spec = pl.BlockSpec((N_CTX, D), lambda b: (ctx0 + b, 0))
    return pl.pallas_call(
        functools.partial(_da_ctx_kernel, lam_init=lam_init),
        grid=(n_batch,),
        in_specs=[pl.BlockSpec((4, DA_HD), lambda b: (0, 0)), spec, spec, spec, pl.BlockSpec((1, D), lambda b: (0, 0))],
        out_specs=pl.BlockSpec((N_CTX, D), lambda b: (b, 0)),
        out_shape=jax.ShapeDtypeStruct((n_batch * N_CTX, D), BF16),
        compiler_params=_cparams(1),
        name="diff_ctx_attention",
    )(lam_p, q, k, v, norm_g.reshape(1, D))


def _da_mixer(h, mod3, g, w_kvq, lam_p, norm_g, layer_idx, n_batch):
    lam_init = 0.8 - 0.6 * float(np.exp(-0.3 * layer_idx))
    segs = [(0, D, 1.0, F32), (D, 2 * D, 1.0, BF16), (2 * D, 3 * D, DA_HD ** -0.5, F32)]
    k, v, q = _proj(h, mod3, g, w_kvq.astype(BF16), segs, n_batch)
    lam_p = lam_p.astype(F32)
    return (_da_attention(q, k, v, lam_p, norm_g, lam_init, n_batch),
            _da_ctx_attention(q, k, v, lam_p, norm_g, lam_init, n_batch))


SCAN_STEPS = 1 + LAT_BLKS


def _scan_row_blk(direction, n_batch):
    ctx0 = n_batch * LAT_BLKS

    def blk(b, s):
        lat = b * LAT_BLKS + (s - 1 if direction == 0 else LAT_BLKS - s)
        return jnp.where(s == 0, ctx0 + b, lat)
    return blk


def _scan_spec(width, direction, n_batch):
    blk = _scan_row_blk(direction, n_batch)
    return pl.BlockSpec((ROW_BLK, width), lambda b, h, s: (blk(b, s), h))


def _scan_spec_all(width, direction, n_batch):
    blk = _scan_row_blk(direction, n_batch)
    return pl.BlockSpec((ROW_BLK, width), lambda b, h, s: (blk(b, s), 0))


def _scan_tri():
    u = np.arange(ROW_BLK)
    tt, uu = np.meshgrid(u, u, indexing="ij")
    return jnp.asarray(np.stack([uu <= tt, uu >= tt]).astype(np.float32), BF16)


def _split3(x):
    hi = x.astype(BF16)
    r1 = x - hi.astype(F32)
    mid = r1.astype(BF16)
    lo = (r1 - mid.astype(F32)).astype(BF16)
    return hi, mid, lo


def _scan_out_specs(width):
    return [pl.BlockSpec((N_LAT, width), lambda b, h, s: (b, h)), pl.BlockSpec((N_CTX, width), lambda b, h, s: (b, h))]


def _scan_write_sum(step, out_fwd, out_bwd, lat_ref, ctx_ref):
    @pl.when(step == 0)
    def _():
        ctx_ref[...] = out_fwd + out_bwd

    rows_f = pl.ds(pl.multiple_of((step - 1) * ROW_BLK, ROW_BLK), ROW_BLK)
    rows_b = pl.ds(pl.multiple_of((LAT_BLKS - step) * ROW_BLK, ROW_BLK), ROW_BLK)

    @pl.when(jnp.logical_and(step >= 1, step <= LAT_BLKS // 2))
    def _():
        lat_ref[rows_f, :] = out_fwd
        lat_ref[rows_b, :] = out_bwd

    @pl.when(step > LAT_BLKS // 2)
    def _():
        lat_ref[rows_f, :] += out_fwd
        lat_ref[rows_b, :] += out_bwd


def _ml_conv_kernel(xk_ref, xkp_ref, xkn_ref, xq_ref, xqp_ref, xqn_ref, w_ref, ok_ref, oq_ref, *, n_lat_blks):
    i = pl.program_id(0)
    is_ctx = i >= n_lat_blks
    first = jnp.logical_or(is_ctx, i % LAT_BLKS == 0)
    last = jnp.logical_or(is_ctx, i % LAT_BLKS == LAT_BLKS - 1)
    rid = lax.broadcasted_iota(jnp.int32, (ROW_BLK, D), 0)
    groups = ((xk_ref, xkp_ref, xkn_ref, 0, ok_ref, ML_HD ** -0.5), (xq_ref, xqp_ref, xqn_ref, D, oq_ref, 1.0))
    for x_ref, xp_ref, xn_ref, c0, o_ref, scale in groups:
        x = x_ref[...]
        prev_row = jnp.where(first, 0.0, xp_ref[7:8, :])
        next_row = jnp.where(last, 0.0, xn_ref[0:1, :])
        x_prev = jnp.where(rid == 0, prev_row, pltpu.roll(x, 1, 0))
        x_next = jnp.where(rid == ROW_BLK - 1, next_row, pltpu.roll(x, ROW_BLK - 1, 0))
        y = w_ref[0:1, c0:c0 + D] * x_prev + w_ref[1:2, c0:c0 + D] * x + w_ref[2:3, c0:c0 + D] * x_next
        o_ref[...] = (_silu(y) * scale).astype(o_ref.dtype)


def _ml_conv(pk, pq, conv_w, n_batch):
    n_rows = pk.shape[0]
    n_blk = n_rows // ROW_BLK
    sub = ROW_BLK // 8
    main = pl.BlockSpec((ROW_BLK, D), lambda i: (i, 0))
    prev = pl.BlockSpec((8, D), lambda i: (jnp.maximum(i * sub - 1, 0), 0))
    nxt = pl.BlockSpec((8, D), lambda i: (jnp.minimum((i + 1) * sub, n_blk * sub - 1), 0))
    return pl.pallas_call(
        functools.partial(_ml_conv_kernel, n_lat_blks=n_batch * LAT_BLKS),
        grid=(n_blk,),
        in_specs=[main, prev, nxt, main, prev, nxt, pl.BlockSpec((3, 2 * D), lambda i: (0, 0))],
        out_specs=[main, main],
        out_shape=[jax.ShapeDtypeStruct((n_rows, D), BF16)] * 2,
        compiler_params=_cparams(1),
        name="mlstm_conv",
    )(pk, pk, pk, pq, pq, pq, conv_w)


def _pick_lane(x, idx):
    lane = lax.broadcasted_iota(jnp.int32, x.shape, 1)
    return jnp.sum(jnp.where(lane == idx, x, 0.0), axis=1, keepdims=True)


def _pick_sublane(x, idx):
    sub = lax.broadcasted_iota(jnp.int32, x.shape, 0)
    return jnp.sum(jnp.where(sub == idx, x, 0.0), axis=0, keepdims=True)


def _ml_scan_kernel(gb_ref, g0_ref, q0_ref, k0_ref, v0_ref, g1_ref, q1_ref, k1_ref, v1_ref,
                    lat_ref, ctx_ref, c_ref, n_ref, m_ref):
    head = pl.program_id(1)

    @pl.when(pl.program_id(2) == 0)
    def _():
        c_ref[...] = jnp.zeros_like(c_ref)
        n_ref[...] = jnp.zeros_like(n_ref)
        m_ref[...] = jnp.zeros_like(m_ref)

    r = lax.broadcasted_iota(jnp.int32, (ROW_BLK, ROW_BLK), 0)
    c = lax.broadcasted_iota(jnp.int32, (ROW_BLK, ROW_BLK), 1)
    outs = [None, None]

    def direction(d, g_ref, q_ref, k_ref, v_ref):
        valid = (r >= c) if d == 0 else (r <= c)
        other = (r <= c) if d == 0 else (r >= c)
        g = g_ref[...] + gb_ref[...]
        gt = g.T
        ii = d * 2 * ML_HEADS + head
        fi = ii + ML_HEADS
        m_prev = m_ref[d][0:1, 0:1]
        n_prev = n_ref[d][0:1, :]
        c_prev = c_ref[d]
        qc = q_ref[...]
        kc = k_ref[...]
        vc = v_ref[...]
        qk = _dot_nt(qc, kc)
        q_c = _dot(qc, c_prev.astype(BF16))
        q_n = jnp.sum(qc.astype(F32) * n_prev, axis=1, keepdims=True)
        yield
        i_col = _pick_lane(g, ii)
        f_col = _log_sigmoid(_pick_lane(g, fi))
        i_row = _pick_sublane(gt, ii)
        f_row = _log_sigmoid(_pick_sublane(gt, fi))
        yield
        b_col = jnp.sum(jnp.where(valid, f_row, 0.0), axis=1, keepdims=True)
        b_row = jnp.sum(jnp.where(other, f_col, 0.0), axis=0, keepdims=True)
        total = jnp.sum(f_col, axis=0, keepdims=True)
        yield
        dmat = jnp.where(valid, b_col - b_row + i_row, NEG)
        inter = b_col + m_prev
        m_t = jnp.maximum(inter, dmat.max(axis=1, keepdims=True))
        w_inter = jnp.exp(inter - m_t)
        g_col = total - b_col + i_col
        m_new = jnp.maximum(total + m_prev, g_col.max(axis=0, keepdims=True))
        w_old = jnp.exp(total + m_prev - m_new)
        kw = kc.astype(F32) * jnp.exp(g_col - m_new)
        upd = _dot_tn(kw.astype(BF16), vc)
        yield
        s = qk * jnp.exp(dmat - m_t)
        num = w_inter * q_c + _dot(s.astype(BF16), vc)
        den = w_inter * q_n + s.sum(axis=1, keepdims=True)
        outs[d] = num / jnp.maximum(jnp.abs(den), jnp.exp(-m_t))
        yield
        c_ref[d] = w_old * c_prev + upd
        n_ref[d] = jnp.broadcast_to(w_old * n_prev + kw.sum(axis=0, keepdims=True), n_ref.shape[1:])
        m_ref[d] = jnp.broadcast_to(m_new, m_ref.shape[1:])

    stages = [direction(0, g0_ref, q0_ref, k0_ref, v0_ref), direction(1, g1_ref, q1_ref, k1_ref, v1_ref)]
    while stages:
        stages = [gen for gen in stages if next(gen, StopIteration) is not StopIteration]
    _scan_write_sum(pl.program_id(2), outs[0], outs[1], lat_ref, ctx_ref)


def _ml_scan(q, k, v, gates, gate_b, n_batch):
    in_specs = [pl.BlockSpec((1, LANES), lambda b, h, s: (0, 0))]
    for d in range(2):
        in_specs += [_scan_spec_all(LANES, d, n_batch)] + [_scan_spec(ML_HD, d, n_batch)] * 3
    return pl.pallas_call(
        _ml_scan_kernel,
        grid=(n_batch, ML_HEADS, SCAN_STEPS),
        in_specs=in_specs,
        out_specs=_scan_out_specs(ML_HD),
        out_shape=[jax.ShapeDtypeStruct((n_batch * N_LAT, D), F32), jax.ShapeDtypeStruct((n_batch * N_CTX, D), F32)],
        scratch_shapes=[pltpu.VMEM((2, ML_HD, ML_HD), F32), pltpu.VMEM((2, 8, ML_HD), F32),
                        pltpu.VMEM((2, 8, LANES), F32)],
        compiler_params=_cparams(3),
        name="mlstm_scan",
    )(gate_b, gates, q, k, v, gates, q, k, v)


def _ml_mixer(h, mod3, g, w_in, gate_b, conv_w, n_batch):
    n_g = 4 * ML_HEADS
    w_in = w_in.astype(BF16)
    wk, wv, wg, wq, wo = (w_in[:, :D], w_in[:, D:2 * D], w_in[:, 2 * D:2 * D + n_g],
                          w_in[:, 2 * D + n_g:3 * D + n_g], w_in[:, 3 * D + n_g:])
    w = jnp.concatenate([wk, wv, wq, wo, jnp.pad(wg, ((0, 0), (0, LANES - n_g)))], axis=1)
    segs = [(0, D, 1.0, F32), (D, 2 * D, 1.0, BF16), (2 * D, 3 * D, 1.0, F32), (3 * D, 4 * D, 1.0, F32),
            (4 * D, 4 * D + LANES, 1.0, F32)]
    pk, v, pq, og, gates = _proj(h, mod3, g, w, segs, n_batch)
    k, q = _ml_conv(pk, pq, conv_w.astype(F32), n_batch)
    gb = jnp.pad(gate_b.astype(F32), (0, LANES - n_g)).reshape(1, LANES)
    h_lat, h_ctx = _ml_scan(q, k, v, gates, gb, n_batch)
    return h_lat, h_ctx, og


GLA_LEVELS = (256, 128, 64, 32, 16, 8, 4, 2)
GLA_BCAST_LEVELS = 6
GLA_DIAG = len(GLA_LEVELS)


def _gla_level_ids():
    u = np.arange(ROW_BLK)
    tt, ss = np.meshgrid(u, u, indexing="ij")
    hb = np.floor(np.log2(np.maximum(tt ^ ss, 1))).astype(np.int64)
    level = len(GLA_LEVELS) - 1 - hb
    lid = np.stack([np.where(tt == ss, GLA_DIAG, np.where(tt > ss, level, -1)),
                    np.where(tt == ss, GLA_DIAG, np.where(tt < ss, level, -1))]).astype(np.int32)
    return jnp.asarray(lid)


def _gla_level_exponents(level, d, bc, bc_ref, la, la_prev, la_next, row):
    n = GLA_LEVELS[level]
    half = n // 2
    if level < GLA_BCAST_LEVELS:
        parts = []
        for j in range(ROW_BLK // n):
            mid = j * n + (half - 1 if d == 0 else half)
            parts.append(jnp.broadcast_to(bc_ref[mid:mid + 1, :], (n, GLA_DK)))
        x = bc - (parts[0] if len(parts) == 1 else jnp.concatenate(parts, axis=0))
        return jnp.minimum(x, 0.0), jnp.minimum(-x, 0.0)
    o = row & (n - 1)
    if n == 4:
        if d == 0:
            return (jnp.where(o == 2, la, jnp.where(o == 3, la + la_prev, 0.0)), jnp.where(o == 0, la_next, 0.0))
        return (jnp.where(o == 0, la + la_next, jnp.where(o == 1, la, 0.0)), jnp.where(o == 3, la_prev, 0.0))
    return jnp.where(o == (1 if d == 0 else 0), la, 0.0), None


def _gla_kernel(tri_ref, lid_ref, wup_ref, bg_ref, q0_ref, k0_ref, v0_ref, l0_ref, q1_ref, k1_ref, v1_ref, l1_ref,
                lat_ref, ctx_ref, st_ref, bc_ref, att_ref):
    @pl.when(pl.program_id(2) == 0)
    def _():
        st_ref[...] = jnp.zeros_like(st_ref)

    row = lax.broadcasted_iota(jnp.int32, (ROW_BLK, GLA_DK), 0)
    outs = [None, None]

    def direction(d, q_ref, k_ref, v_ref, low_ref):
        pre = _dot(low_ref[...].astype(BF16), wup_ref[d]) + bg_ref[d]
        la = _log_sigmoid(pre) * (LOG2E / GLA_TAU)
        hi, mid, lo = _split3(la)
        tri = tri_ref[d]
        bc = _dot(tri, hi) + _dot(tri, mid) + _dot(tri, lo)
        bc_ref[d] = bc
        la_prev = pltpu.roll(la, 1, 0)
        la_next = pltpu.roll(la, ROW_BLK - 1, 0)
        q = q_ref[...]
        k = k_ref[...]
        v = v_ref[...]
        lid = lid_ref[d]
        kb = k.astype(BF16)
        att_ref[d] = jnp.where(lid == GLA_DIAG, _dot_nt(q.astype(BF16), kb), 0.0)
        yield
        for level in range(len(GLA_LEVELS)):
            eq, ek = _gla_level_exponents(level, d, bc, bc_ref.at[d], la, la_prev, la_next, row)
            ql = (q * jnp.exp2(eq)).astype(BF16)
            kl = kb if ek is None else (k * jnp.exp2(ek)).astype(BF16)
            att_ref[d] = jnp.where(lid == level, _dot_nt(ql, kl), att_ref[d])
            yield
        st = st_ref[d]
        qd = (q * jnp.exp2(bc)).astype(BF16)
        outs[d] = _dot(att_ref[d].astype(BF16), v) + _dot_nt(qd, st.astype(BF16))
        b_end = bc[ROW_BLK - 1:ROW_BLK] if d == 0 else bc[0:1]
        kd = (k * jnp.exp2(b_end - bc)).astype(BF16)
        st_ref[d] = st * jnp.exp2(b_end) + _dot_tn(v, kd)

    stages = [direction(0, q0_ref, k0_ref, v0_ref, l0_ref), direction(1, q1_ref, k1_ref, v1_ref, l1_ref)]
    while stages:
        stages = [gen for gen in stages if next(gen, StopIteration) is not StopIteration]
    _scan_write_sum(pl.program_id(2), outs[0], outs[1], lat_ref, ctx_ref)


def _gla_scan(q, k, v, low, wup, bg, n_batch):
    tri, lid = _scan_tri(), _gla_level_ids()
    const3 = lambda b, h, s: (0, 0, 0)
    in_specs = [
        pl.BlockSpec(tri.shape, const3),
        pl.BlockSpec(lid.shape, const3),
        pl.BlockSpec((2, LANES, GLA_DK), lambda b, h, s: (0, 0, h)),
        pl.BlockSpec((2, 1, GLA_DK), lambda b, h, s: (0, 0, h)),
    ]
    for d in range(2):
        in_specs += [_scan_spec(GLA_DK, d, n_batch), _scan_spec(GLA_DK, d, n_batch),
                     _scan_spec(GLA_DV, d, n_batch), _scan_spec_all(LANES, d, n_batch)]
    return pl.pallas_call(
        _gla_kernel,
        grid=(n_batch, GLA_HEADS, SCAN_STEPS),
        in_specs=in_specs,
        out_specs=_scan_out_specs(GLA_DV),
        out_shape=[jax.ShapeDtypeStruct((n_batch * N_LAT, D), F32), jax.ShapeDtypeStruct((n_batch * N_CTX, D), F32)],
        scratch_shapes=[pltpu.VMEM((2, GLA_DV, GLA_DK), F32), pltpu.VMEM((2, ROW_BLK, GLA_DK), F32),
                        pltpu.VMEM((2, ROW_BLK, ROW_BLK), F32)],
        compiler_params=_cparams(3),
        name="gla_scan",
    )(tri, lid, wup, bg, q, k, v, low, q, k, v, low)


def _gla_mixer(h, mod3, g, w_in, w_gate_up, b_gate, n_batch):
    dk_t = GLA_HEADS * GLA_DK
    n_low = 2 * GLA_RANK
    w_in = w_in.astype(BF16)
    wk, wv, wl, wq, wr = (w_in[:, :dk_t], w_in[:, dk_t:dk_t + D], w_in[:, dk_t + D:dk_t + D + n_low],
                          w_in[:, dk_t + D + n_low:2 * dk_t + D + n_low], w_in[:, 2 * dk_t + D + n_low:])
    w = jnp.concatenate([wk, wq, wv, wr, jnp.pad(wl, ((0, 0), (0, LANES - n_low)))], axis=1)
    c = [0, dk_t, 2 * dk_t, 2 * dk_t + D, 2 * dk_t + 2 * D, 2 * dk_t + 2 * D + LANES]
    segs = [(c[0], c[1], 1.0, F32), (c[1], c[2], GLA_DK ** -0.5, F32), (c[2], c[3], 1.0, BF16),
            (c[3], c[4], 1.0, F32), (c[4], c[5], 1.0, F32)]
    k, q, v, rg, low = _proj(h, mod3, g, w, segs, n_batch)
    wup = jnp.zeros((2, LANES, dk_t), F32)
    for r in range(2):
        wup = wup.at[r, r * GLA_RANK:(r + 1) * GLA_RANK].set(w_gate_up[r].astype(F32))
    o_lat, o_ctx = _gla_scan(q, k, v, low, wup.astype(BF16), b_gate.astype(F32).reshape(2, 1, dk_t), n_batch)
    return o_lat, o_ctx, rg


def kernel(x, c, ctx, c_ctx, ada_w, ada_b, norm_g, ffn_w13, ffn_w2, final_g, na_w_kvq, na_rpb, na_w_o, ml_w_in, ml_gate_b, ml_conv_w, ml_norm_g, ml_w_o, da_w_kvq, da_lam, da_norm_g, da_w_o, gla_w_in, gla_w_gate_up, gla_b_gate, gla_norm_g, gla_w_o):
    nb = x.shape[0]
    assert x.shape[1:] == (N_LAT, D) and ctx.shape[1:] == (N_CTX, D) and nb < MOD_ROWS
    n_lat = nb * N_LAT
    n_tok = n_lat + nb * N_CTX

    s = jnp.zeros((MOD_ROWS, D), F32).at[:nb].set(c).at[nb].set(c_ctx)
    modtab = _mod_tables(s, ada_w, ada_b)
    h = (x.reshape(n_lat, D), ctx.reshape(nb * N_CTX, D))

    for i in range(DEPTH):
        kind, j = i % 4, i // 4
        last = i == DEPTH - 1
        mod3 = modtab[i].reshape(MOD_ROWS * N_MOD, 1, D)
        h = _ffn(h, mod3, norm_g[i, 0], ffn_w13[i, 0].astype(BF16), ffn_w2[i, 0].astype(BF16), final_g, 0, n_tok, nb)
        n_out = n_lat if last else n_tok
        if kind == 0:
            ins = _na_mixer(h, mod3, norm_g[i, 1], na_w_kvq[j], na_rpb[j], nb)
            mixer, w_o, mixer_g = "plain", na_w_o[j], None
        elif kind == 1:
            ins = _ml_mixer(h, mod3, norm_g[i, 1], ml_w_in[j], ml_gate_b[j], ml_conv_w[j], nb)
            mixer, w_o, mixer_g = "ml", ml_w_o[j], ml_norm_g[j]
        elif kind == 2:
            ins = _da_mixer(h, mod3, norm_g[i, 1], da_w_kvq[j], da_lam[j], da_norm_g[j], i, nb)
            mixer, w_o, mixer_g = "plain", da_w_o[j], None
        else:
            ins = _gla_mixer(h, mod3, norm_g[i, 1], gla_w_in[j], gla_w_gate_up[j], gla_b_gate[j], nb)
            mixer, w_o, mixer_g = "gla", gla_w_o[j], gla_norm_g[j]
        h = _ffn(h, mod3, norm_g[i, 2], ffn_w13[i, 1].astype(BF16), ffn_w2[i, 1].astype(BF16), final_g, 6, n_out, nb,
                 final=last, mixer=mixer, mixer_ins=ins, w_o=w_o.astype(BF16), mixer_g=mixer_g)
    return h.reshape(nb, N_LAT, D)
```

```python
import functools

import numpy as np
import jax
import jax.numpy as jnp
from jax import lax
from jax.experimental import pallas as pl
from jax.experimental.pallas import tpu as pltpu

F32 = jnp.float32
BF16 = jnp.bfloat16

D = 1024
N_LAT = 2048
N_CTX = 256
DEPTH = 4
N_MOD = 9
D_FF = 2816
EPS = 1e-6
NEG = -1e30
LOG2E = 1.4426950408889634
GRID_W = 64

NA_HEADS = 16
NA_HD = 64
NA_WIN_R = 8
NA_WIN_C = 16

ML_HEADS = 4
ML_HD = 256

DA_HEADS = 8
DA_HD = 64
ROPE_BASE = 10000.0

GLA_HEADS = 4
GLA_DK = 128
GLA_DV = 256
GLA_RANK = 16
GLA_TAU = 16.0

ROW_BLK = 256
LAT_BLKS = N_LAT // ROW_BLK
MOD_ROWS = 16
LANES = 128
VMEM_LIMIT = 56 * 1024 * 1024


def _cparams(n_axes, vmem=VMEM_LIMIT):
    return pltpu.CompilerParams(dimension_semantics=("arbitrary",) * n_axes, vmem_limit_bytes=vmem)


def _sigmoid(x):
    return 1.0 / (1.0 + jnp.exp(-x))


def _silu(x):
    return x * _sigmoid(x)


def _log_sigmoid(x):
    return jnp.minimum(x, 0.0) - jnp.log(1.0 + jnp.exp(-jnp.abs(x)))


def _rms(x, g):
    return x * lax.rsqrt(jnp.mean(x * x, axis=-1, keepdims=True) + EPS) * g


def _dot(a, b):
    return jnp.dot(a, b, preferred_element_type=F32)


def _dot_nt(a, b):
    return lax.dot_general(a, b, (((1,), (1,)), ((), ())), preferred_element_type=F32)


def _dot_tn(a, b):
    return lax.dot_general(a, b, (((0,), (0,)), ((), ())), preferred_element_type=F32)


def _mod_row(i, tm, n_batch):
    return jnp.minimum(i // (N_LAT // tm), n_batch)


def _mod_spec(k, tm, n_batch):
    return pl.BlockSpec((1, 1, D), lambda i: (_mod_row(i, tm, n_batch) * N_MOD + k, 0, 0))


def _mod_kernel(s_ref, w_ref, b_ref, o_ref):
    a = _silu(s_ref[...]).astype(BF16)
    o_ref[0] = _dot(a, w_ref[0].astype(BF16)) + b_ref[0]


def _mod_tables(s, ada_w, ada_b):
    return pl.pallas_call(
        _mod_kernel,
        grid=(DEPTH, N_MOD),
        in_specs=[
            pl.BlockSpec((MOD_ROWS, D), lambda i, k: (0, 0)),
            pl.BlockSpec((1, D, D), lambda i, k: (i, 0, k)),
            pl.BlockSpec((1, 1, D), lambda i, k: (i, 0, k)),
        ],
        out_specs=pl.BlockSpec((1, MOD_ROWS, D), lambda i, k: (i, 0, k)),
        out_shape=jax.ShapeDtypeStruct((DEPTH, MOD_ROWS, N_MOD * D), F32),
        compiler_params=_cparams(2),
        name="mod_tables",
    )(s, ada_w, ada_b.reshape(DEPTH, 1, N_MOD * D))


FFN_TM = 512
MXU_TILE = 256
FFN_SPLIT = (D_FF // MXU_TILE + 1) // 2 * MXU_TILE
FFN_CHUNKS = ((0, FFN_SPLIT), (FFN_SPLIT, D_FF))


def _head_rms(x, g, n_heads):
    hd = D // n_heads
    parts = []
    for i in range(n_heads):
        xi = x[:, i * hd:(i + 1) * hd]
        parts.append(xi * lax.rsqrt(jnp.mean(xi * xi, axis=-1, keepdims=True) + EPS))
    return jnp.concatenate(parts, axis=-1) * g


def _mixer_readout(mode, refs, is_lat):
    y = jnp.where(is_lat, refs[0][...], refs[1][...])
    if mode == "ml":
        return _sigmoid(refs[2][...]) * _head_rms(y, refs[3][...], ML_HEADS)
    if mode == "gla":
        return _head_rms(y, refs[3][...], GLA_HEADS) * _silu(refs[2][...])
    return y


def _ffn_kernel(g_ref, sh_ref, sc_ref, gt_ref, w13_ref, w2_ref, fg_ref, *refs, final, mixer, split_h, n_lat_tiles):
    o_ref = refs[-1]
    if split_h:
        x = jnp.where(pl.program_id(0) < n_lat_tiles, refs[0][...], refs[1][...])
        refs = refs[2:]
    else:
        x = refs[0][...]
        refs = refs[1:]
    if mixer is not None:
        mg_ref, wo_ref = refs[:2]
        y = _mixer_readout(mixer, refs[2:-1], pl.program_id(0) < n_lat_tiles)
        x = x + mg_ref[0] * _dot(y.astype(BF16), wo_ref[...])
    a = (_rms(x, g_ref[...]) * (1.0 + sc_ref[0]) + sh_ref[0]).astype(BF16)
    acc = jnp.zeros(x.shape, F32)
    for c0, c1 in FFN_CHUNKS:
        gate = _dot(a, w13_ref[:, c0:c1])
        up = _dot(a, w13_ref[:, D_FF + c0:D_FF + c1])
        act = (_silu(gate) * up).astype(BF16)
        acc = acc + _dot(act, w2_ref[c0:c1, :])
    y = x + 0.5 * gt_ref[0] * acc
    if final:
        y = _rms(y, fg_ref[...])
    o_ref[...] = y


def _ffn(h, mod3, g, w13, w2, widx, final_g, k0, n_rows, n_batch, final=False, mixer=None, mixer_ins=(), w_o=None,
         mixer_g=None):
    tm = FFN_TM
    nl = n_batch * N_LAT // tm
    const = lambda i: (0, 0)
    row = pl.BlockSpec((tm, D), lambda i: (i, 0))
    lat_ctx = [pl.BlockSpec((tm, D), lambda i: (jnp.minimum(i, nl - 1), 0)),
               pl.BlockSpec((tm, D), lambda i: (jnp.maximum(i - nl, 0), 0))]
    split_h = isinstance(h, tuple)
    in_specs = [
        pl.BlockSpec((1, D), const),
        _mod_spec(k0, tm, n_batch),
        _mod_spec(k0 + 1, tm, n_batch),
        _mod_spec(k0 + 2, tm, n_batch),
        pl.BlockSpec((None, None, D, 2 * D_FF), lambda i: (*widx, 0, 0), pipeline_mode=pl.Buffered(1)),
        pl.BlockSpec((None, None, D_FF, D), lambda i: (*widx, 0, 0), pipeline_mode=pl.Buffered(1)),
        pl.BlockSpec((1, D), const),
    ] + (lat_ctx if split_h else [row])
    args = [g.reshape(1, D), mod3, mod3, mod3, w13, w2, final_g.reshape(1, D)] + (list(h) if split_h else [h])
    if mixer is not None:
        in_specs += [_mod_spec(5, tm, n_batch), pl.BlockSpec((D, D), const, pipeline_mode=pl.Buffered(1))] + lat_ctx
        args += [mod3, w_o, mixer_ins[0], mixer_ins[1]]
        if mixer != "plain":
            in_specs += [row, pl.BlockSpec((1, D), const)]
            args += [mixer_ins[2], mixer_g.reshape(1, D)]
    return pl.pallas_call(
        functools.partial(_ffn_kernel, final=final, mixer=mixer, split_h=split_h, n_lat_tiles=nl),
        grid=(n_rows // tm,),
        in_specs=in_specs,
        out_specs=row,
        out_shape=jax.ShapeDtypeStruct((n_rows, D), F32),
        compiler_params=_cparams(1),
        name="ffn" if mixer is None else "mixer_out_ffn",
    )(*args)


PROJ_TM = 512


def _proj_kernel(h_ref, g_ref, sh_ref, sc_ref, w_ref, *o_refs, segs):
    x = h_ref[...]
    a = (_rms(x, g_ref[...]) * (1.0 + sc_ref[0]) + sh_ref[0]).astype(BF16)
    for o_ref, (c0, c1, scale) in zip(o_refs, segs):
        y = _dot(a, w_ref[:, c0:c1])
        if scale != 1.0:
            y = y * scale
        o_ref[...] = y.astype(o_ref.dtype)


def _proj(h, mod3, g, w, segs, n_batch):
    tm = PROJ_TM
    n_rows = h.shape[0]
    const = lambda i: (0, 0)
    return pl.pallas_call(
        functools.partial(_proj_kernel, segs=tuple((c0, c1, sc) for c0, c1, sc, _ in segs)),
        grid=(n_rows // tm,),
        in_specs=[
            pl.BlockSpec((tm, D), lambda i: (i, 0)),
            pl.BlockSpec((1, D), const),
            _mod_spec(3, tm, n_batch),
            _mod_spec(4, tm, n_batch),
            pl.BlockSpec(w.shape, const, pipeline_mode=pl.Buffered(1)),
        ],
        out_specs=[pl.BlockSpec((tm, c1 - c0), lambda i: (i, 0)) for c0, c1, _, _ in segs],
        out_shape=[jax.ShapeDtypeStruct((n_rows, c1 - c0), dt) for c0, c1, _, dt in segs],
        compiler_params=_cparams(1),
        name="mixer_proj",
    )(h, g.reshape(1, D), mod3, mod3, w)


def _lane_lo(shape):
    return lax.broadcasted_iota(jnp.int32, shape, 1) < (LANES // 2)


def _half_masked(x, lo, head):
    return jnp.where(lo if head == 0 else jnp.logical_not(lo), x, jnp.zeros_like(x))


def _pair_attend(qp, segs, biases):
    tq = qp.shape[0]
    lo = _lane_lo(qp.shape)
    q2 = jnp.concatenate([_half_masked(qp, lo, 0), _half_masked(qp, lo, 1)], axis=0)
    scores = []
    for (k, _), b in zip(segs, biases):
        s = _dot_nt(q2, k)
        scores.append(s if b is None else s + b)
    m = scores[0].max(axis=-1, keepdims=True)
    for s in scores[1:]:
        m = jnp.maximum(m, s.max(axis=-1, keepdims=True))
    acc = None
    for s, (_, v) in zip(scores, segs):
        vx = jnp.concatenate([v, jnp.ones_like(v)], axis=1)
        pv = _dot(jnp.exp2(s - m).astype(BF16), vx)
        acc = pv if acc is None else acc + pv
    o2 = acc[:, :LANES] / acc[:, LANES:]
    return jnp.where(lo, o2[:tq], o2[tq:])


NA_ROWS = N_LAT // GRID_W
NA_QROWS = 4
NA_KROWS = 12
NA_NKEY = NA_KROWS * GRID_W
NA_NOFF = 2 * NA_WIN_R - 1
NA_NENT = NA_NOFF + 1


NA_NCOFF = 2 * NA_WIN_C - 1


def _na_table_kernel(rpb_ref, o_ref):
    h = pl.program_id(0)
    c = lax.broadcasted_iota(jnp.int32, (GRID_W, LANES), 0)
    lane = lax.broadcasted_iota(jnp.int32, (GRID_W, LANES), 1)
    kc = lane & (GRID_W - 1)
    second = lane >= GRID_W
    col_off = jnp.clip(kc - c, 1 - NA_WIN_C, NA_WIN_C - 1) + NA_WIN_C - 1
    win0 = jnp.clip(c - NA_WIN_C // 2, 0, GRID_W - NA_WIN_C)
    in_win = jnp.logical_and(kc >= win0, kc < win0 + NA_WIN_C)
    for e in range(NA_NENT):
        acc = jnp.zeros((GRID_W, LANES), F32)
        for o in range(NA_NCOFF):
            first_v = rpb_ref[(h * NA_NOFF + e - 1) * NA_NCOFF + o] if e >= 1 else 0.0
            second_v = rpb_ref[(h * NA_NOFF + e) * NA_NCOFF + o] if e < NA_NOFF else 0.0
            acc = jnp.where(col_off == o, jnp.where(second, second_v, first_v), acc)
        o_ref[0, e] = jnp.where(in_win, acc * LOG2E, NEG)


def _na_bias_tables(rpb):
    return pl.pallas_call(
        _na_table_kernel,
        grid=(NA_HEADS,),
        in_specs=[pl.BlockSpec(memory_space=pltpu.SMEM)],
        out_specs=pl.BlockSpec((1, NA_NENT, GRID_W, LANES), lambda h: (h, 0, 0, 0)),
        out_shape=jax.ShapeDtypeStruct((NA_HEADS, NA_NENT, GRID_W, LANES), F32),
        compiler_params=_cparams(1),
        name="na_bias_table",
    )(rpb.astype(F32).reshape(-1))


def _na_key_row0(j):
    return jnp.clip(NA_QROWS * j - NA_WIN_R // 2, 0, NA_ROWS - NA_KROWS)


def _na_kernel(q_ref, k_ref, v_ref, kc_ref, vc_ref, t_ref, o_ref):
    j = pl.program_id(1)
    u0 = _na_key_row0(j)
    start = pl.multiple_of(u0 * GRID_W, GRID_W)
    lo = _lane_lo((1, LANES))
    entry = []
    rowmask = []
    for i in range(NA_QROWS):
        r = NA_QROWS * j + i
        r0 = jnp.clip(r - NA_WIN_R // 2, 0, NA_ROWS - NA_WIN_R)
        entry.append([])
        rowmask.append([])
        for p in range(NA_KROWS // 2):
            a = u0 + 2 * p
            entry[i].append(jnp.clip(a - r + NA_WIN_R, 0, NA_NENT - 1))
            out_a = jnp.logical_or(a < r0, a >= r0 + NA_WIN_R)
            out_b = jnp.logical_or(a + 1 < r0, a + 1 >= r0 + NA_WIN_R)
            rowmask[i].append(jnp.where(lo, jnp.where(out_a, NEG, 0.0), jnp.where(out_b, NEG, 0.0)))
    for hp in range(NA_HEADS // 2):
        cols = slice(hp * LANES, (hp + 1) * LANES)
        bias_rows = []
        for head in range(2):
            for i in range(NA_QROWS):
                tiles = [t_ref[2 * hp + head, entry[i][p]] + rowmask[i][p] for p in range(NA_KROWS // 2)]
                bias_rows.append(jnp.concatenate(tiles, axis=1))
        bias = jnp.concatenate(bias_rows, axis=0)
        segs = [(k_ref[pl.ds(start, NA_NKEY), cols], v_ref[pl.ds(start, NA_NKEY), cols]),
                (kc_ref[:, cols], vc_ref[:, cols])]
        o_ref[:, cols] = _pair_attend(q_ref[:, cols], segs, [bias, None]).astype(o_ref.dtype)


def _na_attention(q, k, v, table, n_batch):
    ctx0 = n_batch * LAT_BLKS
    n_steps = NA_ROWS // NA_QROWS
    tq = NA_QROWS * GRID_W
    lat = pl.BlockSpec((N_LAT, D), lambda b, j: (b, 0))
    ctx = pl.BlockSpec((N_CTX, D), lambda b, j: (ctx0 + b, 0))
    return pl.pallas_call(
        _na_kernel,
        grid=(n_batch, n_steps),
        in_specs=[
            pl.BlockSpec((tq, D), lambda b, j: (b * n_steps + j, 0)),
            lat, lat, ctx, ctx,
            pl.BlockSpec(table.shape, lambda b, j: (0, 0, 0, 0), pipeline_mode=pl.Buffered(1)),
        ],
        out_specs=pl.BlockSpec((tq, D), lambda b, j: (b * n_steps + j, 0)),
        out_shape=jax.ShapeDtypeStruct((n_batch * N_LAT, D), BF16),
        compiler_params=_cparams(2),
        name="na_attention",
    )(q, k, v, k, v, table)


def _ctx_attn_kernel(q_ref, k_ref, v_ref, o_ref):
    for hp in range(NA_HEADS // 2):
        cols = slice(hp * LANES, (hp + 1) * LANES)
        segs = [(k_ref[:, cols], v_ref[:, cols])]
        o_ref[:, cols] = _pair_attend(q_ref[:, cols], segs, [None]).astype(o_ref.dtype)


def _ctx_attention(q, k, v, n_batch):
    ctx0 = n_batch * LAT_BLKS
    spec = pl.BlockSpec((N_CTX, D), lambda b: (ctx0 + b, 0))
    return pl.pallas_call(
        _ctx_attn_kernel,
        grid=(n_batch,),
        in_specs=[spec, spec, spec],
        out_specs=pl.BlockSpec((N_CTX, D), lambda b: (b, 0)),
        out_shape=jax.ShapeDtypeStruct((n_batch * N_CTX, D), BF16),
        compiler_params=_cparams(1),
        name="na_ctx_attention",
    )(q, k, v)


def _na_mixer(h, mod3, g, w_kvq, rpb, n_batch):
    segs = [(0, D, 1.0, BF16), (D, 2 * D, 1.0, BF16), (2 * D, 3 * D, NA_HD ** -0.5 * LOG2E, BF16)]
    k, v, q = _proj(h, mod3, g, w_kvq.astype(BF16), segs, n_batch)
    return _na_attention(q, k, v, _na_bias_tables(rpb), n_batch), _ctx_attention(q, k, v, n_batch)


DA_TQ = 1024
DA_W = 2 * DA_HD


def _rope_tables():
    t = np.arange(N_LAT)
    row = (t // GRID_W).astype(np.float64)
    col = (t % GRID_W).astype(np.float64)
    per_axis = DA_HD // 2
    freqs = ROPE_BASE ** (-np.arange(0, per_axis, 2, dtype=np.float64) / per_axis)
    ar = row[:, None] * freqs
    ac = col[:, None] * freqs
    ang = np.concatenate([ar, ar, ac, ac], axis=-1)
    cos = np.tile(np.cos(ang), (1, 2))
    sin = np.tile(np.sin(ang), (1, 2))
    quarter = (np.arange(DA_W) % DA_HD) // (DA_HD // 4)
    even = (quarter % 2 == 0)[None, :]
    sin_a = np.where(even, -sin, 0.0)
    sin_b = np.where(even, 0.0, sin)
    return tuple(jnp.asarray(a, F32) for a in (cos, sin_a, sin_b))


def _rope(x, cos, sin_a, sin_b):
    q16 = DA_HD // 4
    return x * cos + pltpu.roll(x, DA_W - q16, 1) * sin_a + pltpu.roll(x, q16, 1) * sin_b


def _da_lambda(lp, lam_init):
    a = jnp.sum(lp[0:1] * lp[1:2], axis=-1, keepdims=True)
    b = jnp.sum(lp[2:3] * lp[3:4], axis=-1, keepdims=True)
    return jnp.exp(a) - jnp.exp(b) + lam_init


DA_KCHUNK = 512


def _diff_attend(q, k, vx, lam, ng, lam_init):
    lo = _lane_lo(q.shape)
    qm = [_half_masked(q, lo, 0), _half_masked(q, lo, 1)]
    n_keys = k.shape[0]
    chunks = [(c0, min(c0 + DA_KCHUNK, n_keys)) for c0 in range(0, n_keys, DA_KCHUNK)]
    s = [[_dot_nt(qm[j], k[c0:c1, :]) for j in range(2)] for c0, c1 in chunks]
    m = [None, None]
    for sc in s:
        for j in range(2):
            mc = sc[j].max(axis=-1, keepdims=True)
            m[j] = mc if m[j] is None else jnp.maximum(m[j], mc)
    r = [None, None]
    for sc, (c0, c1) in zip(s, chunks):
        for j in range(2):
            pv = _dot(jnp.exp2(sc[j] - m[j]).astype(BF16), vx[c0:c1, :])
            r[j] = pv if r[j] is None else r[j] + pv
    o = r[0][:, :DA_W] / r[0][:, DA_W:] - lam * (r[1][:, :DA_W] / r[1][:, DA_W:])
    return o * lax.rsqrt(jnp.mean(o * o, axis=-1, keepdims=True) + EPS) * ng * (1.0 - lam_init)


def _da_kernel(lam_ref, q_ref, k_ref, v_ref, kc_ref, vc_ref, cq_ref, saq_ref, sbq_ref,
               ck_ref, sak_ref, sbk_ref, ng_ref, o_ref, kr_ref, vx_ref, *, lam_init):
    @pl.when(pl.program_id(2) == 0)
    def _():
        kr_ref[:N_LAT, :] = _rope(k_ref[...], ck_ref[...], sak_ref[...], sbk_ref[...]).astype(BF16)
        kr_ref[N_LAT:, :] = kc_ref[...].astype(BF16)
        vx_ref[:N_LAT, :DA_W] = v_ref[...]
        vx_ref[N_LAT:, :DA_W] = vc_ref[...]
        vx_ref[:, DA_W:] = jnp.ones((N_LAT + N_CTX, DA_W), BF16)

    lam = _da_lambda(lam_ref[...], lam_init)
    q = (_rope(q_ref[...], cq_ref[...], saq_ref[...], sbq_ref[...]) * LOG2E).astype(BF16)
    o_ref[...] = _diff_attend(q, kr_ref, vx_ref, lam, ng_ref[...], lam_init).astype(o_ref.dtype)


def _da_attention(q, k, v, lam_p, norm_g, lam_init, n_batch):
    cos, sin_a, sin_b = _rope_tables()
    nqb = N_LAT // DA_TQ
    ctx0 = n_batch * LAT_BLKS
    tab_q = pl.BlockSpec((DA_TQ, DA_W), lambda b, h, i: (i, 0))
    tab_k = pl.BlockSpec((N_LAT, DA_W), lambda b, h, i: (0, 0))
    lat = pl.BlockSpec((N_LAT, DA_W), lambda b, h, i: (b, h))
    ctx = pl.BlockSpec((N_CTX, DA_W), lambda b, h, i: (ctx0 + b, h))
    return pl.pallas_call(
        functools.partial(_da_kernel, lam_init=lam_init),
        grid=(n_batch, DA_HEADS, nqb),
        in_specs=[
            pl.BlockSpec((4, DA_HD), lambda b, h, i: (0, 0)),
            pl.BlockSpec((DA_TQ, DA_W), lambda b, h, i: (b * nqb + i, h)),
            lat, lat, ctx, ctx,
            tab_q, tab_q, tab_q, tab_k, tab_k, tab_k,
            pl.BlockSpec((1, DA_W), lambda b, h, i: (0, h)),
        ],
        out_specs=pl.BlockSpec((DA_TQ, DA_W), lambda b, h, i: (b * nqb + i, h)),
        out_shape=jax.ShapeDtypeStruct((n_batch * N_LAT, D), BF16),
        scratch_shapes=[pltpu.VMEM((N_LAT + N_CTX, DA_W), BF16), pltpu.VMEM((N_LAT + N_CTX, 2 * DA_W), BF16)],
        compiler_params=_cparams(3),
        name="diff_attention",
    )(lam_p, q, k, v, k, v, cos, sin_a, sin_b, cos, sin_a, sin_b, norm_g.reshape(1, D))


def _da_ctx_kernel(lam_ref, q_ref, k_ref, v_ref, ng_ref, o_ref, *, lam_init):
    lam = _da_lambda(lam_ref[...], lam_init)
    for h in range(DA_HEADS):
        cols = slice(h * DA_W, (h + 1) * DA_W)
        v = v_ref[:, cols]
        vx = jnp.concatenate([v, jnp.ones_like(v)], axis=1)
        q = (q_ref[:, cols] * LOG2E).astype(BF16)
        o = _diff_attend(q, k_ref[:, cols].astype(BF16), vx, lam, ng_ref[:, cols], lam_init)
        o_ref[:, cols] = o.astype(o_ref.dtype)


def _da_ctx_attention(q, k, v, lam_p, norm_g, lam_init, n_batch):
    ctx0 = n_batch * LAT_BLKS
    spec = pl.BlockSpec((N_CTX, D), lambda b: (ctx0 + b, 0))
    return pl.pallas_call(
        functools.partial(_da_ctx_kernel, lam_init=lam_init),
        grid=(n_batch,),
        in_specs=[pl.BlockSpec((4, DA_HD), lambda b: (0, 0)), spec, spec, spec, pl.BlockSpec((1, D), lambda b: (0, 0))],
        out_specs=pl.BlockSpec((N_CTX, D), lambda b: (b, 0)),
        out_shape=jax.ShapeDtypeStruct((n_batch * N_CTX, D), BF16),
        compiler_params=_cparams(1),
        name="diff_ctx_attention",
    )(lam_p, q, k, v, norm_g.reshape(1, D))


def _da_mixer(h, mod3, g, w_kvq, lam_p, norm_g, layer_idx, n_batch):
    lam_init = 0.8 - 0.6 * float(np.exp(-0.3 * layer_idx))
    segs = [(0, D, 1.0, F32), (D, 2 * D, 1.0, BF16), (2 * D, 3 * D, DA_HD ** -0.5, F32)]
    k, v, q = _proj(h, mod3, g, w_kvq.astype(BF16), segs, n_batch)
    lam_p = lam_p.astype(F32)
    return (_da_attention(q, k, v, lam_p, norm_g, lam_init, n_batch),
            _da_ctx_attention(q, k, v, lam_p, norm_g, lam_init, n_batch))


SCAN_STEPS = 1 + LAT_BLKS


def _scan_row_blk(direction, n_batch):
    ctx0 = n_batch * LAT_BLKS

    def blk(b, s):
        lat = b * LAT_BLKS + (s - 1 if direction == 0 else LAT_BLKS - s)
        return jnp.where(s == 0, ctx0 + b, lat)
    return blk


def _scan_spec(width, direction, n_batch):
    blk = _scan_row_blk(direction, n_batch)
    return pl.BlockSpec((ROW_BLK, width), lambda b, h, s: (blk(b, s), h))


def _scan_spec_all(width, direction, n_batch):
    blk = _scan_row_blk(direction, n_batch)
    return pl.BlockSpec((ROW_BLK, width), lambda b, h, s: (blk(b, s), 0))


def _scan_tri():
    u = np.arange(ROW_BLK)
    tt, uu = np.meshgrid(u, u, indexing="ij")
    return jnp.asarray(np.stack([uu <= tt, uu >= tt]).astype(np.float32), BF16)


def _split3(x):
    hi = x.astype(BF16)
    r1 = x - hi.astype(F32)
    mid = r1.astype(BF16)
    lo = (r1 - mid.astype(F32)).astype(BF16)
    return hi, mid, lo


def _scan_out_specs(width):
    return [pl.BlockSpec((N_LAT, width), lambda b, h, s: (b, h)), pl.BlockSpec((N_CTX, width), lambda b, h, s: (b, h))]


def _scan_write_sum(step, out_fwd, out_bwd, lat_ref, ctx_ref):
    @pl.when(step == 0)
    def _():
        ctx_ref[...] = out_fwd + out_bwd

    rows_f = pl.ds(pl.multiple_of((step - 1) * ROW_BLK, ROW_BLK), ROW_BLK)
    rows_b = pl.ds(pl.multiple_of((LAT_BLKS - step) * ROW_BLK, ROW_BLK), ROW_BLK)

    @pl.when(jnp.logical_and(step >= 1, step <= LAT_BLKS // 2))
    def _():
        lat_ref[rows_f, :] = out_fwd
        lat_ref[rows_b, :] = out_bwd

    @pl.when(step > LAT_BLKS // 2)
    def _():
        lat_ref[rows_f, :] += out_fwd
        lat_ref[rows_b, :] += out_bwd


def _ml_conv_kernel(xk_ref, xkp_ref, xkn_ref, xq_ref, xqp_ref, xqn_ref, w_ref, ok_ref, oq_ref, *, n_lat_blks):
    i = pl.program_id(0)
    is_ctx = i >= n_lat_blks
    first = jnp.logical_or(is_ctx, i % LAT_BLKS == 0)
    last = jnp.logical_or(is_ctx, i % LAT_BLKS == LAT_BLKS - 1)
    rid = lax.broadcasted_iota(jnp.int32, (ROW_BLK, D), 0)
    groups = ((xk_ref, xkp_ref, xkn_ref, 0, ok_ref, ML_HD ** -0.5), (xq_ref, xqp_ref, xqn_ref, D, oq_ref, 1.0))
    for x_ref, xp_ref, xn_ref, c0, o_ref, scale in groups:
        x = x_ref[...]
        prev_row = jnp.where(first, 0.0, xp_ref[7:8, :])
        next_row = jnp.where(last, 0.0, xn_ref[0:1, :])
        x_prev = jnp.where(rid == 0, prev_row, pltpu.roll(x, 1, 0))
        x_next = jnp.where(rid == ROW_BLK - 1, next_row, pltpu.roll(x, ROW_BLK - 1, 0))
        y = w_ref[0:1, c0:c0 + D] * x_prev + w_ref[1:2, c0:c0 + D] * x + w_ref[2:3, c0:c0 + D] * x_next
        o_ref[...] = (_silu(y) * scale).astype(o_ref.dtype)


def _ml_conv(pk, pq, conv_w, n_batch):
    n_rows = pk.shape[0]
    n_blk = n_rows // ROW_BLK
    sub = ROW_BLK // 8
    main = pl.BlockSpec((ROW_BLK, D), lambda i: (i, 0))
    prev = pl.BlockSpec((8, D), lambda i: (jnp.maximum(i * sub - 1, 0), 0))
    nxt = pl.BlockSpec((8, D), lambda i: (jnp.minimum((i + 1) * sub, n_blk * sub - 1), 0))
    return pl.pallas_call(
        functools.partial(_ml_conv_kernel, n_lat_blks=n_batch * LAT_BLKS),
        grid=(n_blk,),
        in_specs=[main, prev, nxt, main, prev, nxt, pl.BlockSpec((3, 2 * D), lambda i: (0, 0))],
        out_specs=[main, main],
        out_shape=[jax.ShapeDtypeStruct((n_rows, D), BF16)] * 2,
        compiler_params=_cparams(1),
        name="mlstm_conv",
    )(pk, pk, pk, pq, pq, pq, conv_w)


def _row_sum(x):
    return jnp.sum(x[:, :LANES] + x[:, LANES:], axis=1, keepdims=True)


def _pick_lane(x, idx):
    lane = lax.broadcasted_iota(jnp.int32, x.shape, 1)
    return jnp.sum(jnp.where(lane == idx, x, 0.0), axis=1, keepdims=True)


def _pick_sublane(x, idx):
    sub = lax.broadcasted_iota(jnp.int32, x.shape, 0)
    return jnp.sum(jnp.where(sub == idx, x, 0.0), axis=0, keepdims=True)


def _ml_scan_kernel(gb_ref, g0_ref, q0_ref, k0_ref, v0_ref, g1_ref, q1_ref, k1_ref, v1_ref,
                    lat_ref, ctx_ref, c_ref, n_ref, m_ref):
    head = pl.program_id(1)

    @pl.when(pl.program_id(2) == 0)
    def _():
        c_ref[...] = jnp.zeros_like(c_ref)
        n_ref[...] = jnp.zeros_like(n_ref)
        m_ref[...] = jnp.zeros_like(m_ref)

    r = lax.broadcasted_iota(jnp.int32, (ROW_BLK, ROW_BLK), 0)
    c = lax.broadcasted_iota(jnp.int32, (ROW_BLK, ROW_BLK), 1)
    outs = [None, None]

    def direction(d, g_ref, q_ref, k_ref, v_ref):
        valid = (r >= c) if d == 0 else (r <= c)
        other = (r <= c) if d == 0 else (r >= c)
        g = g_ref[...] + gb_ref[...]
        gt = g.T
        ii = d * 2 * ML_HEADS + head
        fi = ii + ML_HEADS
        m_prev = m_ref[d][0:1, 0:1]
        n_prev = n_ref[d][0:1, :]
        c_prev = c_ref[d]
        qc = q_ref[...]
        kc = k_ref[...]
        vc = v_ref[...]
        qk = _dot_nt(qc, kc)
        q_c = _dot(qc, c_prev.astype(BF16))
        q_n = _row_sum(qc.astype(F32) * n_prev)
        yield
        i_col = _pick_lane(g, ii)
        f_col = _log_sigmoid(_pick_lane(g, fi))
        i_row = _pick_sublane(gt, ii)
        f_row = _log_sigmoid(_pick_sublane(gt, fi))
        yield
        b_col = _row_sum(jnp.where(valid, f_row, 0.0))
        b_row = jnp.sum(jnp.where(other, f_col, 0.0), axis=0, keepdims=True)
        total = jnp.sum(f_col, axis=0, keepdims=True)
        yield
        dmat = jnp.where(valid, b_col - b_row + i_row, NEG)
        inter = b_col + m_prev
        m_t = jnp.maximum(inter, jnp.maximum(dmat[:, :LANES], dmat[:, LANES:]).max(axis=1, keepdims=True))
        w_inter = jnp.exp(inter - m_t)
        g_col = total - b_col + i_col
        m_new = jnp.maximum(total + m_prev, g_col.max(axis=0, keepdims=True))
        w_old = jnp.exp(total + m_prev - m_new)
        kw = kc.astype(F32) * jnp.exp(g_col - m_new)
        upd = _dot_tn(kw.astype(BF16), vc)
        yield
        s = qk * jnp.exp(dmat - m_t)
        num = w_inter * q_c + _dot(s.astype(BF16), vc)
        den = w_inter * q_n + _row_sum(s)
        outs[d] = num / jnp.maximum(jnp.abs(den), jnp.exp(-m_t))
        yield
        c_ref[d] = w_old * c_prev + upd
        n_ref[d] = jnp.broadcast_to(w_old * n_prev + kw.sum(axis=0, keepdims=True), n_ref.shape[1:])
        m_ref[d] = jnp.broadcast_to(m_new, m_ref.shape[1:])

    stages = [direction(0, g0_ref, q0_ref, k0_ref, v0_ref), direction(1, g1_ref, q1_ref, k1_ref, v1_ref)]
    while stages:
        stages = [gen for gen in stages if next(gen, StopIteration) is not StopIteration]
    _scan_write_sum(pl.program_id(2), outs[0], outs[1], lat_ref, ctx_ref)


def _ml_scan(q, k, v, gates, gate_b, n_batch):
    in_specs = [pl.BlockSpec((1, LANES), lambda b, h, s: (0, 0))]
    for d in range(2):
        in_specs += [_scan_spec_all(LANES, d, n_batch)] + [_scan_spec(ML_HD, d, n_batch)] * 3
    return pl.pallas_call(
        _ml_scan_kernel,
        grid=(n_batch, ML_HEADS, SCAN_STEPS),
        in_specs=in_specs,
        out_specs=_scan_out_specs(ML_HD),
        out_shape=[jax.ShapeDtypeStruct((n_batch * N_LAT, D), F32), jax.ShapeDtypeStruct((n_batch * N_CTX, D), F32)],
        scratch_shapes=[pltpu.VMEM((2, ML_HD, ML_HD), F32), pltpu.VMEM((2, 8, ML_HD), F32),
                        pltpu.VMEM((2, 8, LANES), F32)],
        compiler_params=_cparams(3),
        name="mlstm_scan",
    )(gate_b, gates, q, k, v, gates, q, k, v)


def _ml_mixer(h, mod3, g, w_in, gate_b, conv_w, n_batch):
    n_g = 4 * ML_HEADS
    w_in = w_in.astype(BF16)
    wk, wv, wg, wq, wo = (w_in[:, :D], w_in[:, D:2 * D], w_in[:, 2 * D:2 * D + n_g],
                          w_in[:, 2 * D + n_g:3 * D + n_g], w_in[:, 3 * D + n_g:])
    w = jnp.concatenate([wk, wv, wq, wo, jnp.pad(wg, ((0, 0), (0, LANES - n_g)))], axis=1)
    segs = [(0, D, 1.0, F32), (D, 2 * D, 1.0, BF16), (2 * D, 3 * D, 1.0, F32), (3 * D, 4 * D, 1.0, F32),
            (4 * D, 4 * D + LANES, 1.0, F32)]
    pk, v, pq, og, gates = _proj(h, mod3, g, w, segs, n_batch)
    k, q = _ml_conv(pk, pq, conv_w.astype(F32), n_batch)
    gb = jnp.pad(gate_b.astype(F32), (0, LANES - n_g)).reshape(1, LANES)
    h_lat, h_ctx = _ml_scan(q, k, v, gates, gb, n_batch)
    return h_lat, h_ctx, og


GLA_LEVELS = (256, 128, 64, 32, 16, 8, 4, 2)
GLA_BCAST_LEVELS = 6
GLA_DIAG = len(GLA_LEVELS)


def _gla_level_ids():
    u = np.arange(ROW_BLK // 2)
    tt, ss = np.meshgrid(u, u, indexing="ij")
    hb = np.floor(np.log2(np.maximum(tt ^ ss, 1))).astype(np.int64)
    level = len(GLA_LEVELS) - 1 - hb
    lid = np.stack([np.where(tt == ss, GLA_DIAG, np.where(tt > ss, level, -1)),
                    np.where(tt == ss, GLA_DIAG, np.where(tt < ss, level, -1))]).astype(np.int32)
    return jnp.asarray(np.concatenate([lid, lid], axis=1))


def _gla_level_exponents(level, d, bc, bc_ref, la, la_prev, la_next, row):
    n = GLA_LEVELS[level]
    half = n // 2
    if level < GLA_BCAST_LEVELS:
        parts = []
        for j in range(ROW_BLK // n):
            mid = j * n + (half - 1 if d == 0 else half)
            parts.append(jnp.broadcast_to(bc_ref[mid:mid + 1, :], (n, GLA_DK)))
        x = bc - (parts[0] if len(parts) == 1 else jnp.concatenate(parts, axis=0))
        return jnp.minimum(x, 0.0), jnp.minimum(-x, 0.0)
    o = row & (n - 1)
    if n == 4:
        if d == 0:
            return (jnp.where(o == 2, la, jnp.where(o == 3, la + la_prev, 0.0)), jnp.where(o == 0, la_next, 0.0))
        return (jnp.where(o == 0, la + la_next, jnp.where(o == 1, la, 0.0)), jnp.where(o == 3, la_prev, 0.0))
    return jnp.where(o == (1 if d == 0 else 0), la, 0.0), None


def _gla_kernel(tri_ref, lid_ref, wup_ref, bg_ref, q0_ref, k0_ref, v0_ref, l0_ref, q1_ref, k1_ref, v1_ref, l1_ref,
                lat_ref, ctx_ref, st_ref, bc_ref, att_ref):
    @pl.when(pl.program_id(2) == 0)
    def _():
        st_ref[...] = jnp.zeros_like(st_ref)

    row = lax.broadcasted_iota(jnp.int32, (ROW_BLK, GLA_DK), 0)
    outs = [None, None]

    def direction(d, q_ref, k_ref, v_ref, low_ref):
        pre = _dot(low_ref[...].astype(BF16), wup_ref[d]) + bg_ref[d]
        la = _log_sigmoid(pre) * (LOG2E / GLA_TAU)
        hi, mid, lo = _split3(la)
        tri = tri_ref[d]
        bc = _dot(tri, hi) + _dot(tri, mid) + _dot(tri, lo)
        bc_ref[d] = bc
        la_prev = pltpu.roll(la, 1, 0)
        la_next = pltpu.roll(la, ROW_BLK - 1, 0)
        q = q_ref[...]
        k = k_ref[...]
        v = v_ref[...]
        lid = lid_ref[d]
        kb = k.astype(BF16)
        half = ROW_BLK // 2

        def diag_blocks(a, b):
            return jnp.concatenate([_dot_nt(a[:half], b[:half]), _dot_nt(a[half:], b[half:])], axis=0)

        att_ref[d] = jnp.where(lid == GLA_DIAG, diag_blocks(q.astype(BF16), kb), 0.0)
        yield
        eq, ek = _gla_level_exponents(0, d, bc, bc_ref.at[d], la, la_prev, la_next, row)
        q_rows, k_rows = (slice(half, None), slice(None, half)) if d == 0 else (slice(None, half), slice(half, None))
        off = _dot_nt((q[q_rows] * jnp.exp2(eq[q_rows])).astype(BF16), (k[k_rows] * jnp.exp2(ek[k_rows])).astype(BF16))
        yield
        for level in range(1, len(GLA_LEVELS)):
            eq, ek = _gla_level_exponents(level, d, bc, bc_ref.at[d], la, la_prev, la_next, row)
            ql = (q * jnp.exp2(eq)).astype(BF16)
            kl = kb if ek is None else (k * jnp.exp2(ek)).astype(BF16)
            att_ref[d] = jnp.where(lid == level, diag_blocks(ql, kl), att_ref[d])
            yield
        st = st_ref[d]
        qd = (q * jnp.exp2(bc)).astype(BF16)
        diag = att_ref[d]
        if d == 0:
            o_top = _dot(diag[:half].astype(BF16), v[:half])
            o_bot = _dot(jnp.concatenate([off, diag[half:]], axis=1).astype(BF16), v)
        else:
            o_top = _dot(jnp.concatenate([diag[:half], off], axis=1).astype(BF16), v)
            o_bot = _dot(diag[half:].astype(BF16), v[half:])
        outs[d] = jnp.concatenate([o_top, o_bot], axis=0) + _dot_nt(qd, st.astype(BF16))
        b_end = bc[ROW_BLK - 1:ROW_BLK] if d == 0 else bc[0:1]
        kd = (k * jnp.exp2(b_end - bc)).astype(BF16)
        st_ref[d] = st * jnp.exp2(b_end) + _dot_tn(v, kd)

    stages = [direction(0, q0_ref, k0_ref, v0_ref, l0_ref), direction(1, q1_ref, k1_ref, v1_ref, l1_ref)]
    while stages:
        stages = [gen for gen in stages if next(gen, StopIteration) is not StopIteration]
    _scan_write_sum(pl.program_id(2), outs[0], outs[1], lat_ref, ctx_ref)


def _gla_scan(q, k, v, low, wup, bg, n_batch):
    tri, lid = _scan_tri(), _gla_level_ids()
    const3 = lambda b, h, s: (0, 0, 0)
    in_specs = [
        pl.BlockSpec(tri.shape, const3),
        pl.BlockSpec(lid.shape, const3),
        pl.BlockSpec((2, LANES, GLA_DK), lambda b, h, s: (0, 0, h)),
        pl.BlockSpec((2, 1, GLA_DK), lambda b, h, s: (0, 0, h)),
    ]
    for d in range(2):
        in_specs += [_scan_spec(GLA_DK, d, n_batch), _scan_spec(GLA_DK, d, n_batch),
                     _scan_spec(GLA_DV, d, n_batch), _scan_spec_all(LANES, d, n_batch)]
    return pl.pallas_call(
        _gla_kernel,
        grid=(n_batch, GLA_HEADS, SCAN_STEPS),
        in_specs=in_specs,
        out_specs=_scan_out_specs(GLA_DV),
        out_shape=[jax.ShapeDtypeStruct((n_batch * N_LAT, D), F32), jax.ShapeDtypeStruct((n_batch * N_CTX, D), F32)],
        scratch_shapes=[pltpu.VMEM((2, GLA_DV, GLA_DK), F32), pltpu.VMEM((2, ROW_BLK, GLA_DK), F32),
                        pltpu.VMEM((2, ROW_BLK, ROW_BLK // 2), F32)],
        compiler_params=_cparams(3),
        name="gla_scan",
    )(tri, lid, wup, bg, q, k, v, low, q, k, v, low)


def _gla_mixer(h, mod3, g, w_in, w_gate_up, b_gate, n_batch):
    dk_t = GLA_HEADS * GLA_DK
    n_low = 2 * GLA_RANK
    w_in = w_in.astype(BF16)
    wk, wv, wl, wq, wr = (w_in[:, :dk_t], w_in[:, dk_t:dk_t + D], w_in[:, dk_t + D:dk_t + D + n_low],
                          w_in[:, dk_t + D + n_low:2 * dk_t + D + n_low], w_in[:, 2 * dk_t + D + n_low:])
    w = jnp.concatenate([wk, wq, wv, wr, jnp.pad(wl, ((0, 0), (0, LANES - n_low)))], axis=1)
    c = [0, dk_t, 2 * dk_t, 2 * dk_t + D, 2 * dk_t + 2 * D, 2 * dk_t + 2 * D + LANES]
    segs = [(c[0], c[1], 1.0, F32), (c[1], c[2], GLA_DK ** -0.5, F32), (c[2], c[3], 1.0, BF16),
            (c[3], c[4], 1.0, F32), (c[4], c[5], 1.0, F32)]
    k, q, v, rg, low = _proj(h, mod3, g, w, segs, n_batch)
    wup = jnp.zeros((2, LANES, dk_t), F32)
    for r in range(2):
        wup = wup.at[r, r * GLA_RANK:(r + 1) * GLA_RANK].set(w_gate_up[r].astype(F32))
    o_lat, o_ctx = _gla_scan(q, k, v, low, wup.astype(BF16), b_gate.astype(F32).reshape(2, 1, dk_t), n_batch)
    return o_lat, o_ctx, rg


def kernel(x, c, ctx, c_ctx, ada_w, ada_b, norm_g, ffn_w13, ffn_w2, final_g, na_w_kvq, na_rpb, na_w_o, ml_w_in, ml_gate_b, ml_conv_w, ml_norm_g, ml_w_o, da_w_kvq, da_lam, da_norm_g, da_w_o, gla_w_in, gla_w_gate_up, gla_b_gate, gla_norm_g, gla_w_o):
    nb = x.shape[0]
    assert x.shape[1:] == (N_LAT, D) and ctx.shape[1:] == (N_CTX, D) and nb < MOD_ROWS
    n_lat = nb * N_LAT
    n_tok = n_lat + nb * N_CTX

    s = jnp.zeros((MOD_ROWS, D), F32).at[:nb].set(c).at[nb].set(c_ctx)
    modtab = _mod_tables(s, ada_w, ada_b)
    h = (x.reshape(n_lat, D), ctx.reshape(nb * N_CTX, D))
    w13 = ffn_w13.astype(BF16)
    w2 = ffn_w2.astype(BF16)

    for i in range(DEPTH):
        kind, j = i % 4, i // 4
        last = i == DEPTH - 1
        mod3 = modtab[i].reshape(MOD_ROWS * N_MOD, 1, D)
        h = _ffn(h, mod3, norm_g[i, 0], w13, w2, (i, 0), final_g, 0, n_tok, nb)
        n_out = n_lat if last else n_tok
        if kind == 0:
            ins = _na_mixer(h, mod3, norm_g[i, 1], na_w_kvq[j], na_rpb[j], nb)
            mixer, w_o, mixer_g = "plain", na_w_o[j], None
        elif kind == 1:
            ins = _ml_mixer(h, mod3, norm_g[i, 1], ml_w_in[j], ml_gate_b[j], ml_conv_w[j], nb)
            mixer, w_o, mixer_g = "ml", ml_w_o[j], ml_norm_g[j]
        elif kind == 2:
            ins = _da_mixer(h, mod3, norm_g[i, 1], da_w_kvq[j], da_lam[j], da_norm_g[j], i, nb)
            mixer, w_o, mixer_g = "plain", da_w_o[j], None
        else:
            ins = _gla_mixer(h, mod3, norm_g[i, 1], gla_w_in[j], gla_w_gate_up[j], gla_b_gate[j], nb)
            mixer, w_o, mixer_g = "gla", gla_w_o[j], gla_norm_g[j]
        h = _ffn(h, mod3, norm_g[i, 2], w13, w2, (i, 1), final_g, 6, n_out, nb, final=last, mixer=mixer, mixer_ins=ins, w_o=w_o.astype(BF16), mixer_g=mixer_g)
    return h.reshape(nb, N_LAT, D)
```

```python
import functools

import numpy as np
import jax
import jax.numpy as jnp
from jax import lax
from jax.experimental import pallas as pl
from jax.experimental.pallas import tpu as pltpu

F32 = jnp.float32
BF16 = jnp.bfloat16

D = 1024
N_LAT = 2048
N_CTX = 256
DEPTH = 4
N_MOD = 9
D_FF = 2816
EPS = 1e-6
NEG = -1e30
LOG2E = 1.4426950408889634
GRID_W = 64

NA_HEADS = 16
NA_HD = 64
NA_WIN_R = 8
NA_WIN_C = 16

ML_HEADS = 4
ML_HD = 256

DA_HEADS = 8
DA_HD = 64
ROPE_BASE = 10000.0

GLA_HEADS = 4
GLA_DK = 128
GLA_DV = 256
GLA_RANK = 16
GLA_TAU = 16.0

ROW_BLK = 256
LAT_BLKS = N_LAT // ROW_BLK
MOD_ROWS = 16
LANES = 128
VMEM_LIMIT = 56 * 1024 * 1024


def _cparams(n_axes, vmem=VMEM_LIMIT):
    return pltpu.CompilerParams(dimension_semantics=("arbitrary",) * n_axes, vmem_limit_bytes=vmem)


def _sigmoid(x):
    return 1.0 / (1.0 + jnp.exp(-x))


def _silu(x):
    return x * _sigmoid(x)


def _log_sigmoid(x):
    return jnp.minimum(x, 0.0) - jnp.log(1.0 + jnp.exp(-jnp.abs(x)))


def _rms(x, g):
    return x * lax.rsqrt(jnp.mean(x * x, axis=-1, keepdims=True) + EPS) * g


def _dot(a, b):
    return jnp.dot(a, b, preferred_element_type=F32)


def _dot_nt(a, b):
    return lax.dot_general(a, b, (((1,), (1,)), ((), ())), preferred_element_type=F32)


def _dot_tn(a, b):
    return lax.dot_general(a, b, (((0,), (0,)), ((), ())), preferred_element_type=F32)


def _mod_row(i, tm, n_batch):
    return jnp.minimum(i // (N_LAT // tm), n_batch)


def _mod_spec(k, tm, n_batch):
    return pl.BlockSpec((1, 1, D), lambda i: (_mod_row(i, tm, n_batch) * N_MOD + k, 0, 0))


def _mod_kernel(s_ref, w_ref, b_ref, o_ref):
    a = _silu(s_ref[...]).astype(BF16)
    o_ref[0] = _dot(a, w_ref[0].astype(BF16)) + b_ref[0]


def _mod_tables(s, ada_w, ada_b):
    return pl.pallas_call(
        _mod_kernel,
        grid=(DEPTH, N_MOD),
        in_specs=[
            pl.BlockSpec((MOD_ROWS, D), lambda i, k: (0, 0)),
            pl.BlockSpec((1, D, D), lambda i, k: (i, 0, k)),
            pl.BlockSpec((1, 1, D), lambda i, k: (i, 0, k)),
        ],
        out_specs=pl.BlockSpec((1, MOD_ROWS, D), lambda i, k: (i, 0, k)),
        out_shape=jax.ShapeDtypeStruct((DEPTH, MOD_ROWS, N_MOD * D), F32),
        compiler_params=_cparams(2),
        name="mod_tables",
    )(s, ada_w, ada_b.reshape(DEPTH, 1, N_MOD * D))


FFN_TM = 512
MXU_TILE = 256
FFN_SPLIT = (D_FF // MXU_TILE + 1) // 2 * MXU_TILE
FFN_CHUNKS = ((0, FFN_SPLIT), (FFN_SPLIT, D_FF))


def _head_rms(x, g, n_heads):
    hd = D // n_heads
    parts = []
    for i in range(n_heads):
        xi = x[:, i * hd:(i + 1) * hd]
        parts.append(xi * lax.rsqrt(jnp.mean(xi * xi, axis=-1, keepdims=True) + EPS))
    return jnp.concatenate(parts, axis=-1) * g


def _mixer_readout(mode, refs, is_lat):
    y = jnp.where(is_lat, refs[0][...], refs[1][...])
    if mode == "ml":
        return _sigmoid(refs[2][...]) * _head_rms(y, refs[3][...], ML_HEADS)
    if mode == "gla":
        return _head_rms(y, refs[3][...], GLA_HEADS) * _silu(refs[2][...])
    return y


def _ffn_kernel(g_ref, sh_ref, sc_ref, gt_ref, w13_ref, w2_ref, fg_ref, *refs, final, mixer, split_h, n_lat_tiles):
    o_ref = refs[-1]
    if split_h:
        x = jnp.where(pl.program_id(0) < n_lat_tiles, refs[0][...], refs[1][...])
        refs = refs[2:]
    else:
        x = refs[0][...]
        refs = refs[1:]
    if mixer is not None:
        mg_ref, wo_ref = refs[:2]
        y = _mixer_readout(mixer, refs[2:-1], pl.program_id(0) < n_lat_tiles)
        x = x + mg_ref[0] * _dot(y.astype(BF16), wo_ref[...])
    a = (_rms(x, g_ref[...]) * (1.0 + sc_ref[0]) + sh_ref[0]).astype(BF16)
    acc = jnp.zeros(x.shape, F32)
    for c0, c1 in FFN_CHUNKS:
        gate = _dot(a, w13_ref[:, c0:c1])
        up = _dot(a, w13_ref[:, D_FF + c0:D_FF + c1])
        act = (_silu(gate) * up).astype(BF16)
        acc = acc + _dot(act, w2_ref[c0:c1, :])
    y = x + 0.5 * gt_ref[0] * acc
    if final:
        y = _rms(y, fg_ref[...])
    o_ref[...] = y


def _ffn(h, mod3, g, w13, w2, widx, final_g, k0, n_rows, n_batch, final=False, mixer=None, mixer_ins=(), w_o=None,
         mixer_g=None):
    tm = FFN_TM
    nl = n_batch * N_LAT // tm
    const = lambda i: (0, 0)
    row = pl.BlockSpec((tm, D), lambda i: (i, 0))
    lat_ctx = [pl.BlockSpec((tm, D), lambda i: (jnp.minimum(i, nl - 1), 0)),
               pl.BlockSpec((tm, D), lambda i: (jnp.maximum(i - nl, 0), 0))]
    split_h = isinstance(h, tuple)
    in_specs = [
        pl.BlockSpec((1, D), const),
        _mod_spec(k0, tm, n_batch),
        _mod_spec(k0 + 1, tm, n_batch),
        _mod_spec(k0 + 2, tm, n_batch),
        pl.BlockSpec((None, None, D, 2 * D_FF), lambda i: (*widx, 0, 0), pipeline_mode=pl.Buffered(1)),
        pl.BlockSpec((None, None, D_FF, D), lambda i: (*widx, 0, 0), pipeline_mode=pl.Buffered(1)),
        pl.BlockSpec((1, D), const),
    ] + (lat_ctx if split_h else [row])
    args = [g.reshape(1, D), mod3, mod3, mod3, w13, w2, final_g.reshape(1, D)] + (list(h) if split_h else [h])
    if mixer is not None:
        in_specs += [_mod_spec(5, tm, n_batch), pl.BlockSpec((D, D), const, pipeline_mode=pl.Buffered(1))] + lat_ctx
        args += [mod3, w_o, mixer_ins[0], mixer_ins[1]]
        if mixer != "plain":
            in_specs += [row, pl.BlockSpec((1, D), const)]
            args += [mixer_ins[2], mixer_g.reshape(1, D)]
    return pl.pallas_call(
        functools.partial(_ffn_kernel, final=final, mixer=mixer, split_h=split_h, n_lat_tiles=nl),
        grid=(n_rows // tm,),
        in_specs=in_specs,
        out_specs=row,
        out_shape=jax.ShapeDtypeStruct((n_rows, D), F32),
        compiler_params=_cparams(1),
        name="ffn" if mixer is None else "mixer_out_ffn",
    )(*args)


PROJ_TM = 512


def _proj_kernel(h_ref, g_ref, sh_ref, sc_ref, w_ref, *refs, segs, n_rope):
    rope_refs, o_refs = refs[:n_rope], refs[n_rope:]
    x = h_ref[...]
    a = (_rms(x, g_ref[...]) * (1.0 + sc_ref[0]) + sh_ref[0]).astype(BF16)
    for o_ref, (c0, c1, scale, rope) in zip(o_refs, segs):
        y = _dot(a, w_ref[:, c0:c1])
        if scale != 1.0:
            y = y * scale
        if rope:
            tabs = [r[...] for r in rope_refs]
            y = jnp.concatenate([_rope(y[:, c:c + LANES], *tabs) for c in range(0, c1 - c0, LANES)], axis=1)
        o_ref[...] = y.astype(o_ref.dtype)


def _proj(h, mod3, g, w, segs, n_batch, rope_tabs=()):
    tm = PROJ_TM
    n_rows = h.shape[0]
    const = lambda i: (0, 0)
    n_pos = N_LAT // tm
    nl = n_batch * n_pos
    rope_spec = pl.BlockSpec((tm, LANES), lambda i: (jnp.where(i < nl, i % n_pos, n_pos), 0))
    identity = (jnp.ones((tm, LANES), F32), jnp.zeros((tm, LANES), F32), jnp.zeros((tm, LANES), F32))
    rope_args = [jnp.concatenate([t, e], axis=0) for t, e in zip(rope_tabs, identity)]
    return pl.pallas_call(
        functools.partial(_proj_kernel, segs=tuple((s[0], s[1], s[2], len(s) > 4 and s[4]) for s in segs),
                          n_rope=len(rope_args)),
        grid=(n_rows // tm,),
        in_specs=[
            pl.BlockSpec((tm, D), lambda i: (i, 0)),
            pl.BlockSpec((1, D), const),
            _mod_spec(3, tm, n_batch),
            _mod_spec(4, tm, n_batch),
            pl.BlockSpec(w.shape, const, pipeline_mode=pl.Buffered(1)),
        ] + [rope_spec] * len(rope_args),
        out_specs=[pl.BlockSpec((tm, s[1] - s[0]), lambda i: (i, 0)) for s in segs],
        out_shape=[jax.ShapeDtypeStruct((n_rows, s[1] - s[0]), s[3]) for s in segs],
        compiler_params=_cparams(1),
        name="mixer_proj",
    )(h, g.reshape(1, D), mod3, mod3, w, *rope_args)


def _lane_lo(shape):
    return lax.broadcasted_iota(jnp.int32, shape, 1) < (LANES // 2)


def _half_masked(x, lo, head):
    return jnp.where(lo if head == 0 else jnp.logical_not(lo), x, jnp.zeros_like(x))


def _pair_attend(qp, segs, biases):
    tq = qp.shape[0]
    lo = _lane_lo(qp.shape)
    q2 = jnp.concatenate([_half_masked(qp, lo, 0), _half_masked(qp, lo, 1)], axis=0)
    scores = []
    for (k, _), b in zip(segs, biases):
        s = _dot_nt(q2, k)
        scores.append(s if b is None else s + b)
    m = scores[0].max(axis=-1, keepdims=True)
    for s in scores[1:]:
        m = jnp.maximum(m, s.max(axis=-1, keepdims=True))
    acc = None
    for s, (_, v) in zip(scores, segs):
        vx = jnp.concatenate([v, jnp.ones_like(v)], axis=1)
        pv = _dot(jnp.exp2(s - m).astype(BF16), vx)
        acc = pv if acc is None else acc + pv
    o2 = acc[:, :LANES] / acc[:, LANES:]
    return jnp.where(lo, o2[:tq], o2[tq:])


NA_ROWS = N_LAT // GRID_W
NA_QROWS = 4
NA_KROWS = 12
NA_NKEY = NA_KROWS * GRID_W
NA_NOFF = 2 * NA_WIN_R - 1
NA_NENT = NA_NOFF + 1


NA_NCOFF = 2 * NA_WIN_C - 1


def _na_table_kernel(rpb_ref, o_ref):
    h = pl.program_id(0)
    c = lax.broadcasted_iota(jnp.int32, (GRID_W, LANES), 0)
    lane = lax.broadcasted_iota(jnp.int32, (GRID_W, LANES), 1)
    kc = lane & (GRID_W - 1)
    second = lane >= GRID_W
    col_off = jnp.clip(kc - c, 1 - NA_WIN_C, NA_WIN_C - 1) + NA_WIN_C - 1
    win0 = jnp.clip(c - NA_WIN_C // 2, 0, GRID_W - NA_WIN_C)
    in_win = jnp.logical_and(kc >= win0, kc < win0 + NA_WIN_C)
    for e in range(NA_NENT):
        acc = jnp.zeros((GRID_W, LANES), F32)
        for o in range(NA_NCOFF):
            first_v = rpb_ref[(h * NA_NOFF + e - 1) * NA_NCOFF + o] if e >= 1 else 0.0
            second_v = rpb_ref[(h * NA_NOFF + e) * NA_NCOFF + o] if e < NA_NOFF else 0.0
            acc = jnp.where(col_off == o, jnp.where(second, second_v, first_v), acc)
        o_ref[0, e] = jnp.where(in_win, acc * LOG2E, NEG)


def _na_bias_tables(rpb):
    return pl.pallas_call(
        _na_table_kernel,
        grid=(NA_HEADS,),
        in_specs=[pl.BlockSpec(memory_space=pltpu.SMEM)],
        out_specs=pl.BlockSpec((1, NA_NENT, GRID_W, LANES), lambda h: (h, 0, 0, 0)),
        out_shape=jax.ShapeDtypeStruct((NA_HEADS, NA_NENT, GRID_W, LANES), F32),
        compiler_params=_cparams(1),
        name="na_bias_table",
    )(rpb.astype(F32).reshape(-1))


def _na_key_row0(j):
    return jnp.clip(NA_QROWS * j - NA_WIN_R // 2, 0, NA_ROWS - NA_KROWS)


def _na_kernel(q_ref, k_ref, v_ref, kc_ref, vc_ref, t_ref, o_ref):
    j = pl.program_id(1)
    u0 = _na_key_row0(j)
    start = pl.multiple_of(u0 * GRID_W, GRID_W)
    lo = _lane_lo((1, LANES))
    entry = []
    rowmask = []
    for i in range(NA_QROWS):
        r = NA_QROWS * j + i
        r0 = jnp.clip(r - NA_WIN_R // 2, 0, NA_ROWS - NA_WIN_R)
        entry.append([])
        rowmask.append([])
        for p in range(NA_KROWS // 2):
            a = u0 + 2 * p
            entry[i].append(jnp.clip(a - r + NA_WIN_R, 0, NA_NENT - 1))
            out_a = jnp.logical_or(a < r0, a >= r0 + NA_WIN_R)
            out_b = jnp.logical_or(a + 1 < r0, a + 1 >= r0 + NA_WIN_R)
            rowmask[i].append(jnp.where(lo, jnp.where(out_a, NEG, 0.0), jnp.where(out_b, NEG, 0.0)))
    for hp in range(NA_HEADS // 2):
        cols = slice(hp * LANES, (hp + 1) * LANES)
        bias_rows = []
        for head in range(2):
            for i in range(NA_QROWS):
                tiles = [t_ref[2 * hp + head, entry[i][p]] + rowmask[i][p] for p in range(NA_KROWS // 2)]
                bias_rows.append(jnp.concatenate(tiles, axis=1))
        bias = jnp.concatenate(bias_rows, axis=0)
        segs = [(k_ref[pl.ds(start, NA_NKEY), cols], v_ref[pl.ds(start, NA_NKEY), cols]),
                (kc_ref[:, cols], vc_ref[:, cols])]
        o_ref[:, cols] = _pair_attend(q_ref[:, cols], segs, [bias, None]).astype(o_ref.dtype)


def _na_attention(q, k, v, table, n_batch):
    ctx0 = n_batch * LAT_BLKS
    n_steps = NA_ROWS // NA_QROWS
    tq = NA_QROWS * GRID_W
    lat = pl.BlockSpec((N_LAT, D), lambda b, j: (b, 0))
    ctx = pl.BlockSpec((N_CTX, D), lambda b, j: (ctx0 + b, 0))
    return pl.pallas_call(
        _na_kernel,
        grid=(n_batch, n_steps),
        in_specs=[
            pl.BlockSpec((tq, D), lambda b, j: (b * n_steps + j, 0)),
            lat, lat, ctx, ctx,
            pl.BlockSpec(table.shape, lambda b, j: (0, 0, 0, 0), pipeline_mode=pl.Buffered(1)),
        ],
        out_specs=pl.BlockSpec((tq, D), lambda b, j: (b * n_steps + j, 0)),
        out_shape=jax.ShapeDtypeStruct((n_batch * N_LAT, D), BF16),
        compiler_params=_cparams(2),
        name="na_attention",
    )(q, k, v, k, v, table)


def _ctx_attn_kernel(q_ref, k_ref, v_ref, o_ref):
    for hp in range(NA_HEADS // 2):
        cols = slice(hp * LANES, (hp + 1) * LANES)
        segs = [(k_ref[:, cols], v_ref[:, cols])]
        o_ref[:, cols] = _pair_attend(q_ref[:, cols], segs, [None]).astype(o_ref.dtype)


def _ctx_attention(q, k, v, n_batch):
    ctx0 = n_batch * LAT_BLKS
    spec = pl.BlockSpec((N_CTX, D), lambda b: (ctx0 + b, 0))
    return pl.pallas_call(
        _ctx_attn_kernel,
        grid=(n_batch,),
        in_specs=[spec, spec, spec],
        out_specs=pl.BlockSpec((N_CTX, D), lambda b: (b, 0)),
        out_shape=jax.ShapeDtypeStruct((n_batch * N_CTX, D), BF16),
        compiler_params=_cparams(1),
        name="na_ctx_attention",
    )(q, k, v)


def _na_mixer(h, mod3, g, w_kvq, rpb, n_batch):
    segs = [(0, D, 1.0, BF16), (D, 2 * D, 1.0, BF16), (2 * D, 3 * D, NA_HD ** -0.5 * LOG2E, BF16)]
    k, v, q = _proj(h, mod3, g, w_kvq.astype(BF16), segs, n_batch)
    return _na_attention(q, k, v, _na_bias_tables(rpb), n_batch), _ctx_attention(q, k, v, n_batch)


DA_TQ = 1024
DA_W = 2 * DA_HD


def _rope_tables():
    t = np.arange(N_LAT)
    row = (t // GRID_W).astype(np.float64)
    col = (t % GRID_W).astype(np.float64)
    per_axis = DA_HD // 2
    freqs = ROPE_BASE ** (-np.arange(0, per_axis, 2, dtype=np.float64) / per_axis)
    ar = row[:, None] * freqs
    ac = col[:, None] * freqs
    ang = np.concatenate([ar, ar, ac, ac], axis=-1)
    cos = np.tile(np.cos(ang), (1, 2))
    sin = np.tile(np.sin(ang), (1, 2))
    quarter = (np.arange(DA_W) % DA_HD) // (DA_HD // 4)
    even = (quarter % 2 == 0)[None, :]
    sin_a = np.where(even, -sin, 0.0)
    sin_b = np.where(even, 0.0, sin)
    return tuple(jnp.asarray(a, F32) for a in (cos, sin_a, sin_b))


def _rope(x, cos, sin_a, sin_b):
    q16 = DA_HD // 4
    return x * cos + pltpu.roll(x, DA_W - q16, 1) * sin_a + pltpu.roll(x, q16, 1) * sin_b


def _da_lambda(lp, lam_init):
    a = jnp.sum(lp[0:1] * lp[1:2], axis=-1, keepdims=True)
    b = jnp.sum(lp[2:3] * lp[3:4], axis=-1, keepdims=True)
    return jnp.exp(a) - jnp.exp(b) + lam_init


DA_KCHUNK = 512


def _diff_attend(q, segs, lam, ng, lam_init):
    lo = _lane_lo(q.shape)
    qm = [_half_masked(q, lo, 0), _half_masked(q, lo, 1)]
    chunks = [(k, v, c0, min(c0 + DA_KCHUNK, k.shape[0])) for k, v in segs for c0 in range(0, k.shape[0], DA_KCHUNK)]
    s = [[_dot_nt(qm[j], k[c0:c1, :]) for j in range(2)] for k, _, c0, c1 in chunks]
    m = [None, None]
    for sc in s:
        for j in range(2):
            mc = sc[j].max(axis=-1, keepdims=True)
            m[j] = mc if m[j] is None else jnp.maximum(m[j], mc)
    r = [None, None]
    for sc, (_, v, c0, c1) in zip(s, chunks):
        vc = v[c0:c1, :]
        vx = jnp.concatenate([vc, jnp.ones_like(vc)], axis=1)
        for j in range(2):
            pv = _dot(jnp.exp2(sc[j] - m[j]).astype(BF16), vx)
            r[j] = pv if r[j] is None else r[j] + pv
    o = r[0][:, :DA_W] / r[0][:, DA_W:] - lam * (r[1][:, :DA_W] / r[1][:, DA_W:])
    return o * lax.rsqrt(jnp.mean(o * o, axis=-1, keepdims=True) + EPS) * ng * (1.0 - lam_init)


def _da_kernel(lam_ref, q_ref, k_ref, v_ref, kc_ref, vc_ref, ng_ref, o_ref, *, lam_init):
    lam = _da_lambda(lam_ref[...], lam_init)
    segs = [(k_ref, v_ref), (kc_ref, vc_ref)]
    o_ref[...] = _diff_attend(q_ref[...], segs, lam, ng_ref[...], lam_init).astype(o_ref.dtype)


def _da_attention(q, k, v, lam_p, norm_g, lam_init, n_batch):
    nqb = N_LAT // DA_TQ
    ctx0 = n_batch * LAT_BLKS
    lat = pl.BlockSpec((N_LAT, DA_W), lambda b, h, i: (b, h))
    ctx = pl.BlockSpec((N_CTX, DA_W), lambda b, h, i: (ctx0 + b, h))
    return pl.pallas_call(
        functools.partial(_da_kernel, lam_init=lam_init),
        grid=(n_batch, DA_HEADS, nqb),
        in_specs=[
            pl.BlockSpec((4, DA_HD), lambda b, h, i: (0, 0)),
            pl.BlockSpec((DA_TQ, DA_W), lambda b, h, i: (b * nqb + i, h)),
            lat, lat, ctx, ctx,
            pl.BlockSpec((1, DA_W), lambda b, h, i: (0, h)),
        ],
        out_specs=pl.BlockSpec((DA_TQ, DA_W), lambda b, h, i: (b * nqb + i, h)),
        out_shape=jax.ShapeDtypeStruct((n_batch * N_LAT, D), BF16),
        compiler_params=_cparams(3),
        name="diff_attention",
    )(lam_p, q, k, v, k, v, norm_g.reshape(1, D))


def _da_ctx_kernel(lam_ref, q_ref, k_ref, v_ref, ng_ref, o_ref, *, lam_init):
    lam = _da_lambda(lam_ref[...], lam_init)
    for h in range(DA_HEADS):
        cols = slice(h * DA_W, (h + 1) * DA_W)
        o = _diff_attend(q_ref[:, cols], [(k_ref[:, cols], v_ref[:, cols])], lam, ng_ref[:, cols], lam_init)
        o_ref[:, cols] = o.astype(o_ref.dtype)


def _da_ctx_attention(q, k, v, lam_p, norm_g, lam_init, n_batch):
    ctx0 = n_batch * LAT_BLKS
    spec = pl.BlockSpec((N_CTX, D), lambda b: (ctx0 + b, 0))
    return pl.pallas_call(
        functools.partial(_da_ctx_kernel, lam_init=lam_init),
        grid=(n_batch,),
        in_specs=[pl.BlockSpec((4, DA_HD), lambda b: (0, 0)), spec, spec, spec, pl.BlockSpec((1, D), lambda b: (0, 0))],
        out_specs=pl.BlockSpec((N_CTX, D), lambda b: (b, 0)),
        out_shape=jax.ShapeDtypeStruct((n_batch * N_CTX, D), BF16),
        compiler_params=_cparams(1),
        name="diff_ctx_attention",
    )(lam_p, q, k, v, norm_g.reshape(1, D))


def _da_mixer(h, mod3, g, w_kvq, lam_p, norm_g, layer_idx, n_batch):
    lam_init = 0.8 - 0.6 * float(np.exp(-0.3 * layer_idx))
    segs = [(0, D, 1.0, BF16, True), (D, 2 * D, 1.0, BF16), (2 * D, 3 * D, DA_HD ** -0.5 * LOG2E, BF16, True)]
    k, v, q = _proj(h, mod3, g, w_kvq.astype(BF16), segs, n_batch, rope_tabs=_rope_tables())
    lam_p = lam_p.astype(F32)
    return (_da_attention(q, k, v, lam_p, norm_g, lam_init, n_batch),
            _da_ctx_attention(q, k, v, lam_p, norm_g, lam_init, n_batch))


SCAN_STEPS = 1 + LAT_BLKS


def _scan_row_blk(direction, n_batch):
    ctx0 = n_batch * LAT_BLKS

    def blk(b, s):
        lat = b * LAT_BLKS + (s - 1 if direction == 0 else LAT_BLKS - s)
        return jnp.where(s == 0, ctx0 + b, lat)
    return blk


def _scan_spec(width, direction, n_batch):
    blk = _scan_row_blk(direction, n_batch)
    return pl.BlockSpec((ROW_BLK, width), lambda b, h, s: (blk(b, s), h))


def _scan_spec_all(width, direction, n_batch):
    blk = _scan_row_blk(direction, n_batch)
    return pl.BlockSpec((ROW_BLK, width), lambda b, h, s: (blk(b, s), 0))


def _scan_tri():
    u = np.arange(ROW_BLK)
    tt, uu = np.meshgrid(u, u, indexing="ij")
    return jnp.asarray(np.stack([uu <= tt, uu >= tt]).astype(np.float32), BF16)


def _split3(x):
    hi = x.astype(BF16)
    r1 = x - hi.astype(F32)
    mid = r1.astype(BF16)
    lo = (r1 - mid.astype(F32)).astype(BF16)
    return hi, mid, lo


def _scan_out_specs(width):
    return [pl.BlockSpec((N_LAT, width), lambda b, h, s: (b, h)), pl.BlockSpec((N_CTX, width), lambda b, h, s: (b, h))]


def _scan_write_sum(step, out_fwd, out_bwd, lat_ref, ctx_ref):
    @pl.when(step == 0)
    def _():
        ctx_ref[...] = out_fwd + out_bwd

    rows_f = pl.ds(pl.multiple_of((step - 1) * ROW_BLK, ROW_BLK), ROW_BLK)
    rows_b = pl.ds(pl.multiple_of((LAT_BLKS - step) * ROW_BLK, ROW_BLK), ROW_BLK)

    @pl.when(jnp.logical_and(step >= 1, step <= LAT_BLKS // 2))
    def _():
        lat_ref[rows_f, :] = out_fwd
        lat_ref[rows_b, :] = out_bwd

    @pl.when(step > LAT_BLKS // 2)
    def _():
        lat_ref[rows_f, :] += out_fwd
        lat_ref[rows_b, :] += out_bwd


def _conv_silu(x_ref, prev_ref, next_ref, w_ref, first, last, scale):
    x = x_ref[...]
    rid = lax.broadcasted_iota(jnp.int32, x.shape, 0)
    prev_row = jnp.where(first, 0.0, prev_ref[7:8, :])
    next_row = jnp.where(last, 0.0, next_ref[0:1, :])
    x_prev = jnp.where(rid == 0, prev_row, pltpu.roll(x, 1, 0))
    x_next = jnp.where(rid == ROW_BLK - 1, next_row, pltpu.roll(x, ROW_BLK - 1, 0))
    y = w_ref[0:1, :] * x_prev + w_ref[1:2, :] * x + w_ref[2:3, :] * x_next
    return (_silu(y) * scale).astype(BF16)


def _ml_conv_kernel(xk_ref, xkp_ref, xkn_ref, xq_ref, xqp_ref, xqn_ref, wk_ref, wq_ref, ok_ref, oq_ref, *, n_lat_blks):
    i = pl.program_id(0)
    is_ctx = i >= n_lat_blks
    first = jnp.logical_or(is_ctx, i % LAT_BLKS == 0)
    last = jnp.logical_or(is_ctx, i % LAT_BLKS == LAT_BLKS - 1)
    ok_ref[...] = _conv_silu(xk_ref, xkp_ref, xkn_ref, wk_ref, first, last, ML_HD ** -0.5)
    oq_ref[...] = _conv_silu(xq_ref, xqp_ref, xqn_ref, wq_ref, first, last, 1.0)


def _ml_conv(pk, pq, conv_w, n_batch):
    n_rows = pk.shape[0]
    n_blk = n_rows // ROW_BLK
    sub = ROW_BLK // 8
    main = pl.BlockSpec((ROW_BLK, D), lambda i: (i, 0))
    prev = pl.BlockSpec((8, D), lambda i: (jnp.maximum(i * sub - 1, 0), 0))
    nxt = pl.BlockSpec((8, D), lambda i: (jnp.minimum((i + 1) * sub, n_blk * sub - 1), 0))
    return pl.pallas_call(
        functools.partial(_ml_conv_kernel, n_lat_blks=n_batch * LAT_BLKS),
        grid=(n_blk,),
        in_specs=[main, prev, nxt, main, prev, nxt,
                  pl.BlockSpec((3, D), lambda i: (0, 0)), pl.BlockSpec((3, D), lambda i: (0, 1))],
        out_specs=[main, main],
        out_shape=[jax.ShapeDtypeStruct((n_rows, D), BF16)] * 2,
        compiler_params=_cparams(1),
        name="mlstm_conv",
    )(pk, pk, pk, pq, pq, pq, conv_w, conv_w)


def _row_sum(x):
    return jnp.sum(x[:, :LANES] + x[:, LANES:], axis=1, keepdims=True)


def _pick_lane(x, idx):
    lane = lax.broadcasted_iota(jnp.int32, x.shape, 1)
    return jnp.sum(jnp.where(lane == idx, x, 0.0), axis=1, keepdims=True)


def _pick_sublane(x, idx):
    sub = lax.broadcasted_iota(jnp.int32, x.shape, 0)
    return jnp.sum(jnp.where(sub == idx, x, 0.0), axis=0, keepdims=True)


def _ml_scan_kernel(gb_ref, g0_ref, q0_ref, k0_ref, v0_ref, g1_ref, q1_ref, k1_ref, v1_ref,
                    lat_ref, ctx_ref, c_ref, n_ref, m_ref):
    head = pl.program_id(1)

    @pl.when(pl.program_id(2) == 0)
    def _():
        c_ref[...] = jnp.zeros_like(c_ref)
        n_ref[...] = jnp.zeros_like(n_ref)
        m_ref[...] = jnp.zeros_like(m_ref)

    r = lax.broadcasted_iota(jnp.int32, (ROW_BLK, ROW_BLK), 0)
    c = lax.broadcasted_iota(jnp.int32, (ROW_BLK, ROW_BLK), 1)
    outs = [None, None]

    def direction(d, g_ref, q_ref, k_ref, v_ref):
        valid = (r >= c) if d == 0 else (r <= c)
        other = (r <= c) if d == 0 else (r >= c)
        g = g_ref[...] + gb_ref[...]
        gt = g.T
        ii = d * 2 * ML_HEADS + head
        fi = ii + ML_HEADS
        m_prev = m_ref[d][0:1, 0:1]
        n_prev = n_ref[d][0:1, :]
        c_prev = c_ref[d]
        qc = q_ref[...]
        kc = k_ref[...]
        vc = v_ref[...]
        qk = _dot_nt(qc, kc)
        q_c = _dot(qc, c_prev.astype(BF16))
        q_n = _row_sum(qc.astype(F32) * n_prev)
        yield
        i_col = _pick_lane(g, ii)
        f_col = _log_sigmoid(_pick_lane(g, fi))
        i_row = _pick_sublane(gt, ii)
        f_row = _log_sigmoid(_pick_sublane(gt, fi))
        yield
        b_col = _row_sum(jnp.where(valid, f_row, 0.0))
        b_row = jnp.sum(jnp.where(other, f_col, 0.0), axis=0, keepdims=True)
        total = jnp.sum(f_col, axis=0, keepdims=True)
        yield
        dmat = jnp.where(valid, b_col - b_row + i_row, NEG)
        inter = b_col + m_prev
        m_t = jnp.maximum(inter, jnp.maximum(dmat[:, :LANES], dmat[:, LANES:]).max(axis=1, keepdims=True))
        w_inter = jnp.exp(inter - m_t)
        g_col = total - b_col + i_col
        m_new = jnp.maximum(total + m_prev, g_col.max(axis=0, keepdims=True))
        w_old = jnp.exp(total + m_prev - m_new)
        kw = kc.astype(F32) * jnp.exp(g_col - m_new)
        upd = _dot_tn(kw.astype(BF16), vc)
        yield
        s = qk * jnp.exp(dmat - m_t)
        num = w_inter * q_c + _dot(s.astype(BF16), vc)
        den = w_inter * q_n + _row_sum(s)
        outs[d] = num / jnp.maximum(jnp.abs(den), jnp.exp(-m_t))
        yield
        c_ref[d] = w_old * c_prev + upd
        n_ref[d] = jnp.broadcast_to(w_old * n_prev + kw.sum(axis=0, keepdims=True), n_ref.shape[1:])
        m_ref[d] = jnp.broadcast_to(m_new, m_ref.shape[1:])

    stages = [direction(0, g0_ref, q0_ref, k0_ref, v0_ref), direction(1, g1_ref, q1_ref, k1_ref, v1_ref)]
    while stages:
        stages = [gen for gen in stages if next(gen, StopIteration) is not StopIteration]
    _scan_write_sum(pl.program_id(2), outs[0], outs[1], lat_ref, ctx_ref)


def _ml_scan(q, k, v, gates, gate_b, n_batch):
    in_specs = [pl.BlockSpec((1, LANES), lambda b, h, s: (0, 0))]
    for d in range(2):
        in_specs += [_scan_spec_all(LANES, d, n_batch)] + [_scan_spec(ML_HD, d, n_batch)] * 3
    return pl.pallas_call(
        _ml_scan_kernel,
        grid=(n_batch, ML_HEADS, SCAN_STEPS),
        in_specs=in_specs,
        out_specs=_scan_out_specs(ML_HD),
        out_shape=[jax.ShapeDtypeStruct((n_batch * N_LAT, D), F32), jax.ShapeDtypeStruct((n_batch * N_CTX, D), F32)],
        scratch_shapes=[pltpu.VMEM((2, ML_HD, ML_HD), F32), pltpu.VMEM((2, 8, ML_HD), F32),
                        pltpu.VMEM((2, 8, LANES), F32)],
        compiler_params=_cparams(3),
        name="mlstm_scan",
    )(gate_b, gates, q, k, v, gates, q, k, v)


def _ml_mixer(h, mod3, g, w_in, gate_b, conv_w, n_batch):
    n_g = 4 * ML_HEADS
    w_in = w_in.astype(BF16)
    wk, wv, wg, wq, wo = (w_in[:, :D], w_in[:, D:2 * D], w_in[:, 2 * D:2 * D + n_g],
                          w_in[:, 2 * D + n_g:3 * D + n_g], w_in[:, 3 * D + n_g:])
    w = jnp.concatenate([wk, wv, wq, wo, jnp.pad(wg, ((0, 0), (0, LANES - n_g)))], axis=1)
    segs = [(0, D, 1.0, F32), (D, 2 * D, 1.0, BF16), (2 * D, 3 * D, 1.0, F32), (3 * D, 4 * D, 1.0, F32),
            (4 * D, 4 * D + LANES, 1.0, F32)]
    pk, v, pq, og, gates = _proj(h, mod3, g, w, segs, n_batch)
    k, q = _ml_conv(pk, pq, conv_w.astype(F32), n_batch)
    gb = jnp.pad(gate_b.astype(F32), (0, LANES - n_g)).reshape(1, LANES)
    h_lat, h_ctx = _ml_scan(q, k, v, gates, gb, n_batch)
    return h_lat, h_ctx, og


GLA_LEVELS = (256, 128, 64, 32, 16, 8, 4, 2)
GLA_BCAST_LEVELS = 6
GLA_DIAG = len(GLA_LEVELS)


def _gla_level_ids():
    u = np.arange(ROW_BLK // 2)
    tt, ss = np.meshgrid(u, u, indexing="ij")
    hb = np.floor(np.log2(np.maximum(tt ^ ss, 1))).astype(np.int64)
    level = len(GLA_LEVELS) - 1 - hb
    lid = np.stack([np.where(tt == ss, GLA_DIAG, np.where(tt > ss, level, -1)),
                    np.where(tt == ss, GLA_DIAG, np.where(tt < ss, level, -1))]).astype(np.int32)
    return jnp.asarray(np.concatenate([lid, lid], axis=1))


def _gla_level_exponents(level, d, bc_ref, la_ref, row):
    n = GLA_LEVELS[level]
    half = n // 2
    if level < GLA_BCAST_LEVELS:
        parts = []
        for j in range(ROW_BLK // n):
            mid = j * n + (half - 1 if d == 0 else half)
            parts.append(jnp.broadcast_to(bc_ref[mid:mid + 1, :], (n, GLA_DK)))
        e = -jnp.abs(bc_ref[...] - (parts[0] if len(parts) == 1 else jnp.concatenate(parts, axis=0)))
        return e, e
    o = row & (n - 1)
    la = la_ref[...]
    if n == 4:
        la_prev = pltpu.roll(la, 1, 0)
        la_next = pltpu.roll(la, ROW_BLK - 1, 0)
        if d == 0:
            return (jnp.where(o == 2, la, jnp.where(o == 3, la + la_prev, 0.0)), jnp.where(o == 0, la_next, 0.0))
        return (jnp.where(o == 0, la + la_next, jnp.where(o == 1, la, 0.0)), jnp.where(o == 3, la_prev, 0.0))
    return jnp.where(o == (1 if d == 0 else 0), la, 0.0), None


def _gla_kernel(tri_ref, lid_ref, wup_ref, bg_ref, q0_ref, k0_ref, v0_ref, l0_ref, q1_ref, k1_ref, v1_ref, l1_ref,
                lat_ref, ctx_ref, st_ref, bc_ref, att_ref, la_ref):
    @pl.when(pl.program_id(2) == 0)
    def _():
        st_ref[...] = jnp.zeros_like(st_ref)

    row = lax.broadcasted_iota(jnp.int32, (ROW_BLK, GLA_DK), 0)
    outs = [None, None]

    def direction(d, q_ref, k_ref, v_ref, low_ref):
        pre = _dot(low_ref[...].astype(BF16), wup_ref[d]) + bg_ref[d]
        la = _log_sigmoid(pre) * (LOG2E / GLA_TAU)
        hi, mid, lo = _split3(la)
        tri = tri_ref[d]
        bc_ref[d] = _dot(tri, hi) + _dot(tri, mid) + _dot(tri, lo)
        la_ref[d] = la
        half = ROW_BLK // 2

        def diag_blocks(a, b):
            return jnp.concatenate([_dot_nt(a[:half], b[:half]), _dot_nt(a[half:], b[half:])], axis=0)

        att_ref[d] = jnp.where(lid_ref[d] == GLA_DIAG,
                               diag_blocks(q_ref[...].astype(BF16), k_ref[...].astype(BF16)), 0.0)
        yield
        eq, _ = _gla_level_exponents(0, d, bc_ref.at[d], la_ref.at[d], row)
        q_rows, k_rows = (slice(half, None), slice(None, half)) if d == 0 else (slice(None, half), slice(half, None))
        f = jnp.exp2(eq)
        off = _dot_nt((q_ref[q_rows, :] * f[q_rows]).astype(BF16), (k_ref[k_rows, :] * f[k_rows]).astype(BF16))
        yield
        for level in range(1, len(GLA_LEVELS)):
            eq, ek = _gla_level_exponents(level, d, bc_ref.at[d], la_ref.at[d], row)
            fq = jnp.exp2(eq)
            ql = (q_ref[...] * fq).astype(BF16)
            k = k_ref[...]
            kl = k.astype(BF16) if ek is None else (k * (fq if ek is eq else jnp.exp2(ek))).astype(BF16)
            att_ref[d] = jnp.where(lid_ref[d] == level, diag_blocks(ql, kl), att_ref[d])
            yield
        st = st_ref[d]
        bc = bc_ref[d]
        k = k_ref[...]
        v = v_ref[...]
        qd = (q_ref[...] * jnp.exp2(bc)).astype(BF16)
        diag = att_ref[d]
        if d == 0:
            o_top = _dot(diag[:half].astype(BF16), v[:half])
            o_bot = _dot(jnp.concatenate([off, diag[half:]], axis=1).astype(BF16), v)
        else:
            o_top = _dot(jnp.concatenate([diag[:half], off], axis=1).astype(BF16), v)
            o_bot = _dot(diag[half:].astype(BF16), v[half:])
        outs[d] = jnp.concatenate([o_top, o_bot], axis=0) + _dot_nt(qd, st.astype(BF16))
        b_end = bc[ROW_BLK - 1:ROW_BLK] if d == 0 else bc[0:1]
        kd = (k * jnp.exp2(b_end - bc)).astype(BF16)
        st_ref[d] = st * jnp.exp2(b_end) + _dot_tn(v, kd)

    stages = [direction(0, q0_ref, k0_ref, v0_ref, l0_ref), direction(1, q1_ref, k1_ref, v1_ref, l1_ref)]
    while stages:
        stages = [gen for gen in stages if next(gen, StopIteration) is not StopIteration]
    _scan_write_sum(pl.program_id(2), outs[0], outs[1], lat_ref, ctx_ref)


def _gla_scan(q, k, v, low, wup, bg, n_batch):
    tri, lid = _scan_tri(), _gla_level_ids()
    const3 = lambda b, h, s: (0, 0, 0)
    in_specs = [
        pl.BlockSpec(tri.shape, const3),
        pl.BlockSpec(lid.shape, const3),
        pl.BlockSpec((2, LANES, GLA_DK), lambda b, h, s: (0, 0, h)),
        pl.BlockSpec((2, 1, GLA_DK), lambda b, h, s: (0, 0, h)),
    ]
    for d in range(2):
        in_specs += [_scan_spec(GLA_DK, d, n_batch), _scan_spec(GLA_DK, d, n_batch),
                     _scan_spec(GLA_DV, d, n_batch), _scan_spec_all(LANES, d, n_batch)]
    return pl.pallas_call(
        _gla_kernel,
        grid=(n_batch, GLA_HEADS, SCAN_STEPS),
        in_specs=in_specs,
        out_specs=_scan_out_specs(GLA_DV),
        out_shape=[jax.ShapeDtypeStruct((n_batch * N_LAT, D), F32), jax.ShapeDtypeStruct((n_batch * N_CTX, D), F32)],
        scratch_shapes=[pltpu.VMEM((2, GLA_DV, GLA_DK), F32), pltpu.VMEM((2, ROW_BLK, GLA_DK), F32),
                        pltpu.VMEM((2, ROW_BLK, ROW_BLK // 2), F32), pltpu.VMEM((2, ROW_BLK, GLA_DK), F32)],
        compiler_params=_cparams(3),
        name="gla_scan",
    )(tri, lid, wup, bg, q, k, v, low, q, k, v, low)


def _gla_mixer(h, mod3, g, w_in, w_gate_up, b_gate, n_batch):
    dk_t = GLA_HEADS * GLA_DK
    n_low = 2 * GLA_RANK
    w_in = w_in.astype(BF16)
    wk, wv, wl, wq, wr = (w_in[:, :dk_t], w_in[:, dk_t:dk_t + D], w_in[:, dk_t + D:dk_t + D + n_low],
                          w_in[:, dk_t + D + n_low:2 * dk_t + D + n_low], w_in[:, 2 * dk_t + D + n_low:])
    w = jnp.concatenate([wk, wq, wv, wr, jnp.pad(wl, ((0, 0), (0, LANES - n_low)))], axis=1)
    c = [0, dk_t, 2 * dk_t, 2 * dk_t + D, 2 * dk_t + 2 * D, 2 * dk_t + 2 * D + LANES]
    segs = [(c[0], c[1], 1.0, F32), (c[1], c[2], GLA_DK ** -0.5, F32), (c[2], c[3], 1.0, BF16),
            (c[3], c[4], 1.0, F32), (c[4], c[5], 1.0, F32)]
    k, q, v, rg, low = _proj(h, mod3, g, w, segs, n_batch)
    wup = jnp.zeros((2, LANES, dk_t), F32)
    for r in range(2):
        wup = wup.at[r, r * GLA_RANK:(r + 1) * GLA_RANK].set(w_gate_up[r].astype(F32))
    o_lat, o_ctx = _gla_scan(q, k, v, low, wup.astype(BF16), b_gate.astype(F32).reshape(2, 1, dk_t), n_batch)
    return o_lat, o_ctx, rg


def kernel(x, c, ctx, c_ctx, ada_w, ada_b, norm_g, ffn_w13, ffn_w2, final_g, na_w_kvq, na_rpb, na_w_o, ml_w_in, ml_gate_b, ml_conv_w, ml_norm_g, ml_w_o, da_w_kvq, da_lam, da_norm_g, da_w_o, gla_w_in, gla_w_gate_up, gla_b_gate, gla_norm_g, gla_w_o):
    nb = x.shape[0]
    assert x.shape[1:] == (N_LAT, D) and ctx.shape[1:] == (N_CTX, D) and nb < MOD_ROWS
    n_lat = nb * N_LAT
    n_tok = n_lat + nb * N_CTX

    s = jnp.zeros((MOD_ROWS, D), F32).at[:nb].set(c).at[nb].set(c_ctx)
    modtab = _mod_tables(s, ada_w, ada_b)
    h = (x.reshape(n_lat, D), ctx.reshape(nb * N_CTX, D))
    w13 = ffn_w13.astype(BF16)
    w2 = ffn_w2.astype(BF16)

    for i in range(DEPTH):
        kind, j = i % 4, i // 4
        last = i == DEPTH - 1
        mod3 = modtab[i].reshape(MOD_ROWS * N_MOD, 1, D)
        h = _ffn(h, mod3, norm_g[i, 0], w13, w2, (i, 0), final_g, 0, n_tok, nb)
        n_out = n_lat if last else n_tok
        if kind == 0:
            ins = _na_mixer(h, mod3, norm_g[i, 1], na_w_kvq[j], na_rpb[j], nb)
            mixer, w_o, mixer_g = "plain", na_w_o[j], None
        elif kind == 1:
            ins = _ml_mixer(h, mod3, norm_g[i, 1], ml_w_in[j], ml_gate_b[j], ml_conv_w[j], nb)
            mixer, w_o, mixer_g = "ml", ml_w_o[j], ml_norm_g[j]
        elif kind == 2:
            ins = _da_mixer(h, mod3, norm_g[i, 1], da_w_kvq[j], da_lam[j], da_norm_g[j], i, nb)
            mixer, w_o, mixer_g = "plain", da_w_o[j], None
        else:
            ins = _gla_mixer(h, mod3, norm_g[i, 1], gla_w_in[j], gla_w_gate_up[j], gla_b_gate[j], nb)
            mixer, w_o, mixer_g = "gla", gla_w_o[j], gla_norm_g[j]
        h = _ffn(h, mod3, norm_g[i, 2], w13, w2, (i, 1), final_g, 6, n_out, nb, final=last, mixer=mixer, mixer_ins=ins, w_o=w_o.astype(BF16), mixer_g=mixer_g)
    return h.reshape(nb, N_LAT, D)
```

```python
import functools

import numpy as np
import jax
import jax.numpy as jnp
from jax import lax
from jax.experimental import pallas as pl
from jax.experimental.pallas import tpu as pltpu

F32 = jnp.float32
BF16 = jnp.bfloat16

D = 1024
N_LAT = 2048
N_CTX = 256
DEPTH = 4
N_MOD = 9
D_FF = 2816
EPS = 1e-6
NEG = -1e30
LOG2E = 1.4426950408889634
GRID_W = 64

NA_HEADS = 16
NA_HD = 64
NA_WIN_R = 8
NA_WIN_C = 16

ML_HEADS = 4
ML_HD = 256
ML_HPS = 4

DA_HEADS = 8
DA_HD = 64
ROPE_BASE = 10000.0

GLA_HEADS = 4
GLA_DK = 128
GLA_DV = 256
GLA_RANK = 16
GLA_TAU = 16.0

ROW_BLK = 256
LAT_BLKS = N_LAT // ROW_BLK
MOD_ROWS = 16
LANES = 128
VMEM_LIMIT = 56 * 1024 * 1024


def _cparams(n_axes, vmem=VMEM_LIMIT):
    return pltpu.CompilerParams(dimension_semantics=("arbitrary",) * n_axes, vmem_limit_bytes=vmem)


def _sigmoid(x):
    return 1.0 / (1.0 + jnp.exp(-x))


def _silu(x):
    return x * _sigmoid(x)


def _log_sigmoid(x):
    return jnp.minimum(x, 0.0) - jnp.log(1.0 + jnp.exp(-jnp.abs(x)))


def _rms(x, g):
    return x * lax.rsqrt(jnp.mean(x * x, axis=-1, keepdims=True) + EPS) * g


def _dot(a, b):
    return jnp.dot(a, b, preferred_element_type=F32)


def _dot_nt(a, b):
    return lax.dot_general(a, b, (((1,), (1,)), ((), ())), preferred_element_type=F32)


def _dot_tn(a, b):
    return lax.dot_general(a, b, (((0,), (0,)), ((), ())), preferred_element_type=F32)


def _mod_row(i, tm, n_batch):
    return jnp.minimum(i // (N_LAT // tm), n_batch)


def _mod_spec(k, tm, n_batch):
    return pl.BlockSpec((1, 1, D), lambda i: (_mod_row(i, tm, n_batch) * N_MOD + k, 0, 0))


def _mod_kernel(s_ref, w_ref, b_ref, o_ref):
    a = _silu(s_ref[...]).astype(BF16)
    o_ref[0] = _dot(a, w_ref[0].astype(BF16)) + b_ref[0]


def _mod_tables(s, ada_w, ada_b):
    return pl.pallas_call(
        _mod_kernel,
        grid=(DEPTH, N_MOD),
        in_specs=[
            pl.BlockSpec((MOD_ROWS, D), lambda i, k: (0, 0)),
            pl.BlockSpec((1, D, D), lambda i, k: (i, 0, k)),
            pl.BlockSpec((1, 1, D), lambda i, k: (i, 0, k)),
        ],
        out_specs=pl.BlockSpec((1, MOD_ROWS, D), lambda i, k: (i, 0, k)),
        out_shape=jax.ShapeDtypeStruct((DEPTH, MOD_ROWS, N_MOD * D), F32),
        compiler_params=_cparams(2),
        name="mod_tables",
    )(s, ada_w, ada_b.reshape(DEPTH, 1, N_MOD * D))


FFN_TM = 512
MXU_TILE = 256
FFN_SPLIT = (D_FF // MXU_TILE + 1) // 2 * MXU_TILE
FFN_CHUNKS = ((0, FFN_SPLIT), (FFN_SPLIT, D_FF))


def _head_rms(x, g, n_heads):
    hd = D // n_heads
    parts = []
    for i in range(n_heads):
        xi = x[:, i * hd:(i + 1) * hd]
        parts.append(xi * lax.rsqrt(jnp.mean(xi * xi, axis=-1, keepdims=True) + EPS))
    return jnp.concatenate(parts, axis=-1) * g


def _mixer_readout(mode, refs, is_lat):
    y = jnp.where(is_lat, refs[0][...], refs[1][...])
    if mode == "ml":
        return _sigmoid(refs[2][...]) * _head_rms(y, refs[3][...], ML_HEADS)
    if mode == "gla":
        return _head_rms(y, refs[3][...], GLA_HEADS) * _silu(refs[2][...])
    return y


def _ffn_kernel(g_ref, sh_ref, sc_ref, gt_ref, w13_ref, w2_ref, fg_ref, *refs, final, mixer, split_h, n_lat_tiles):
    o_ref = refs[-1]
    if split_h:
        x = jnp.where(pl.program_id(0) < n_lat_tiles, refs[0][...], refs[1][...])
        refs = refs[2:]
    else:
        x = refs[0][...]
        refs = refs[1:]
    if mixer is not None:
        mg_ref, wo_ref = refs[:2]
        y = _mixer_readout(mixer, refs[2:-1], pl.program_id(0) < n_lat_tiles)
        x = x + mg_ref[0] * _dot(y.astype(BF16), wo_ref[...])
    a = (_rms(x, g_ref[...]) * (1.0 + sc_ref[0]) + sh_ref[0]).astype(BF16)
    acc = jnp.zeros(x.shape, F32)
    for c0, c1 in FFN_CHUNKS:
        gate = _dot(a, w13_ref[:, c0:c1])
        up = _dot(a, w13_ref[:, D_FF + c0:D_FF + c1])
        act = (_silu(gate) * up).astype(BF16)
        acc = acc + _dot(act, w2_ref[c0:c1, :])
    y = x + 0.5 * gt_ref[0] * acc
    if final:
        y = _rms(y, fg_ref[...])
    o_ref[...] = y


def _ffn(h, mod3, g, w13, w2, widx, final_g, k0, n_rows, n_batch, final=False, mixer=None, mixer_ins=(), w_o=None,
         mixer_g=None):
    tm = FFN_TM
    nl = n_batch * N_LAT // tm
    const = lambda i: (0, 0)
    row = pl.BlockSpec((tm, D), lambda i: (i, 0))
    lat_ctx = [pl.BlockSpec((tm, D), lambda i: (jnp.minimum(i, nl - 1), 0)),
               pl.BlockSpec((tm, D), lambda i: (jnp.maximum(i - nl, 0), 0))]
    split_h = isinstance(h, tuple)
    in_specs = [
        pl.BlockSpec((1, D), const),
        _mod_spec(k0, tm, n_batch),
        _mod_spec(k0 + 1, tm, n_batch),
        _mod_spec(k0 + 2, tm, n_batch),
        pl.BlockSpec((None, None, D, 2 * D_FF), lambda i: (*widx, 0, 0), pipeline_mode=pl.Buffered(1)),
        pl.BlockSpec((None, None, D_FF, D), lambda i: (*widx, 0, 0), pipeline_mode=pl.Buffered(1)),
        pl.BlockSpec((1, D), const),
    ] + (lat_ctx if split_h else [row])
    args = [g.reshape(1, D), mod3, mod3, mod3, w13, w2, final_g.reshape(1, D)] + (list(h) if split_h else [h])
    if mixer is not None:
        in_specs += [_mod_spec(5, tm, n_batch), pl.BlockSpec((D, D), const, pipeline_mode=pl.Buffered(1))] + lat_ctx
        args += [mod3, w_o, mixer_ins[0], mixer_ins[1]]
        if mixer != "plain":
            in_specs += [row, pl.BlockSpec((1, D), const)]
            args += [mixer_ins[2], mixer_g.reshape(1, D)]
    return pl.pallas_call(
        functools.partial(_ffn_kernel, final=final, mixer=mixer, split_h=split_h, n_lat_tiles=nl),
        grid=(n_rows // tm,),
        in_specs=in_specs,
        out_specs=row,
        out_shape=jax.ShapeDtypeStruct((n_rows, D), F32),
        compiler_params=_cparams(1),
        name="ffn" if mixer is None else "mixer_out_ffn",
    )(*args)


PROJ_TM = 512


def _proj_kernel(h_ref, g_ref, sh_ref, sc_ref, w_ref, *refs, segs, n_rope):
    rope_refs, o_refs = refs[:n_rope], refs[n_rope:]
    x = h_ref[...]
    a = (_rms(x, g_ref[...]) * (1.0 + sc_ref[0]) + sh_ref[0]).astype(BF16)
    for o_ref, (c0, c1, scale, rope) in zip(o_refs, segs):
        y = _dot(a, w_ref[:, c0:c1])
        if scale != 1.0:
            y = y * scale
        if rope:
            tabs = [r[...] for r in rope_refs]
            y = jnp.concatenate([_rope(y[:, c:c + LANES], *tabs) for c in range(0, c1 - c0, LANES)], axis=1)
        o_ref[...] = y.astype(o_ref.dtype)


def _proj(h, mod3, g, w, segs, n_batch, rope_tabs=()):
    tm = PROJ_TM
    n_rows = h.shape[0]
    const = lambda i: (0, 0)
    n_pos = N_LAT // tm
    nl = n_batch * n_pos
    rope_spec = pl.BlockSpec((tm, LANES), lambda i: (jnp.where(i < nl, i % n_pos, n_pos), 0))
    identity = (jnp.ones((tm, LANES), F32), jnp.zeros((tm, LANES), F32), jnp.zeros((tm, LANES), F32))
    rope_args = [jnp.concatenate([t, e], axis=0) for t, e in zip(rope_tabs, identity)]
    return pl.pallas_call(
        functools.partial(_proj_kernel, segs=tuple((s[0], s[1], s[2], len(s) > 4 and s[4]) for s in segs),
                          n_rope=len(rope_args)),
        grid=(n_rows // tm,),
        in_specs=[
            pl.BlockSpec((tm, D), lambda i: (i, 0)),
            pl.BlockSpec((1, D), const),
            _mod_spec(3, tm, n_batch),
            _mod_spec(4, tm, n_batch),
            pl.BlockSpec(w.shape, const, pipeline_mode=pl.Buffered(1)),
        ] + [rope_spec] * len(rope_args),
        out_specs=[pl.BlockSpec((tm, s[1] - s[0]), lambda i: (i, 0)) for s in segs],
        out_shape=[jax.ShapeDtypeStruct((n_rows, s[1] - s[0]), s[3]) for s in segs],
        compiler_params=_cparams(1),
        name="mixer_proj",
    )(h, g.reshape(1, D), mod3, mod3, w, *rope_args)


def _lane_lo(shape):
    return lax.broadcasted_iota(jnp.int32, shape, 1) < (LANES // 2)


def _half_masked(x, lo, head):
    return jnp.where(lo if head == 0 else jnp.logical_not(lo), x, jnp.zeros_like(x))


def _pair_attend(qp, segs, biases):
    tq = qp.shape[0]
    lo = _lane_lo(qp.shape)
    q2 = jnp.concatenate([_half_masked(qp, lo, 0), _half_masked(qp, lo, 1)], axis=0)
    scores = []
    for (k, _), b in zip(segs, biases):
        s = _dot_nt(q2, k)
        scores.append(s if b is None else s + b)
    m = scores[0].max(axis=-1, keepdims=True)
    for s in scores[1:]:
        m = jnp.maximum(m, s.max(axis=-1, keepdims=True))
    acc = None
    for s, (_, v) in zip(scores, segs):
        vx = jnp.concatenate([v, jnp.ones_like(v)], axis=1)
        pv = _dot(jnp.exp2(s - m).astype(BF16), vx)
        acc = pv if acc is None else acc + pv
    o2 = acc[:, :LANES] / acc[:, LANES:]
    return jnp.where(lo, o2[:tq], o2[tq:])


NA_ROWS = N_LAT // GRID_W
NA_QROWS = 4
NA_KROWS = 12
NA_NKEY = NA_KROWS * GRID_W
NA_NOFF = 2 * NA_WIN_R - 1
NA_NENT = NA_NOFF + 1


NA_NCOFF = 2 * NA_WIN_C - 1


def _na_table_kernel(rpb_ref, o_ref):
    h = pl.program_id(0)
    c = lax.broadcasted_iota(jnp.int32, (GRID_W, LANES), 0)
    lane = lax.broadcasted_iota(jnp.int32, (GRID_W, LANES), 1)
    kc = lane & (GRID_W - 1)
    second = lane >= GRID_W
    col_off = jnp.clip(kc - c, 1 - NA_WIN_C, NA_WIN_C - 1) + NA_WIN_C - 1
    win0 = jnp.clip(c - NA_WIN_C // 2, 0, GRID_W - NA_WIN_C)
    in_win = jnp.logical_and(kc >= win0, kc < win0 + NA_WIN_C)
    for e in range(NA_NENT):
        acc = jnp.zeros((GRID_W, LANES), F32)
        for o in range(NA_NCOFF):
            first_v = rpb_ref[(h * NA_NOFF + e - 1) * NA_NCOFF + o] if e >= 1 else 0.0
            second_v = rpb_ref[(h * NA_NOFF + e) * NA_NCOFF + o] if e < NA_NOFF else 0.0
            acc = jnp.where(col_off == o, jnp.where(second, second_v, first_v), acc)
        o_ref[0, e] = jnp.where(in_win, acc * LOG2E, NEG)


def _na_bias_tables(rpb):
    return pl.pallas_call(
        _na_table_kernel,
        grid=(NA_HEADS,),
        in_specs=[pl.BlockSpec(memory_space=pltpu.SMEM)],
        out_specs=pl.BlockSpec((1, NA_NENT, GRID_W, LANES), lambda h: (h, 0, 0, 0)),
        out_shape=jax.ShapeDtypeStruct((NA_HEADS, NA_NENT, GRID_W, LANES), F32),
        compiler_params=_cparams(1),
        name="na_bias_table",
    )(rpb.astype(F32).reshape(-1))


def _na_key_row0(j):
    return jnp.clip(NA_QROWS * j - NA_WIN_R // 2, 0, NA_ROWS - NA_KROWS)


def _na_kernel(q_ref, k_ref, v_ref, kc_ref, vc_ref, t_ref, o_ref):
    j = pl.program_id(1)
    u0 = _na_key_row0(j)
    start = pl.multiple_of(u0 * GRID_W, GRID_W)
    lo = _lane_lo((1, LANES))
    entry = []
    rowmask = []
    for i in range(NA_QROWS):
        r = NA_QROWS * j + i
        r0 = jnp.clip(r - NA_WIN_R // 2, 0, NA_ROWS - NA_WIN_R)
        entry.append([])
        rowmask.append([])
        for p in range(NA_KROWS // 2):
            a = u0 + 2 * p
            entry[i].append(jnp.clip(a - r + NA_WIN_R, 0, NA_NENT - 1))
            out_a = jnp.logical_or(a < r0, a >= r0 + NA_WIN_R)
            out_b = jnp.logical_or(a + 1 < r0, a + 1 >= r0 + NA_WIN_R)
            rowmask[i].append(jnp.where(lo, jnp.where(out_a, NEG, 0.0), jnp.where(out_b, NEG, 0.0)))
    for hp in range(NA_HEADS // 2):
        cols = slice(hp * LANES, (hp + 1) * LANES)
        bias_rows = []
        for head in range(2):
            for i in range(NA_QROWS):
                tiles = [t_ref[2 * hp + head, entry[i][p]] + rowmask[i][p] for p in range(NA_KROWS // 2)]
                bias_rows.append(jnp.concatenate(tiles, axis=1))
        bias = jnp.concatenate(bias_rows, axis=0)
        segs = [(k_ref[pl.ds(start, NA_NKEY), cols], v_ref[pl.ds(start, NA_NKEY), cols]),
                (kc_ref[:, cols], vc_ref[:, cols])]
        o_ref[:, cols] = _pair_attend(q_ref[:, cols], segs, [bias, None]).astype(o_ref.dtype)


def _na_attention(q, k, v, table, n_batch):
    ctx0 = n_batch * LAT_BLKS
    n_steps = NA_ROWS // NA_QROWS
    tq = NA_QROWS * GRID_W
    lat = pl.BlockSpec((N_LAT, D), lambda b, j: (b, 0))
    ctx = pl.BlockSpec((N_CTX, D), lambda b, j: (ctx0 + b, 0))
    return pl.pallas_call(
        _na_kernel,
        grid=(n_batch, n_steps),
        in_specs=[
            pl.BlockSpec((tq, D), lambda b, j: (b * n_steps + j, 0)),
            lat, lat, ctx, ctx,
            pl.BlockSpec(table.shape, lambda b, j: (0, 0, 0, 0), pipeline_mode=pl.Buffered(1)),
        ],
        out_specs=pl.BlockSpec((tq, D), lambda b, j: (b * n_steps + j, 0)),
        out_shape=jax.ShapeDtypeStruct((n_batch * N_LAT, D), BF16),
        compiler_params=_cparams(2),
        name="na_attention",
    )(q, k, v, k, v, table)


def _ctx_attn_kernel(q_ref, k_ref, v_ref, o_ref):
    for hp in range(NA_HEADS // 2):
        cols = slice(hp * LANES, (hp + 1) * LANES)
        segs = [(k_ref[:, cols], v_ref[:, cols])]
        o_ref[:, cols] = _pair_attend(q_ref[:, cols], segs, [None]).astype(o_ref.dtype)


def _ctx_attention(q, k, v, n_batch):
    ctx0 = n_batch * LAT_BLKS
    spec = pl.BlockSpec((N_CTX, D), lambda b: (ctx0 + b, 0))
    return pl.pallas_call(
        _ctx_attn_kernel,
        grid=(n_batch,),
        in_specs=[spec, spec, spec],
        out_specs=pl.BlockSpec((N_CTX, D), lambda b: (b, 0)),
        out_shape=jax.ShapeDtypeStruct((n_batch * N_CTX, D), BF16),
        compiler_params=_cparams(1),
        name="na_ctx_attention",
    )(q, k, v)


def _na_mixer(h, mod3, g, w_kvq, rpb, n_batch):
    segs = [(0, D, 1.0, BF16), (D, 2 * D, 1.0, BF16), (2 * D, 3 * D, NA_HD ** -0.5 * LOG2E, BF16)]
    k, v, q = _proj(h, mod3, g, w_kvq.astype(BF16), segs, n_batch)
    return _na_attention(q, k, v, _na_bias_tables(rpb), n_batch), _ctx_attention(q, k, v, n_batch)


DA_TQ = 1024
DA_W = 2 * DA_HD


def _rope_tables():
    t = np.arange(N_LAT)
    row = (t // GRID_W).astype(np.float64)
    col = (t % GRID_W).astype(np.float64)
    per_axis = DA_HD // 2
    freqs = ROPE_BASE ** (-np.arange(0, per_axis, 2, dtype=np.float64) / per_axis)
    ar = row[:, None] * freqs
    ac = col[:, None] * freqs
    ang = np.concatenate([ar, ar, ac, ac], axis=-1)
    cos = np.tile(np.cos(ang), (1, 2))
    sin = np.tile(np.sin(ang), (1, 2))
    quarter = (np.arange(DA_W) % DA_HD) // (DA_HD // 4)
    even = (quarter % 2 == 0)[None, :]
    sin_a = np.where(even, -sin, 0.0)
    sin_b = np.where(even, 0.0, sin)
    return tuple(jnp.asarray(a, F32) for a in (cos, sin_a, sin_b))


def _rope(x, cos, sin_a, sin_b):
    q16 = DA_HD // 4
    return x * cos + pltpu.roll(x, DA_W - q16, 1) * sin_a + pltpu.roll(x, q16, 1) * sin_b


def _da_lambda(lp, lam_init):
    a = jnp.sum(lp[0:1] * lp[1:2], axis=-1, keepdims=True)
    b = jnp.sum(lp[2:3] * lp[3:4], axis=-1, keepdims=True)
    return jnp.exp(a) - jnp.exp(b) + lam_init


DA_KCHUNK = 512


def _diff_attend(q, segs, lam, ng, lam_init):
    lo = _lane_lo(q.shape)
    qm = [_half_masked(q, lo, 0), _half_masked(q, lo, 1)]
    chunks = [(k, v, c0, min(c0 + DA_KCHUNK, k.shape[0])) for k, v in segs for c0 in range(0, k.shape[0], DA_KCHUNK)]
    s = [[_dot_nt(qm[j], k[c0:c1, :]) for j in range(2)] for k, _, c0, c1 in chunks]
    m = [None, None]
    for sc in s:
        for j in range(2):
            mc = sc[j].max(axis=-1, keepdims=True)
            m[j] = mc if m[j] is None else jnp.maximum(m[j], mc)
    r = [None, None]
    for sc, (_, v, c0, c1) in zip(s, chunks):
        vc = v[c0:c1, :]
        vx = jnp.concatenate([vc, jnp.ones_like(vc)], axis=1)
        for j in range(2):
            pv = _dot(jnp.exp2(sc[j] - m[j]).astype(BF16), vx)
            r[j] = pv if r[j] is None else r[j] + pv
    o = r[0][:, :DA_W] / r[0][:, DA_W:] - lam * (r[1][:, :DA_W] / r[1][:, DA_W:])
    return o * lax.rsqrt(jnp.mean(o * o, axis=-1, keepdims=True) + EPS) * ng * (1.0 - lam_init)


def _da_kernel(lam_ref, q_ref, k_ref, v_ref, kc_ref, vc_ref, ng_ref, o_ref, *, lam_init):
    lam = _da_lambda(lam_ref[...], lam_init)
    segs = [(k_ref, v_ref), (kc_ref, vc_ref)]
    o_ref[...] = _diff_attend(q_ref[...], segs, lam, ng_ref[...], lam_init).astype(o_ref.dtype)


def _da_attention(q, k, v, lam_p, norm_g, lam_init, n_batch):
    nqb = N_LAT // DA_TQ
    ctx0 = n_batch * LAT_BLKS
    lat = pl.BlockSpec((N_LAT, DA_W), lambda b, h, i: (b, h))
    ctx = pl.BlockSpec((N_CTX, DA_W), lambda b, h, i: (ctx0 + b, h))
    return pl.pallas_call(
        functools.partial(_da_kernel, lam_init=lam_init),
        grid=(n_batch, DA_HEADS, nqb),
        in_specs=[
            pl.BlockSpec((4, DA_HD), lambda b, h, i: (0, 0)),
            pl.BlockSpec((DA_TQ, DA_W), lambda b, h, i: (b * nqb + i, h)),
            lat, lat, ctx, ctx,
            pl.BlockSpec((1, DA_W), lambda b, h, i: (0, h)),
        ],
        out_specs=pl.BlockSpec((DA_TQ, DA_W), lambda b, h, i: (b * nqb + i, h)),
        out_shape=jax.ShapeDtypeStruct((n_batch * N_LAT, D), BF16),
        compiler_params=_cparams(3),
        name="diff_attention",
    )(lam_p, q, k, v, k, v, norm_g.reshape(1, D))


def _da_ctx_kernel(lam_ref, q_ref, k_ref, v_ref, ng_ref, o_ref, *, lam_init):
    lam = _da_lambda(lam_ref[...], lam_init)
    for h in range(DA_HEADS):
        cols = slice(h * DA_W, (h + 1) * DA_W)
        o = _diff_attend(q_ref[:, cols], [(k_ref[:, cols], v_ref[:, cols])], lam, ng_ref[:, cols], lam_init)
        o_ref[:, cols] = o.astype(o_ref.dtype)


def _da_ctx_attention(q, k, v, lam_p, norm_g, lam_init, n_batch):
    ctx0 = n_batch * LAT_BLKS
    spec = pl.BlockSpec((N_CTX, D), lambda b: (ctx0 + b, 0))
    return pl.pallas_call(
        functools.partial(_da_ctx_kernel, lam_init=lam_init),
        grid=(n_batch,),
        in_specs=[pl.BlockSpec((4, DA_HD), lambda b: (0, 0)), spec, spec, spec, pl.BlockSpec((1, D), lambda b: (0, 0))],
        out_specs=pl.BlockSpec((N_CTX, D), lambda b: (b, 0)),
        out_shape=jax.ShapeDtypeStruct((n_batch * N_CTX, D), BF16),
        compiler_params=_cparams(1),
        name="diff_ctx_attention",
    )(lam_p, q, k, v, norm_g.reshape(1, D))


def _da_mixer(h, mod3, g, w_kvq, lam_p, norm_g, layer_idx, n_batch):
    lam_init = 0.8 - 0.6 * float(np.exp(-0.3 * layer_idx))
    segs = [(0, D, 1.0, BF16, True), (D, 2 * D, 1.0, BF16), (2 * D, 3 * D, DA_HD ** -0.5 * LOG2E, BF16, True)]
    k, v, q = _proj(h, mod3, g, w_kvq.astype(BF16), segs, n_batch, rope_tabs=_rope_tables())
    lam_p = lam_p.astype(F32)
    return (_da_attention(q, k, v, lam_p, norm_g, lam_init, n_batch),
            _da_ctx_attention(q, k, v, lam_p, norm_g, lam_init, n_batch))


SCAN_STEPS = 1 + LAT_BLKS


def _scan_row_blk(direction, n_batch):
    ctx0 = n_batch * LAT_BLKS

    def blk(b, s):
        lat = b * LAT_BLKS + (s - 1 if direction == 0 else LAT_BLKS - s)
        return jnp.where(s == 0, ctx0 + b, lat)
    return blk


def _scan_spec(width, direction, n_batch):
    blk = _scan_row_blk(direction, n_batch)
    return pl.BlockSpec((ROW_BLK, width), lambda b, h, s: (blk(b, s), h))


def _scan_spec_all(width, direction, n_batch):
    blk = _scan_row_blk(direction, n_batch)
    return pl.BlockSpec((ROW_BLK, width), lambda b, h, s: (blk(b, s), 0))


def _scan_tri():
    u = np.arange(ROW_BLK)
    tt, uu = np.meshgrid(u, u, indexing="ij")
    return jnp.asarray(np.stack([uu <= tt, uu >= tt]).astype(np.float32), BF16)


def _split3(x):
    hi = x.astype(BF16)
    r1 = x - hi.astype(F32)
    mid = r1.astype(BF16)
    lo = (r1 - mid.astype(F32)).astype(BF16)
    return hi, mid, lo


def _scan_out_specs(width):
    return [pl.BlockSpec((N_LAT, width), lambda b, h, s: (b, h)), pl.BlockSpec((N_CTX, width), lambda b, h, s: (b, h))]


def _scan_write_sum(step, out_fwd, out_bwd, lat_ref, ctx_ref):
    @pl.when(step == 0)
    def _():
        ctx_ref[...] = out_fwd + out_bwd

    rows_f = pl.ds(pl.multiple_of((step - 1) * ROW_BLK, ROW_BLK), ROW_BLK)
    rows_b = pl.ds(pl.multiple_of((LAT_BLKS - step) * ROW_BLK, ROW_BLK), ROW_BLK)

    @pl.when(jnp.logical_and(step >= 1, step <= LAT_BLKS // 2))
    def _():
        lat_ref[rows_f, :] = out_fwd
        lat_ref[rows_b, :] = out_bwd

    @pl.when(step > LAT_BLKS // 2)
    def _():
        lat_ref[rows_f, :] += out_fwd
        lat_ref[rows_b, :] += out_bwd


def _conv_silu(x_ref, prev_ref, next_ref, w_ref, first, last, scale):
    x = x_ref[...]
    rid = lax.broadcasted_iota(jnp.int32, x.shape, 0)
    prev_row = jnp.where(first, 0.0, prev_ref[7:8, :])
    next_row = jnp.where(last, 0.0, next_ref[0:1, :])
    x_prev = jnp.where(rid == 0, prev_row, pltpu.roll(x, 1, 0))
    x_next = jnp.where(rid == ROW_BLK - 1, next_row, pltpu.roll(x, ROW_BLK - 1, 0))
    y = w_ref[0:1, :] * x_prev + w_ref[1:2, :] * x + w_ref[2:3, :] * x_next
    return (_silu(y) * scale).astype(BF16)


def _ml_conv_kernel(xk_ref, xkp_ref, xkn_ref, xq_ref, xqp_ref, xqn_ref, wk_ref, wq_ref, ok_ref, oq_ref, *, n_lat_blks):
    i = pl.program_id(0)
    is_ctx = i >= n_lat_blks
    first = jnp.logical_or(is_ctx, i % LAT_BLKS == 0)
    last = jnp.logical_or(is_ctx, i % LAT_BLKS == LAT_BLKS - 1)
    ok_ref[...] = _conv_silu(xk_ref, xkp_ref, xkn_ref, wk_ref, first, last, ML_HD ** -0.5)
    oq_ref[...] = _conv_silu(xq_ref, xqp_ref, xqn_ref, wq_ref, first, last, 1.0)


def _ml_conv(pk, pq, conv_w, n_batch):
    n_rows = pk.shape[0]
    n_blk = n_rows // ROW_BLK
    sub = ROW_BLK // 8
    main = pl.BlockSpec((ROW_BLK, D), lambda i: (i, 0))
    prev = pl.BlockSpec((8, D), lambda i: (jnp.maximum(i * sub - 1, 0), 0))
    nxt = pl.BlockSpec((8, D), lambda i: (jnp.minimum((i + 1) * sub, n_blk * sub - 1), 0))
    return pl.pallas_call(
        functools.partial(_ml_conv_kernel, n_lat_blks=n_batch * LAT_BLKS),
        grid=(n_blk,),
        in_specs=[main, prev, nxt, main, prev, nxt,
                  pl.BlockSpec((3, D), lambda i: (0, 0)), pl.BlockSpec((3, D), lambda i: (0, 1))],
        out_specs=[main, main],
        out_shape=[jax.ShapeDtypeStruct((n_rows, D), BF16)] * 2,
        compiler_params=_cparams(1),
        name="mlstm_conv",
    )(pk, pk, pk, pq, pq, pq, conv_w, conv_w)


def _row_sum(x):
    return jnp.sum(x[:, :LANES] + x[:, LANES:], axis=1, keepdims=True)


def _pick_lane(x, idx):
    lane = lax.broadcasted_iota(jnp.int32, x.shape, 1)
    return jnp.sum(jnp.where(lane == idx, x, 0.0), axis=1, keepdims=True)


def _pick_sublane(x, idx):
    sub = lax.broadcasted_iota(jnp.int32, x.shape, 0)
    return jnp.sum(jnp.where(sub == idx, x, 0.0), axis=0, keepdims=True)


def _ml_scan_kernel(gb_ref, g0_ref, q0_ref, k0_ref, v0_ref, g1_ref, q1_ref, k1_ref, v1_ref,
                    lat_ref, ctx_ref, c_ref, n_ref, m_ref):
    head0 = pl.program_id(1) * ML_HPS

    @pl.when(pl.program_id(2) == 0)
    def _():
        c_ref[...] = jnp.zeros_like(c_ref)
        n_ref[...] = jnp.zeros_like(n_ref)
        m_ref[...] = jnp.zeros_like(m_ref)

    r = lax.broadcasted_iota(jnp.int32, (ROW_BLK, ROW_BLK), 0)
    c = lax.broadcasted_iota(jnp.int32, (ROW_BLK, ROW_BLK), 1)
    outs = [[None] * ML_HPS, [None] * ML_HPS]

    def direction(d, hh, g_ref, q_ref, k_ref, v_ref):
        valid = (r >= c) if d == 0 else (r <= c)
        other = (r <= c) if d == 0 else (r >= c)
        g = g_ref[...] + gb_ref[...]
        gt = g.T
        ii = d * 2 * ML_HEADS + head0 + hh
        fi = ii + ML_HEADS
        si = d * ML_HPS + hh
        cols = slice(hh * ML_HD, (hh + 1) * ML_HD)
        m_prev = m_ref[si][0:1, 0:1]
        n_prev = n_ref[si][0:1, :]
        c_prev = c_ref[si]
        qc = q_ref[:, cols]
        kc = k_ref[:, cols]
        vc = v_ref[:, cols]
        qk = _dot_nt(qc, kc)
        q_c = _dot(qc, c_prev.astype(BF16))
        q_n = _row_sum(qc.astype(F32) * n_prev)
        yield
        i_col = _pick_lane(g, ii)
        f_col = _log_sigmoid(_pick_lane(g, fi))
        i_row = _pick_sublane(gt, ii)
        f_row = _log_sigmoid(_pick_sublane(gt, fi))
        yield
        b_col = _row_sum(jnp.where(valid, f_row, 0.0))
        b_row = jnp.sum(jnp.where(other, f_col, 0.0), axis=0, keepdims=True)
        total = jnp.sum(f_col, axis=0, keepdims=True)
        yield
        dmat = jnp.where(valid, b_col - b_row + i_row, NEG)
        inter = b_col + m_prev
        m_t = jnp.maximum(inter, jnp.maximum(dmat[:, :LANES], dmat[:, LANES:]).max(axis=1, keepdims=True))
        w_inter = jnp.exp(inter - m_t)
        g_col = total - b_col + i_col
        m_new = jnp.maximum(total + m_prev, g_col.max(axis=0, keepdims=True))
        w_old = jnp.exp(total + m_prev - m_new)
        kw = kc.astype(F32) * jnp.exp(g_col - m_new)
        upd = _dot_tn(kw.astype(BF16), vc)
        yield
        s = qk * jnp.exp(dmat - m_t)
        num = w_inter * q_c + _dot(s.astype(BF16), vc)
        den = w_inter * q_n + _row_sum(s)
        outs[d][hh] = num / jnp.maximum(jnp.abs(den), jnp.exp(-m_t))
        yield
        c_ref[si] = w_old * c_prev + upd
        n_ref[si] = jnp.broadcast_to(w_old * n_prev + kw.sum(axis=0, keepdims=True), n_ref.shape[1:])
        m_ref[si] = jnp.broadcast_to(m_new, m_ref.shape[1:])

    stages = []
    for hh in range(ML_HPS):
        stages += [direction(0, hh, g0_ref, q0_ref, k0_ref, v0_ref), direction(1, hh, g1_ref, q1_ref, k1_ref, v1_ref)]
    while stages:
        stages = [gen for gen in stages if next(gen, StopIteration) is not StopIteration]
    _scan_write_sum(pl.program_id(2), jnp.concatenate(outs[0], axis=1), jnp.concatenate(outs[1], axis=1),
                    lat_ref, ctx_ref)


def _ml_scan(q, k, v, gates, gate_b, n_batch):
    width = ML_HPS * ML_HD
    in_specs = [pl.BlockSpec((1, LANES), lambda b, h, s: (0, 0))]
    for d in range(2):
        in_specs += [_scan_spec_all(LANES, d, n_batch)] + [_scan_spec(width, d, n_batch)] * 3
    return pl.pallas_call(
        _ml_scan_kernel,
        grid=(n_batch, ML_HEADS // ML_HPS, SCAN_STEPS),
        in_specs=in_specs,
        out_specs=_scan_out_specs(width),
        out_shape=[jax.ShapeDtypeStruct((n_batch * N_LAT, D), F32), jax.ShapeDtypeStruct((n_batch * N_CTX, D), F32)],
        scratch_shapes=[pltpu.VMEM((2 * ML_HPS, ML_HD, ML_HD), F32), pltpu.VMEM((2 * ML_HPS, 8, ML_HD), F32),
                        pltpu.VMEM((2 * ML_HPS, 8, LANES), F32)],
        compiler_params=_cparams(3),
        name="mlstm_scan",
    )(gate_b, gates, q, k, v, gates, q, k, v)


def _ml_mixer(h, mod3, g, w_in, gate_b, conv_w, n_batch):
    n_g = 4 * ML_HEADS
    w_in = w_in.astype(BF16)
    wk, wv, wg, wq, wo = (w_in[:, :D], w_in[:, D:2 * D], w_in[:, 2 * D:2 * D + n_g],
                          w_in[:, 2 * D + n_g:3 * D + n_g], w_in[:, 3 * D + n_g:])
    w = jnp.concatenate([wk, wv, wq, wo, jnp.pad(wg, ((0, 0), (0, LANES - n_g)))], axis=1)
    segs = [(0, D, 1.0, F32), (D, 2 * D, 1.0, BF16), (2 * D, 3 * D, 1.0, F32), (3 * D, 4 * D, 1.0, F32),
            (4 * D, 4 * D + LANES, 1.0, F32)]
    pk, v, pq, og, gates = _proj(h, mod3, g, w, segs, n_batch)
    k, q = _ml_conv(pk, pq, conv_w.astype(F32), n_batch)
    gb = jnp.pad(gate_b.astype(F32), (0, LANES - n_g)).reshape(1, LANES)
    h_lat, h_ctx = _ml_scan(q, k, v, gates, gb, n_batch)
    return h_lat, h_ctx, og


GLA_LEVELS = (256, 128, 64, 32, 16, 8, 4, 2)
GLA_BCAST_LEVELS = 6
GLA_DIAG = len(GLA_LEVELS)
GLA_HPS = 4


def _gla_level_ids():
    u = np.arange(ROW_BLK // 2)
    tt, ss = np.meshgrid(u, u, indexing="ij")
    hb = np.floor(np.log2(np.maximum(tt ^ ss, 1))).astype(np.int64)
    level = len(GLA_LEVELS) - 1 - hb
    lid = np.stack([np.where(tt == ss, GLA_DIAG, np.where(tt > ss, level, -1)),
                    np.where(tt == ss, GLA_DIAG, np.where(tt < ss, level, -1))]).astype(np.int32)
    return jnp.asarray(np.concatenate([lid, lid], axis=1))


def _gla_level_exponents(level, d, bc_ref, la_ref, row):
    n = GLA_LEVELS[level]
    half = n // 2
    if level < GLA_BCAST_LEVELS:
        parts = []
        for j in range(ROW_BLK // n):
            mid = j * n + (half - 1 if d == 0 else half)
            parts.append(jnp.broadcast_to(bc_ref[mid:mid + 1, :], (n, GLA_DK)))
        e = -jnp.abs(bc_ref[...] - (parts[0] if len(parts) == 1 else jnp.concatenate(parts, axis=0)))
        return e, e
    o = row & (n - 1)
    la = la_ref[...]
    if n == 4:
        la_prev = pltpu.roll(la, 1, 0)
        la_next = pltpu.roll(la, ROW_BLK - 1, 0)
        if d == 0:
            return (jnp.where(o == 2, la, jnp.where(o == 3, la + la_prev, 0.0)), jnp.where(o == 0, la_next, 0.0))
        return (jnp.where(o == 0, la + la_next, jnp.where(o == 1, la, 0.0)), jnp.where(o == 3, la_prev, 0.0))
    return jnp.where(o == (1 if d == 0 else 0), la, 0.0), None


def _gla_kernel(tri_ref, lid_ref, wup_ref, bg_ref, q0_ref, k0_ref, v0_ref, l0_ref, q1_ref, k1_ref, v1_ref, l1_ref,
                lat_ref, ctx_ref, st_ref, bc_ref, att_ref, la_ref):
    @pl.when(pl.program_id(2) == 0)
    def _():
        st_ref[...] = jnp.zeros_like(st_ref)

    row = lax.broadcasted_iota(jnp.int32, (ROW_BLK, GLA_DK), 0)
    outs = [[None] * GLA_HPS, [None] * GLA_HPS]

    def direction(d, hh, q_ref, k_ref, v_ref, low_ref):
        si = d * GLA_HPS + hh
        kc = slice(hh * GLA_DK, (hh + 1) * GLA_DK)
        vc = slice(hh * GLA_DV, (hh + 1) * GLA_DV)
        pre = _dot(low_ref[...].astype(BF16), wup_ref[d][:, kc]) + bg_ref[d][:, kc]
        la = _log_sigmoid(pre) * (LOG2E / GLA_TAU)
        hi, mid, lo = _split3(la)
        tri = tri_ref[d]
        bc_ref[si] = _dot(tri, hi) + _dot(tri, mid) + _dot(tri, lo)
        la_ref[si] = la
        half = ROW_BLK // 2

        def diag_blocks(a, b):
            return jnp.concatenate([_dot_nt(a[:half], b[:half]), _dot_nt(a[half:], b[half:])], axis=0)

        att_ref[si] = jnp.where(lid_ref[d] == GLA_DIAG,
                                diag_blocks(q_ref[:, kc].astype(BF16), k_ref[:, kc].astype(BF16)), 0.0)
        yield
        eq, _ = _gla_level_exponents(0, d, bc_ref.at[si], la_ref.at[si], row)
        q_rows, k_rows = (slice(half, None), slice(None, half)) if d == 0 else (slice(None, half), slice(half, None))
        f = jnp.exp2(eq)
        off = _dot_nt((q_ref[q_rows, kc] * f[q_rows]).astype(BF16), (k_ref[k_rows, kc] * f[k_rows]).astype(BF16))
        yield
        for level in range(1, len(GLA_LEVELS)):
            eq, ek = _gla_level_exponents(level, d, bc_ref.at[si], la_ref.at[si], row)
            fq = jnp.exp2(eq)
            ql = (q_ref[:, kc] * fq).astype(BF16)
            k = k_ref[:, kc]
            kl = k.astype(BF16) if ek is None else (k * (fq if ek is eq else jnp.exp2(ek))).astype(BF16)
            att_ref[si] = jnp.where(lid_ref[d] == level, diag_blocks(ql, kl), att_ref[si])
            yield
        st = st_ref[si]
        bc = bc_ref[si]
        k = k_ref[:, kc]
        v = v_ref[:, vc]
        qd = (q_ref[:, kc] * jnp.exp2(bc)).astype(BF16)
        diag = att_ref[si]
        if d == 0:
            o_top = _dot(diag[:half].astype(BF16), v[:half])
            o_bot = _dot(jnp.concatenate([off, diag[half:]], axis=1).astype(BF16), v)
        else:
            o_top = _dot(jnp.concatenate([diag[:half], off], axis=1).astype(BF16), v)
            o_bot = _dot(diag[half:].astype(BF16), v[half:])
        outs[d][hh] = jnp.concatenate([o_top, o_bot], axis=0) + _dot_nt(qd, st.astype(BF16))
        b_end = bc[ROW_BLK - 1:ROW_BLK] if d == 0 else bc[0:1]
        kd = (k * jnp.exp2(b_end - bc)).astype(BF16)
        st_ref[si] = st * jnp.exp2(b_end) + _dot_tn(v, kd)

    stages = []
    for hh in range(GLA_HPS):
        stages += [direction(0, hh, q0_ref, k0_ref, v0_ref, l0_ref), direction(1, hh, q1_ref, k1_ref, v1_ref, l1_ref)]
    while stages:
        stages = [gen for gen in stages if next(gen, StopIteration) is not StopIteration]
    _scan_write_sum(pl.program_id(2), jnp.concatenate(outs[0], axis=1), jnp.concatenate(outs[1], axis=1),
                    lat_ref, ctx_ref)


def _gla_scan(q, k, v, low, wup, bg, n_batch):
    tri, lid = _scan_tri(), _gla_level_ids()
    const3 = lambda b, h, s: (0, 0, 0)
    in_specs = [
        pl.BlockSpec(tri.shape, const3),
        pl.BlockSpec(lid.shape, const3),
        pl.BlockSpec((2, LANES, GLA_HPS * GLA_DK), lambda b, h, s: (0, 0, h)),
        pl.BlockSpec((2, 1, GLA_HPS * GLA_DK), lambda b, h, s: (0, 0, h)),
    ]
    for d in range(2):
        in_specs += [_scan_spec(GLA_HPS * GLA_DK, d, n_batch), _scan_spec(GLA_HPS * GLA_DK, d, n_batch),
                     _scan_spec(GLA_HPS * GLA_DV, d, n_batch), _scan_spec_all(LANES, d, n_batch)]
    n_slots = 2 * GLA_HPS
    return pl.pallas_call(
        _gla_kernel,
        grid=(n_batch, GLA_HEADS // GLA_HPS, SCAN_STEPS),
        in_specs=in_specs,
        out_specs=_scan_out_specs(GLA_HPS * GLA_DV),
        out_shape=[jax.ShapeDtypeStruct((n_batch * N_LAT, D), F32), jax.ShapeDtypeStruct((n_batch * N_CTX, D), F32)],
        scratch_shapes=[pltpu.VMEM((n_slots, GLA_DV, GLA_DK), F32), pltpu.VMEM((n_slots, ROW_BLK, GLA_DK), F32),
                        pltpu.VMEM((n_slots, ROW_BLK, ROW_BLK // 2), F32), pltpu.VMEM((n_slots, ROW_BLK, GLA_DK), F32)],
        compiler_params=_cparams(3),
        name="gla_scan",
    )(tri, lid, wup, bg, q, k, v, low, q, k, v, low)


def _gla_mixer(h, mod3, g, w_in, w_gate_up, b_gate, n_batch):
    dk_t = GLA_HEADS * GLA_DK
    n_low = 2 * GLA_RANK
    w_in = w_in.astype(BF16)
    wk, wv, wl, wq, wr = (w_in[:, :dk_t], w_in[:, dk_t:dk_t + D], w_in[:, dk_t + D:dk_t + D + n_low],
                          w_in[:, dk_t + D + n_low:2 * dk_t + D + n_low], w_in[:, 2 * dk_t + D + n_low:])
    w = jnp.concatenate([wk, wq, wv, wr, jnp.pad(wl, ((0, 0), (0, LANES - n_low)))], axis=1)
    c = [0, dk_t, 2 * dk_t, 2 * dk_t + D, 2 * dk_t + 2 * D, 2 * dk_t + 2 * D + LANES]
    segs = [(c[0], c[1], 1.0, F32), (c[1], c[2], GLA_DK ** -0.5, F32), (c[2], c[3], 1.0, BF16),
            (c[3], c[4], 1.0, F32), (c[4], c[5], 1.0, F32)]
    k, q, v, rg, low = _proj(h, mod3, g, w, segs, n_batch)
    wup = jnp.zeros((2, LANES, dk_t), F32)
    for r in range(2):
        wup = wup.at[r, r * GLA_RANK:(r + 1) * GLA_RANK].set(w_gate_up[r].astype(F32))
    o_lat, o_ctx = _gla_scan(q, k, v, low, wup.astype(BF16), b_gate.astype(F32).reshape(2, 1, dk_t), n_batch)
    return o_lat, o_ctx, rg


def kernel(x, c, ctx, c_ctx, ada_w, ada_b, norm_g, ffn_w13, ffn_w2, final_g, na_w_kvq, na_rpb, na_w_o, ml_w_in, ml_gate_b, ml_conv_w, ml_norm_g, ml_w_o, da_w_kvq, da_lam, da_norm_g, da_w_o, gla_w_in, gla_w_gate_up, gla_b_gate, gla_norm_g, gla_w_o):
    nb = x.shape[0]
    assert x.shape[1:] == (N_LAT, D) and ctx.shape[1:] == (N_CTX, D) and nb < MOD_ROWS
    n_lat = nb * N_LAT
    n_tok = n_lat + nb * N_CTX

    s = jnp.zeros((MOD_ROWS, D), F32).at[:nb].set(c).at[nb].set(c_ctx)
    modtab = _mod_tables(s, ada_w, ada_b)
    h = (x.reshape(n_lat, D), ctx.reshape(nb * N_CTX, D))
    w13 = ffn_w13.astype(BF16)
    w2 = ffn_w2.astype(BF16)

    for i in range(DEPTH):
        kind, j = i % 4, i // 4
        last = i == DEPTH - 1
        mod3 = modtab[i].reshape(MOD_ROWS * N_MOD, 1, D)
        h = _ffn(h, mod3, norm_g[i, 0], w13, w2, (i, 0), final_g, 0, n_tok, nb)
        n_out = n_lat if last else n_tok
        if kind == 0:
            ins = _na_mixer(h, mod3, norm_g[i, 1], na_w_kvq[j], na_rpb[j], nb)
            mixer, w_o, mixer_g = "plain", na_w_o[j], None
        elif kind == 1:
            ins = _ml_mixer(h, mod3, norm_g[i, 1], ml_w_in[j], ml_gate_b[j], ml_conv_w[j], nb)
            mixer, w_o, mixer_g = "ml", ml_w_o[j], ml_norm_g[j]
        elif kind == 2:
            ins = _da_mixer(h, mod3, norm_g[i, 1], da_w_kvq[j], da_lam[j], da_norm_g[j], i, nb)
            mixer, w_o, mixer_g = "plain", da_w_o[j], None
        else:
            ins = _gla_mixer(h, mod3, norm_g[i, 1], gla_w_in[j], gla_w_gate_up[j], gla_b_gate[j], nb)
            mixer, w_o, mixer_g = "gla", gla_w_o[j], gla_norm_g[j]
        h = _ffn(h, mod3, norm_g[i, 2], w13, w2, (i, 1), final_g, 6, n_out, nb, final=last, mixer=mixer, mixer_ins=ins, w_o=w_o.astype(BF16), mixer_g=mixer_g)
    return h.reshape(nb, N_LAT, D)
```

```python
import functools

import numpy as np
import jax
import jax.numpy as jnp
from jax import lax
from jax.experimental import pallas as pl
from jax.experimental.pallas import tpu as pltpu

F32 = jnp.float32
BF16 = jnp.bfloat16

D = 1024
N_LAT = 2048
N_CTX = 256
DEPTH = 4
N_MOD = 9
D_FF = 2816
EPS = 1e-6
NEG = -1e30
LOG2E = 1.4426950408889634
GRID_W = 64

NA_HEADS = 16
NA_HD = 64
NA_WIN_R = 8
NA_WIN_C = 16

ML_HEADS = 4
ML_HD = 256
ML_HPS = 4

DA_HEADS = 8
DA_HD = 64
ROPE_BASE = 10000.0

GLA_HEADS = 4
GLA_DK = 128
GLA_DV = 256
GLA_RANK = 16
GLA_TAU = 16.0

ROW_BLK = 256
LAT_BLKS = N_LAT // ROW_BLK
MOD_ROWS = 16
LANES = 128
VMEM_LIMIT = 56 * 1024 * 1024


def _cparams(n_axes, vmem=VMEM_LIMIT):
    return pltpu.CompilerParams(dimension_semantics=("arbitrary",) * n_axes, vmem_limit_bytes=vmem)


def _sigmoid(x):
    return 1.0 / (1.0 + jnp.exp(-x))


def _silu(x):
    return x * _sigmoid(x)


def _log_sigmoid(x):
    return jnp.minimum(x, 0.0) - jnp.log(1.0 + jnp.exp(-jnp.abs(x)))


def _rms(x, g):
    return x * lax.rsqrt(jnp.mean(x * x, axis=-1, keepdims=True) + EPS) * g


def _dot(a, b):
    return jnp.dot(a, b, preferred_element_type=F32)


def _dot_nt(a, b):
    return lax.dot_general(a, b, (((1,), (1,)), ((), ())), preferred_element_type=F32)


def _dot_tn(a, b):
    return lax.dot_general(a, b, (((0,), (0,)), ((), ())), preferred_element_type=F32)


def _mod_row(i, tm, n_batch):
    return jnp.minimum(i // (N_LAT // tm), n_batch)


def _mod_spec(k, tm, n_batch):
    return pl.BlockSpec((1, 1, D), lambda i: (_mod_row(i, tm, n_batch) * N_MOD + k, 0, 0))


def _mod_kernel(s_ref, w_ref, b_ref, o_ref):
    a = _silu(s_ref[...]).astype(BF16)
    o_ref[0] = _dot(a, w_ref[0].astype(BF16)) + b_ref[0]


def _mod_tables(s, ada_w, ada_b):
    return pl.pallas_call(
        _mod_kernel,
        grid=(DEPTH, N_MOD),
        in_specs=[
            pl.BlockSpec((MOD_ROWS, D), lambda i, k: (0, 0)),
            pl.BlockSpec((1, D, D), lambda i, k: (i, 0, k)),
            pl.BlockSpec((1, 1, D), lambda i, k: (i, 0, k)),
        ],
        out_specs=pl.BlockSpec((1, MOD_ROWS, D), lambda i, k: (i, 0, k)),
        out_shape=jax.ShapeDtypeStruct((DEPTH, MOD_ROWS, N_MOD * D), F32),
        compiler_params=_cparams(2),
        name="mod_tables",
    )(s, ada_w, ada_b.reshape(DEPTH, 1, N_MOD * D))


FFN_TM = 512
MXU_TILE = 256
FFN_SPLIT = (D_FF // MXU_TILE + 1) // 2 * MXU_TILE
FFN_CHUNKS = ((0, FFN_SPLIT), (FFN_SPLIT, D_FF))


def _head_rms(x, g, n_heads):
    hd = D // n_heads
    parts = []
    for i in range(n_heads):
        xi = x[:, i * hd:(i + 1) * hd]
        parts.append(xi * lax.rsqrt(jnp.mean(xi * xi, axis=-1, keepdims=True) + EPS))
    return jnp.concatenate(parts, axis=-1) * g


def _mixer_readout(mode, refs, is_lat):
    y = jnp.where(is_lat, refs[0][...], refs[1][...])
    if mode == "ml":
        return _sigmoid(refs[2][...]) * _head_rms(y, refs[3][...], ML_HEADS)
    if mode == "gla":
        return _head_rms(y, refs[3][...], GLA_HEADS) * _silu(refs[2][...])
    return y


def _ffn_kernel(g_ref, sh_ref, sc_ref, gt_ref, w13_ref, w2_ref, fg_ref, *refs, final, mixer, split_h, n_lat_tiles):
    o_ref = refs[-1]
    if split_h:
        x = jnp.where(pl.program_id(0) < n_lat_tiles, refs[0][...], refs[1][...])
        refs = refs[2:]
    else:
        x = refs[0][...]
        refs = refs[1:]
    if mixer is not None:
        mg_ref, wo_ref = refs[:2]
        y = _mixer_readout(mixer, refs[2:-1], pl.program_id(0) < n_lat_tiles)
        x = x + mg_ref[0] * _dot(y.astype(BF16), wo_ref[...])
    a = (_rms(x, g_ref[...]) * (1.0 + sc_ref[0]) + sh_ref[0]).astype(BF16)
    acc = jnp.zeros(x.shape, F32)
    for c0, c1 in FFN_CHUNKS:
        gate = _dot(a, w13_ref[:, c0:c1])
        up = _dot(a, w13_ref[:, D_FF + c0:D_FF + c1])
        act = (_silu(gate) * up).astype(BF16)
        acc = acc + _dot(act, w2_ref[c0:c1, :])
    y = x + 0.5 * gt_ref[0] * acc
    if final:
        y = _rms(y, fg_ref[...])
    o_ref[...] = y


def _ffn(h, mod3, g, w13, w2, widx, final_g, k0, n_rows, n_batch, final=False, mixer=None, mixer_ins=(), w_o=None,
         mixer_g=None):
    tm = FFN_TM
    nl = n_batch * N_LAT // tm
    const = lambda i: (0, 0)
    row = pl.BlockSpec((tm, D), lambda i: (i, 0))
    lat_ctx = [pl.BlockSpec((tm, D), lambda i: (jnp.minimum(i, nl - 1), 0)),
               pl.BlockSpec((tm, D), lambda i: (jnp.maximum(i - nl, 0), 0))]
    split_h = isinstance(h, tuple)
    in_specs = [
        pl.BlockSpec((1, D), const),
        _mod_spec(k0, tm, n_batch),
        _mod_spec(k0 + 1, tm, n_batch),
        _mod_spec(k0 + 2, tm, n_batch),
        pl.BlockSpec((None, None, D, 2 * D_FF), lambda i: (*widx, 0, 0), pipeline_mode=pl.Buffered(1)),
        pl.BlockSpec((None, None, D_FF, D), lambda i: (*widx, 0, 0), pipeline_mode=pl.Buffered(1)),
        pl.BlockSpec((1, D), const),
    ] + (lat_ctx if split_h else [row])
    args = [g.reshape(1, D), mod3, mod3, mod3, w13, w2, final_g.reshape(1, D)] + (list(h) if split_h else [h])
    if mixer is not None:
        in_specs += [_mod_spec(5, tm, n_batch), pl.BlockSpec((D, D), const, pipeline_mode=pl.Buffered(1))] + lat_ctx
        args += [mod3, w_o, mixer_ins[0], mixer_ins[1]]
        if mixer != "plain":
            in_specs += [row, pl.BlockSpec((1, D), const)]
            args += [mixer_ins[2], mixer_g.reshape(1, D)]
    return pl.pallas_call(
        functools.partial(_ffn_kernel, final=final, mixer=mixer, split_h=split_h, n_lat_tiles=nl),
        grid=(n_rows // tm,),
        in_specs=in_specs,
        out_specs=row,
        out_shape=jax.ShapeDtypeStruct((n_rows, D), F32),
        compiler_params=_cparams(1),
        name="ffn" if mixer is None else "mixer_out_ffn",
    )(*args)


PROJ_TM = 512


def _proj_kernel(h_ref, g_ref, sh_ref, sc_ref, w_ref, *refs, segs, n_rope):
    rope_refs, o_refs = refs[:n_rope], refs[n_rope:]
    x = h_ref[...]
    a = (_rms(x, g_ref[...]) * (1.0 + sc_ref[0]) + sh_ref[0]).astype(BF16)
    for o_ref, (c0, c1, scale, rope) in zip(o_refs, segs):
        y = _dot(a, w_ref[:, c0:c1])
        if scale != 1.0:
            y = y * scale
        if rope:
            tabs = [r[...] for r in rope_refs]
            y = jnp.concatenate([_rope(y[:, c:c + LANES], *tabs) for c in range(0, c1 - c0, LANES)], axis=1)
        o_ref[...] = y.astype(o_ref.dtype)


def _proj(h, mod3, g, w, segs, n_batch, rope_tabs=()):
    tm = PROJ_TM
    n_rows = h.shape[0]
    const = lambda i: (0, 0)
    n_pos = N_LAT // tm
    nl = n_batch * n_pos
    rope_spec = pl.BlockSpec((tm, LANES), lambda i: (jnp.where(i < nl, i % n_pos, n_pos), 0))
    identity = (jnp.ones((tm, LANES), F32), jnp.zeros((tm, LANES), F32), jnp.zeros((tm, LANES), F32))
    rope_args = [jnp.concatenate([t, e], axis=0) for t, e in zip(rope_tabs, identity)]
    return pl.pallas_call(
        functools.partial(_proj_kernel, segs=tuple((s[0], s[1], s[2], len(s) > 4 and s[4]) for s in segs),
                          n_rope=len(rope_args)),
        grid=(n_rows // tm,),
        in_specs=[
            pl.BlockSpec((tm, D), lambda i: (i, 0)),
            pl.BlockSpec((1, D), const),
            _mod_spec(3, tm, n_batch),
            _mod_spec(4, tm, n_batch),
            pl.BlockSpec(w.shape, const, pipeline_mode=pl.Buffered(1)),
        ] + [rope_spec] * len(rope_args),
        out_specs=[pl.BlockSpec((tm, s[1] - s[0]), lambda i: (i, 0)) for s in segs],
        out_shape=[jax.ShapeDtypeStruct((n_rows, s[1] - s[0]), s[3]) for s in segs],
        compiler_params=_cparams(1),
        name="mixer_proj",
    )(h, g.reshape(1, D), mod3, mod3, w, *rope_args)


def _lane_lo(shape):
    return lax.broadcasted_iota(jnp.int32, shape, 1) < (LANES // 2)


def _half_masked(x, lo, head):
    return jnp.where(lo if head == 0 else jnp.logical_not(lo), x, jnp.zeros_like(x))


def _pair_attend(qp, segs, biases):
    stages = _pair_attend_stages(qp, segs, biases)
    next(stages)
    return next(stages)


def _pair_attend_stages(qp, segs, biases):
    tq = qp.shape[0]
    lo = _lane_lo(qp.shape)
    q2 = jnp.concatenate([_half_masked(qp, lo, 0), _half_masked(qp, lo, 1)], axis=0)
    scores = []
    for (k, _), b in zip(segs, biases):
        s = _dot_nt(q2, k)
        scores.append(s if b is None else s + b)
    yield None
    m = scores[0].max(axis=-1, keepdims=True)
    for s in scores[1:]:
        m = jnp.maximum(m, s.max(axis=-1, keepdims=True))
    acc = None
    for s, (_, v) in zip(scores, segs):
        vx = jnp.concatenate([v, jnp.ones_like(v)], axis=1)
        pv = _dot(jnp.exp2(s - m).astype(BF16), vx)
        acc = pv if acc is None else acc + pv
    o2 = acc[:, :LANES] / acc[:, LANES:]
    yield jnp.where(lo, o2[:tq], o2[tq:])


NA_ROWS = N_LAT // GRID_W
NA_QROWS = 4
NA_KROWS = 12
NA_NKEY = NA_KROWS * GRID_W
NA_PAIRS_IN_FLIGHT = 8
NA_NOFF = 2 * NA_WIN_R - 1
NA_NENT = NA_NOFF + 1


NA_NCOFF = 2 * NA_WIN_C - 1


def _na_table_kernel(rpb_ref, o_ref):
    h = pl.program_id(0)
    c = lax.broadcasted_iota(jnp.int32, (GRID_W, LANES), 0)
    lane = lax.broadcasted_iota(jnp.int32, (GRID_W, LANES), 1)
    kc = lane & (GRID_W - 1)
    second = lane >= GRID_W
    col_off = jnp.clip(kc - c, 1 - NA_WIN_C, NA_WIN_C - 1) + NA_WIN_C - 1
    win0 = jnp.clip(c - NA_WIN_C // 2, 0, GRID_W - NA_WIN_C)
    in_win = jnp.logical_and(kc >= win0, kc < win0 + NA_WIN_C)
    for e in range(NA_NENT):
        acc = jnp.zeros((GRID_W, LANES), F32)
        for o in range(NA_NCOFF):
            first_v = rpb_ref[(h * NA_NOFF + e - 1) * NA_NCOFF + o] if e >= 1 else 0.0
            second_v = rpb_ref[(h * NA_NOFF + e) * NA_NCOFF + o] if e < NA_NOFF else 0.0
            acc = jnp.where(col_off == o, jnp.where(second, second_v, first_v), acc)
        o_ref[0, e] = jnp.where(in_win, acc * LOG2E, NEG)


def _na_bias_tables(rpb):
    return pl.pallas_call(
        _na_table_kernel,
        grid=(NA_HEADS,),
        in_specs=[pl.BlockSpec(memory_space=pltpu.SMEM)],
        out_specs=pl.BlockSpec((1, NA_NENT, GRID_W, LANES), lambda h: (h, 0, 0, 0)),
        out_shape=jax.ShapeDtypeStruct((NA_HEADS, NA_NENT, GRID_W, LANES), F32),
        compiler_params=_cparams(1),
        name="na_bias_table",
    )(rpb.astype(F32).reshape(-1))


def _na_key_row0(j):
    return jnp.clip(NA_QROWS * j - NA_WIN_R // 2, 0, NA_ROWS - NA_KROWS)


def _na_kernel(q_ref, k_ref, v_ref, kc_ref, vc_ref, t_ref, o_ref):
    j = pl.program_id(1)
    u0 = _na_key_row0(j)
    start = pl.multiple_of(u0 * GRID_W, GRID_W)
    lo = _lane_lo((1, LANES))
    entry = []
    rowmask = []
    for i in range(NA_QROWS):
        r = NA_QROWS * j + i
        r0 = jnp.clip(r - NA_WIN_R // 2, 0, NA_ROWS - NA_WIN_R)
        entry.append([])
        rowmask.append([])
        for p in range(NA_KROWS // 2):
            a = u0 + 2 * p
            entry[i].append(jnp.clip(a - r + NA_WIN_R, 0, NA_NENT - 1))
            out_a = jnp.logical_or(a < r0, a >= r0 + NA_WIN_R)
            out_b = jnp.logical_or(a + 1 < r0, a + 1 >= r0 + NA_WIN_R)
            rowmask[i].append(jnp.where(lo, jnp.where(out_a, NEG, 0.0), jnp.where(out_b, NEG, 0.0)))
    def head_pair(hp):
        cols = slice(hp * LANES, (hp + 1) * LANES)
        bias_rows = []
        for head in range(2):
            for i in range(NA_QROWS):
                tiles = [t_ref[2 * hp + head, entry[i][p]] + rowmask[i][p] for p in range(NA_KROWS // 2)]
                bias_rows.append(jnp.concatenate(tiles, axis=1))
        bias = jnp.concatenate(bias_rows, axis=0)
        segs = [(k_ref[pl.ds(start, NA_NKEY), cols], v_ref[pl.ds(start, NA_NKEY), cols]),
                (kc_ref[:, cols], vc_ref[:, cols])]
        stages = _pair_attend_stages(q_ref[:, cols], segs, [bias, None])
        yield next(stages)
        o_ref[:, cols] = next(stages).astype(o_ref.dtype)

    for hp0 in range(0, NA_HEADS // 2, NA_PAIRS_IN_FLIGHT):
        pairs = [head_pair(hp0 + i) for i in range(NA_PAIRS_IN_FLIGHT)]
        while pairs:
            pairs = [gen for gen in pairs if next(gen, StopIteration) is not StopIteration]


def _na_attention(q, k, v, table, n_batch):
    ctx0 = n_batch * LAT_BLKS
    n_steps = NA_ROWS // NA_QROWS
    tq = NA_QROWS * GRID_W
    lat = pl.BlockSpec((N_LAT, D), lambda b, j: (b, 0))
    ctx = pl.BlockSpec((N_CTX, D), lambda b, j: (ctx0 + b, 0))
    return pl.pallas_call(
        _na_kernel,
        grid=(n_batch, n_steps),
        in_specs=[
            pl.BlockSpec((tq, D), lambda b, j: (b * n_steps + j, 0)),
            lat, lat, ctx, ctx,
            pl.BlockSpec(table.shape, lambda b, j: (0, 0, 0, 0), pipeline_mode=pl.Buffered(1)),
        ],
        out_specs=pl.BlockSpec((tq, D), lambda b, j: (b * n_steps + j, 0)),
        out_shape=jax.ShapeDtypeStruct((n_batch * N_LAT, D), BF16),
        compiler_params=_cparams(2),
        name="na_attention",
    )(q, k, v, k, v, table)


def _ctx_attn_kernel(q_ref, k_ref, v_ref, o_ref):
    for hp in range(NA_HEADS // 2):
        cols = slice(hp * LANES, (hp + 1) * LANES)
        segs = [(k_ref[:, cols], v_ref[:, cols])]
        o_ref[:, cols] = _pair_attend(q_ref[:, cols], segs, [None]).astype(o_ref.dtype)


def _ctx_attention(q, k, v, n_batch):
    ctx0 = n_batch * LAT_BLKS
    spec = pl.BlockSpec((N_CTX, D), lambda b: (ctx0 + b, 0))
    return pl.pallas_call(
        _ctx_attn_kernel,
        grid=(n_batch,),
        in_specs=[spec, spec, spec],
        out_specs=pl.BlockSpec((N_CTX, D), lambda b: (b, 0)),
        out_shape=jax.ShapeDtypeStruct((n_batch * N_CTX, D), BF16),
        compiler_params=_cparams(1),
        name="na_ctx_attention",
    )(q, k, v)


def _na_mixer(h, mod3, g, w_kvq, rpb, n_batch):
    segs = [(0, D, 1.0, BF16), (D, 2 * D, 1.0, BF16), (2 * D, 3 * D, NA_HD ** -0.5 * LOG2E, BF16)]
    k, v, q = _proj(h, mod3, g, w_kvq.astype(BF16), segs, n_batch)
    return _na_attention(q, k, v, _na_bias_tables(rpb), n_batch), _ctx_attention(q, k, v, n_batch)


DA_TQ = 1024
DA_W = 2 * DA_HD


def _rope_tables():
    t = np.arange(N_LAT)
    row = (t // GRID_W).astype(np.float64)
    col = (t % GRID_W).astype(np.float64)
    per_axis = DA_HD // 2
    freqs = ROPE_BASE ** (-np.arange(0, per_axis, 2, dtype=np.float64) / per_axis)
    ar = row[:, None] * freqs
    ac = col[:, None] * freqs
    ang = np.concatenate([ar, ar, ac, ac], axis=-1)
    cos = np.tile(np.cos(ang), (1, 2))
    sin = np.tile(np.sin(ang), (1, 2))
    quarter = (np.arange(DA_W) % DA_HD) // (DA_HD // 4)
    even = (quarter % 2 == 0)[None, :]
    sin_a = np.where(even, -sin, 0.0)
    sin_b = np.where(even, 0.0, sin)
    return tuple(jnp.asarray(a, F32) for a in (cos, sin_a, sin_b))


def _rope(x, cos, sin_a, sin_b):
    q16 = DA_HD // 4
    return x * cos + pltpu.roll(x, DA_W - q16, 1) * sin_a + pltpu.roll(x, q16, 1) * sin_b


def _da_lambda(lp, lam_init):
    a = jnp.sum(lp[0:1] * lp[1:2], axis=-1, keepdims=True)
    b = jnp.sum(lp[2:3] * lp[3:4], axis=-1, keepdims=True)
    return jnp.exp(a) - jnp.exp(b) + lam_init


DA_KCHUNK = 512


def _diff_attend(q, segs, lam, ng, lam_init):
    lo = _lane_lo(q.shape)
    qm = [_half_masked(q, lo, 0), _half_masked(q, lo, 1)]
    chunks = [(k, v, c0, min(c0 + DA_KCHUNK, k.shape[0])) for k, v in segs for c0 in range(0, k.shape[0], DA_KCHUNK)]
    s = [[_dot_nt(qm[j], k[c0:c1, :]) for j in range(2)] for k, _, c0, c1 in chunks]
    m = [None, None]
    for sc in s:
        for j in range(2):
            mc = sc[j].max(axis=-1, keepdims=True)
            m[j] = mc if m[j] is None else jnp.maximum(m[j], mc)
    r = [None, None]
    for sc, (_, v, c0, c1) in zip(s, chunks):
        vc = v[c0:c1, :]
        vx = jnp.concatenate([vc, jnp.ones_like(vc)], axis=1)
        for j in range(2):
            pv = _dot(jnp.exp2(sc[j] - m[j]).astype(BF16), vx)
            r[j] = pv if r[j] is None else r[j] + pv
    o = r[0][:, :DA_W] / r[0][:, DA_W:] - lam * (r[1][:, :DA_W] / r[1][:, DA_W:])
    return o * lax.rsqrt(jnp.mean(o * o, axis=-1, keepdims=True) + EPS) * ng * (1.0 - lam_init)


def _da_kernel(lam_ref, q_ref, k_ref, v_ref, kc_ref, vc_ref, ng_ref, o_ref, *, lam_init):
    lam = _da_lambda(lam_ref[...], lam_init)
    segs = [(k_ref, v_ref), (kc_ref, vc_ref)]
    o_ref[...] = _diff_attend(q_ref[...], segs, lam, ng_ref[...], lam_init).astype(o_ref.dtype)


def _da_attention(q, k, v, lam_p, norm_g, lam_init, n_batch):
    nqb = N_LAT // DA_TQ
    ctx0 = n_batch * LAT_BLKS
    lat = pl.BlockSpec((N_LAT, DA_W), lambda b, h, i: (b, h))
    ctx = pl.BlockSpec((N_CTX, DA_W), lambda b, h, i: (ctx0 + b, h))
    return pl.pallas_call(
        functools.partial(_da_kernel, lam_init=lam_init),
        grid=(n_batch, DA_HEADS, nqb),
        in_specs=[
            pl.BlockSpec((4, DA_HD), lambda b, h, i: (0, 0)),
            pl.BlockSpec((DA_TQ, DA_W), lambda b, h, i: (b * nqb + i, h)),
            lat, lat, ctx, ctx,
            pl.BlockSpec((1, DA_W), lambda b, h, i: (0, h)),
        ],
        out_specs=pl.BlockSpec((DA_TQ, DA_W), lambda b, h, i: (b * nqb + i, h)),
        out_shape=jax.ShapeDtypeStruct((n_batch * N_LAT, D), BF16),
        compiler_params=_cparams(3),
        name="diff_attention",
    )(lam_p, q, k, v, k, v, norm_g.reshape(1, D))


def _da_ctx_kernel(lam_ref, q_ref, k_ref, v_ref, ng_ref, o_ref, *, lam_init):
    lam = _da_lambda(lam_ref[...], lam_init)
    for h in range(DA_HEADS):
        cols = slice(h * DA_W, (h + 1) * DA_W)
        o = _diff_attend(q_ref[:, cols], [(k_ref[:, cols], v_ref[:, cols])], lam, ng_ref[:, cols], lam_init)
        o_ref[:, cols] = o.astype(o_ref.dtype)


def _da_ctx_attention(q, k, v, lam_p, norm_g, lam_init, n_batch):
    ctx0 = n_batch * LAT_BLKS
    spec = pl.BlockSpec((N_CTX, D), lambda b: (ctx0 + b, 0))
    return pl.pallas_call(
        functools.partial(_da_ctx_kernel, lam_init=lam_init),
        grid=(n_batch,),
        in_specs=[pl.BlockSpec((4, DA_HD), lambda b: (0, 0)), spec, spec, spec, pl.BlockSpec((1, D), lambda b: (0, 0))],
        out_specs=pl.BlockSpec((N_CTX, D), lambda b: (b, 0)),
        out_shape=jax.ShapeDtypeStruct((n_batch * N_CTX, D), BF16),
        compiler_params=_cparams(1),
        name="diff_ctx_attention",
    )(lam_p, q, k, v, norm_g.reshape(1, D))


def _da_mixer(h, mod3, g, w_kvq, lam_p, norm_g, layer_idx, n_batch):
    lam_init = 0.8 - 0.6 * float(np.exp(-0.3 * layer_idx))
    segs = [(0, D, 1.0, BF16, True), (D, 2 * D, 1.0, BF16), (2 * D, 3 * D, DA_HD ** -0.5 * LOG2E, BF16, True)]
    k, v, q = _proj(h, mod3, g, w_kvq.astype(BF16), segs, n_batch, rope_tabs=_rope_tables())
    lam_p = lam_p.astype(F32)
    return (_da_attention(q, k, v, lam_p, norm_g, lam_init, n_batch),
            _da_ctx_attention(q, k, v, lam_p, norm_g, lam_init, n_batch))


SCAN_STEPS = 1 + LAT_BLKS


def _scan_row_blk(direction, n_batch):
    ctx0 = n_batch * LAT_BLKS

    def blk(b, s):
        lat = b * LAT_BLKS + (s - 1 if direction == 0 else LAT_BLKS - s)
        return jnp.where(s == 0, ctx0 + b, lat)
    return blk


def _scan_spec(width, direction, n_batch):
    blk = _scan_row_blk(direction, n_batch)
    return pl.BlockSpec((ROW_BLK, width), lambda b, h, s: (blk(b, s), h))


def _scan_spec_all(width, direction, n_batch):
    blk = _scan_row_blk(direction, n_batch)
    return pl.BlockSpec((ROW_BLK, width), lambda b, h, s: (blk(b, s), 0))


def _scan_tri():
    u = np.arange(ROW_BLK)
    tt, uu = np.meshgrid(u, u, indexing="ij")
    return jnp.asarray(np.stack([uu <= tt, uu >= tt]).astype(np.float32), BF16)


def _split3(x):
    hi = x.astype(BF16)
    r1 = x - hi.astype(F32)
    mid = r1.astype(BF16)
    lo = (r1 - mid.astype(F32)).astype(BF16)
    return hi, mid, lo


def _scan_out_specs(width):
    return [pl.BlockSpec((N_LAT, width), lambda b, h, s: (b, h)), pl.BlockSpec((N_CTX, width), lambda b, h, s: (b, h))]


def _scan_write_sum(step, out_fwd, out_bwd, lat_ref, ctx_ref):
    @pl.when(step == 0)
    def _():
        ctx_ref[...] = out_fwd + out_bwd

    rows_f = pl.ds(pl.multiple_of((step - 1) * ROW_BLK, ROW_BLK), ROW_BLK)
    rows_b = pl.ds(pl.multiple_of((LAT_BLKS - step) * ROW_BLK, ROW_BLK), ROW_BLK)

    @pl.when(jnp.logical_and(step >= 1, step <= LAT_BLKS // 2))
    def _():
        lat_ref[rows_f, :] = out_fwd
        lat_ref[rows_b, :] = out_bwd

    @pl.when(step > LAT_BLKS // 2)
    def _():
        lat_ref[rows_f, :] += out_fwd
        lat_ref[rows_b, :] += out_bwd


def _conv_silu(x_ref, prev_ref, next_ref, w_ref, first, last, scale):
    x = x_ref[...]
    rid = lax.broadcasted_iota(jnp.int32, x.shape, 0)
    prev_row = jnp.where(first, 0.0, prev_ref[7:8, :])
    next_row = jnp.where(last, 0.0, next_ref[0:1, :])
    x_prev = jnp.where(rid == 0, prev_row, pltpu.roll(x, 1, 0))
    x_next = jnp.where(rid == ROW_BLK - 1, next_row, pltpu.roll(x, ROW_BLK - 1, 0))
    y = w_ref[0:1, :] * x_prev + w_ref[1:2, :] * x + w_ref[2:3, :] * x_next
    return (_silu(y) * scale).astype(BF16)


def _ml_conv_kernel(xk_ref, xkp_ref, xkn_ref, xq_ref, xqp_ref, xqn_ref, wk_ref, wq_ref, ok_ref, oq_ref, *, n_lat_blks):
    i = pl.program_id(0)
    is_ctx = i >= n_lat_blks
    first = jnp.logical_or(is_ctx, i % LAT_BLKS == 0)
    last = jnp.logical_or(is_ctx, i % LAT_BLKS == LAT_BLKS - 1)
    ok_ref[...] = _conv_silu(xk_ref, xkp_ref, xkn_ref, wk_ref, first, last, ML_HD ** -0.5)
    oq_ref[...] = _conv_silu(xq_ref, xqp_ref, xqn_ref, wq_ref, first, last, 1.0)


def _ml_conv(pk, pq, conv_w, n_batch):
    n_rows = pk.shape[0]
    n_blk = n_rows // ROW_BLK
    sub = ROW_BLK // 8
    main = pl.BlockSpec((ROW_BLK, D), lambda i: (i, 0))
    prev = pl.BlockSpec((8, D), lambda i: (jnp.maximum(i * sub - 1, 0), 0))
    nxt = pl.BlockSpec((8, D), lambda i: (jnp.minimum((i + 1) * sub, n_blk * sub - 1), 0))
    return pl.pallas_call(
        functools.partial(_ml_conv_kernel, n_lat_blks=n_batch * LAT_BLKS),
        grid=(n_blk,),
        in_specs=[main, prev, nxt, main, prev, nxt,
                  pl.BlockSpec((3, D), lambda i: (0, 0)), pl.BlockSpec((3, D), lambda i: (0, 1))],
        out_specs=[main, main],
        out_shape=[jax.ShapeDtypeStruct((n_rows, D), BF16)] * 2,
        compiler_params=_cparams(1),
        name="mlstm_conv",
    )(pk, pk, pk, pq, pq, pq, conv_w, conv_w)


def _row_sum(x):
    return jnp.sum(x[:, :LANES] + x[:, LANES:], axis=1, keepdims=True)


def _pick_lane(x, idx):
    lane = lax.broadcasted_iota(jnp.int32, x.shape, 1)
    return jnp.sum(jnp.where(lane == idx, x, 0.0), axis=1, keepdims=True)


def _pick_sublane(x, idx):
    sub = lax.broadcasted_iota(jnp.int32, x.shape, 0)
    return jnp.sum(jnp.where(sub == idx, x, 0.0), axis=0, keepdims=True)


def _ml_scan_kernel(gb_ref, g0_ref, q0_ref, k0_ref, v0_ref, g1_ref, q1_ref, k1_ref, v1_ref,
                    lat_ref, ctx_ref, c_ref, n_ref, m_ref):
    head0 = pl.program_id(1) * ML_HPS

    @pl.when(pl.program_id(2) == 0)
    def _():
        c_ref[...] = jnp.zeros_like(c_ref)
        n_ref[...] = jnp.zeros_like(n_ref)
        m_ref[...] = jnp.zeros_like(m_ref)

    r = lax.broadcasted_iota(jnp.int32, (ROW_BLK, ROW_BLK), 0)
    c = lax.broadcasted_iota(jnp.int32, (ROW_BLK, ROW_BLK), 1)
    outs = [[None] * ML_HPS, [None] * ML_HPS]

    def direction(d, hh, g_ref, q_ref, k_ref, v_ref):
        valid = (r >= c) if d == 0 else (r <= c)
        other = (r <= c) if d == 0 else (r >= c)
        g = g_ref[...] + gb_ref[...]
        gt = g.T
        ii = d * 2 * ML_HEADS + head0 + hh
        fi = ii + ML_HEADS
        si = d * ML_HPS + hh
        cols = slice(hh * ML_HD, (hh + 1) * ML_HD)
        m_prev = m_ref[si][0:1, 0:1]
        n_prev = n_ref[si][0:1, :]
        c_prev = c_ref[si]
        qc = q_ref[:, cols]
        kc = k_ref[:, cols]
        vc = v_ref[:, cols]
        qk = _dot_nt(qc, kc)
        q_c = _dot(qc, c_prev.astype(BF16))
        q_n = _row_sum(qc.astype(F32) * n_prev)
        yield
        i_col = _pick_lane(g, ii)
        f_col = _log_sigmoid(_pick_lane(g, fi))
        i_row = _pick_sublane(gt, ii)
        f_row = _log_sigmoid(_pick_sublane(gt, fi))
        yield
        b_col = _row_sum(jnp.where(valid, f_row, 0.0))
        b_row = jnp.sum(jnp.where(other, f_col, 0.0), axis=0, keepdims=True)
        total = jnp.sum(f_col, axis=0, keepdims=True)
        yield
        dmat = jnp.where(valid, b_col - b_row + i_row, NEG)
        inter = b_col + m_prev
        m_t = jnp.maximum(inter, jnp.maximum(dmat[:, :LANES], dmat[:, LANES:]).max(axis=1, keepdims=True))
        w_inter = jnp.exp(inter - m_t)
        g_col = total - b_col + i_col
        m_new = jnp.maximum(total + m_prev, g_col.max(axis=0, keepdims=True))
        w_old = jnp.exp(total + m_prev - m_new)
        kw = kc.astype(F32) * jnp.exp(g_col - m_new)
        upd = _dot_tn(kw.astype(BF16), vc)
        yield
        s = qk * jnp.exp(dmat - m_t)
        num = w_inter * q_c + _dot(s.astype(BF16), vc)
        den = w_inter * q_n + _row_sum(s)
        outs[d][hh] = num / jnp.maximum(jnp.abs(den), jnp.exp(-m_t))
        yield
        c_ref[si] = w_old * c_prev + upd
        n_ref[si] = jnp.broadcast_to(w_old * n_prev + kw.sum(axis=0, keepdims=True), n_ref.shape[1:])
        m_ref[si] = jnp.broadcast_to(m_new, m_ref.shape[1:])

    stages = []
    for hh in range(ML_HPS):
        stages += [direction(0, hh, g0_ref, q0_ref, k0_ref, v0_ref), direction(1, hh, g1_ref, q1_ref, k1_ref, v1_ref)]
    while stages:
        stages = [gen for gen in stages if next(gen, StopIteration) is not StopIteration]
    _scan_write_sum(pl.program_id(2), jnp.concatenate(outs[0], axis=1), jnp.concatenate(outs[1], axis=1),
                    lat_ref, ctx_ref)


def _ml_scan(q, k, v, gates, gate_b, n_batch):
    width = ML_HPS * ML_HD
    in_specs = [pl.BlockSpec((1, LANES), lambda b, h, s: (0, 0))]
    for d in range(2):
        in_specs += [_scan_spec_all(LANES, d, n_batch)] + [_scan_spec(width, d, n_batch)] * 3
    return pl.pallas_call(
        _ml_scan_kernel,
        grid=(n_batch, ML_HEADS // ML_HPS, SCAN_STEPS),
        in_specs=in_specs,
        out_specs=_scan_out_specs(width),
        out_shape=[jax.ShapeDtypeStruct((n_batch * N_LAT, D), F32), jax.ShapeDtypeStruct((n_batch * N_CTX, D), F32)],
        scratch_shapes=[pltpu.VMEM((2 * ML_HPS, ML_HD, ML_HD), F32), pltpu.VMEM((2 * ML_HPS, 8, ML_HD), F32),
                        pltpu.VMEM((2 * ML_HPS, 8, LANES), F32)],
        compiler_params=_cparams(3),
        name="mlstm_scan",
    )(gate_b, gates, q, k, v, gates, q, k, v)


def _ml_mixer(h, mod3, g, w_in, gate_b, conv_w, n_batch):
    n_g = 4 * ML_HEADS
    w_in = w_in.astype(BF16)
    wk, wv, wg, wq, wo = (w_in[:, :D], w_in[:, D:2 * D], w_in[:, 2 * D:2 * D + n_g],
                          w_in[:, 2 * D + n_g:3 * D + n_g], w_in[:, 3 * D + n_g:])
    w = jnp.concatenate([wk, wv, wq, wo, jnp.pad(wg, ((0, 0), (0, LANES - n_g)))], axis=1)
    segs = [(0, D, 1.0, F32), (D, 2 * D, 1.0, BF16), (2 * D, 3 * D, 1.0, F32), (3 * D, 4 * D, 1.0, F32),
            (4 * D, 4 * D + LANES, 1.0, F32)]
    pk, v, pq, og, gates = _proj(h, mod3, g, w, segs, n_batch)
    k, q = _ml_conv(pk, pq, conv_w.astype(F32), n_batch)
    gb = jnp.pad(gate_b.astype(F32), (0, LANES - n_g)).reshape(1, LANES)
    h_lat, h_ctx = _ml_scan(q, k, v, gates, gb, n_batch)
    return h_lat, h_ctx, og


GLA_LEVELS = (256, 128, 64, 32, 16, 8, 4, 2)
GLA_BCAST_LEVELS = 6
GLA_DIAG = len(GLA_LEVELS)
GLA_HPS = 4


def _gla_level_ids():
    u = np.arange(ROW_BLK // 2)
    tt, ss = np.meshgrid(u, u, indexing="ij")
    hb = np.floor(np.log2(np.maximum(tt ^ ss, 1))).astype(np.int64)
    level = len(GLA_LEVELS) - 1 - hb
    lid = np.stack([np.where(tt == ss, GLA_DIAG, np.where(tt > ss, level, -1)),
                    np.where(tt == ss, GLA_DIAG, np.where(tt < ss, level, -1))]).astype(np.int32)
    return jnp.asarray(np.concatenate([lid, lid], axis=1))


def _gla_level_exponents(level, d, bc_ref, la_ref, row):
    n = GLA_LEVELS[level]
    half = n // 2
    if level < GLA_BCAST_LEVELS:
        parts = []
        for j in range(ROW_BLK // n):
            mid = j * n + (half - 1 if d == 0 else half)
            parts.append(jnp.broadcast_to(bc_ref[mid:mid + 1, :], (n, GLA_DK)))
        e = -jnp.abs(bc_ref[...] - (parts[0] if len(parts) == 1 else jnp.concatenate(parts, axis=0)))
        return e, e
    o = row & (n - 1)
    la = la_ref[...]
    if n == 4:
        la_prev = pltpu.roll(la, 1, 0)
        la_next = pltpu.roll(la, ROW_BLK - 1, 0)
        if d == 0:
            return (jnp.where(o == 2, la, jnp.where(o == 3, la + la_prev, 0.0)), jnp.where(o == 0, la_next, 0.0))
        return (jnp.where(o == 0, la + la_next, jnp.where(o == 1, la, 0.0)), jnp.where(o == 3, la_prev, 0.0))
    return jnp.where(o == (1 if d == 0 else 0), la, 0.0), None


def _gla_kernel(tri_ref, lid_ref, wup_ref, bg_ref, q0_ref, k0_ref, v0_ref, l0_ref, q1_ref, k1_ref, v1_ref, l1_ref,
                lat_ref, ctx_ref, st_ref, bc_ref, att_ref, la_ref):
    @pl.when(pl.program_id(2) == 0)
    def _():
        st_ref[...] = jnp.zeros_like(st_ref)

    row = lax.broadcasted_iota(jnp.int32, (ROW_BLK, GLA_DK), 0)
    outs = [[None] * GLA_HPS, [None] * GLA_HPS]

    def direction(d, hh, q_ref, k_ref, v_ref, low_ref):
        si = d * GLA_HPS + hh
        kc = slice(hh * GLA_DK, (hh + 1) * GLA_DK)
        vc = slice(hh * GLA_DV, (hh + 1) * GLA_DV)
        pre = _dot(low_ref[...].astype(BF16), wup_ref[d][:, kc]) + bg_ref[d][:, kc]
        la = _log_sigmoid(pre) * (LOG2E / GLA_TAU)
        hi, mid, lo = _split3(la)
        tri = tri_ref[d]
        bc_ref[si] = _dot(tri, hi) + _dot(tri, mid) + _dot(tri, lo)
        la_ref[si] = la
        half = ROW_BLK // 2

        def diag_blocks(a, b):
            return jnp.concatenate([_dot_nt(a[:half], b[:half]), _dot_nt(a[half:], b[half:])], axis=0)

        att_ref[si] = jnp.where(lid_ref[d] == GLA_DIAG,
                                diag_blocks(q_ref[:, kc].astype(BF16), k_ref[:, kc].astype(BF16)), 0.0)
        yield
        eq, _ = _gla_level_exponents(0, d, bc_ref.at[si], la_ref.at[si], row)
        q_rows, k_rows = (slice(half, None), slice(None, half)) if d == 0 else (slice(None, half), slice(half, None))
        f = jnp.exp2(eq)
        off = _dot_nt((q_ref[q_rows, kc] * f[q_rows]).astype(BF16), (k_ref[k_rows, kc] * f[k_rows]).astype(BF16))
        yield
        for level in range(1, len(GLA_LEVELS)):
            eq, ek = _gla_level_exponents(level, d, bc_ref.at[si], la_ref.at[si], row)
            fq = jnp.exp2(eq)
            ql = (q_ref[:, kc] * fq).astype(BF16)
            k = k_ref[:, kc]
            kl = k.astype(BF16) if ek is None else (k * (fq if ek is eq else jnp.exp2(ek))).astype(BF16)
            att_ref[si] = jnp.where(lid_ref[d] == level, diag_blocks(ql, kl), att_ref[si])
            yield
        st = st_ref[si]
        bc = bc_ref[si]
        k = k_ref[:, kc]
        v = v_ref[:, vc]
        qd = (q_ref[:, kc] * jnp.exp2(bc)).astype(BF16)
        diag = att_ref[si]
        if d == 0:
            o_top = _dot(diag[:half].astype(BF16), v[:half])
            o_bot = _dot(jnp.concatenate([off, diag[half:]], axis=1).astype(BF16), v)
        else:
            o_top = _dot(jnp.concatenate([diag[:half], off], axis=1).astype(BF16), v)
            o_bot = _dot(diag[half:].astype(BF16), v[half:])
        outs[d][hh] = jnp.concatenate([o_top, o_bot], axis=0) + _dot_nt(qd, st.astype(BF16))
        b_end = bc[ROW_BLK - 1:ROW_BLK] if d == 0 else bc[0:1]
        kd = (k * jnp.exp2(b_end - bc)).astype(BF16)
        st_ref[si] = st * jnp.exp2(b_end) + _dot_tn(v, kd)

    stages = []
    for hh in range(GLA_HPS):
        stages += [direction(0, hh, q0_ref, k0_ref, v0_ref, l0_ref), direction(1, hh, q1_ref, k1_ref, v1_ref, l1_ref)]
    while stages:
        stages = [gen for gen in stages if next(gen, StopIteration) is not StopIteration]
    _scan_write_sum(pl.program_id(2), jnp.concatenate(outs[0], axis=1), jnp.concatenate(outs[1], axis=1),
                    lat_ref, ctx_ref)


def _gla_scan(q, k, v, low, wup, bg, n_batch):
    tri, lid = _scan_tri(), _gla_level_ids()
    const3 = lambda b, h, s: (0, 0, 0)
    in_specs = [
        pl.BlockSpec(tri.shape, const3),
        pl.BlockSpec(lid.shape, const3),
        pl.BlockSpec((2, LANES, GLA_HPS * GLA_DK), lambda b, h, s: (0, 0, h)),
        pl.BlockSpec((2, 1, GLA_HPS * GLA_DK), lambda b, h, s: (0, 0, h)),
    ]
    for d in range(2):
        in_specs += [_scan_spec(GLA_HPS * GLA_DK, d, n_batch), _scan_spec(GLA_HPS * GLA_DK, d, n_batch),
                     _scan_spec(GLA_HPS * GLA_DV, d, n_batch), _scan_spec_all(LANES, d, n_batch)]
    n_slots = 2 * GLA_HPS
    return pl.pallas_call(
        _gla_kernel,
        grid=(n_batch, GLA_HEADS // GLA_HPS, SCAN_STEPS),
        in_specs=in_specs,
        out_specs=_scan_out_specs(GLA_HPS * GLA_DV),
        out_shape=[jax.ShapeDtypeStruct((n_batch * N_LAT, D), F32), jax.ShapeDtypeStruct((n_batch * N_CTX, D), F32)],
        scratch_shapes=[pltpu.VMEM((n_slots, GLA_DV, GLA_DK), F32), pltpu.VMEM((n_slots, ROW_BLK, GLA_DK), F32),
                        pltpu.VMEM((n_slots, ROW_BLK, ROW_BLK // 2), F32), pltpu.VMEM((n_slots, ROW_BLK, GLA_DK), F32)],
        compiler_params=_cparams(3),
        name="gla_scan",
    )(tri, lid, wup, bg, q, k, v, low, q, k, v, low)


def _gla_mixer(h, mod3, g, w_in, w_gate_up, b_gate, n_batch):
    dk_t = GLA_HEADS * GLA_DK
    n_low = 2 * GLA_RANK
    w_in = w_in.astype(BF16)
    wk, wv, wl, wq, wr = (w_in[:, :dk_t], w_in[:, dk_t:dk_t + D], w_in[:, dk_t + D:dk_t + D + n_low],
                          w_in[:, dk_t + D + n_low:2 * dk_t + D + n_low], w_in[:, 2 * dk_t + D + n_low:])
    w = jnp.concatenate([wk, wq, wv, wr, jnp.pad(wl, ((0, 0), (0, LANES - n_low)))], axis=1)
    c = [0, dk_t, 2 * dk_t, 2 * dk_t + D, 2 * dk_t + 2 * D, 2 * dk_t + 2 * D + LANES]
    segs = [(c[0], c[1], 1.0, F32), (c[1], c[2], GLA_DK ** -0.5, F32), (c[2], c[3], 1.0, BF16),
            (c[3], c[4], 1.0, F32), (c[4], c[5], 1.0, F32)]
    k, q, v, rg, low = _proj(h, mod3, g, w, segs, n_batch)
    wup = jnp.zeros((2, LANES, dk_t), F32)
    for r in range(2):
        wup = wup.at[r, r * GLA_RANK:(r + 1) * GLA_RANK].set(w_gate_up[r].astype(F32))
    o_lat, o_ctx = _gla_scan(q, k, v, low, wup.astype(BF16), b_gate.astype(F32).reshape(2, 1, dk_t), n_batch)
    return o_lat, o_ctx, rg


def kernel(x, c, ctx, c_ctx, ada_w, ada_b, norm_g, ffn_w13, ffn_w2, final_g, na_w_kvq, na_rpb, na_w_o, ml_w_in, ml_gate_b, ml_conv_w, ml_norm_g, ml_w_o, da_w_kvq, da_lam, da_norm_g, da_w_o, gla_w_in, gla_w_gate_up, gla_b_gate, gla_norm_g, gla_w_o):
    nb = x.shape[0]
    assert x.shape[1:] == (N_LAT, D) and ctx.shape[1:] == (N_CTX, D) and nb < MOD_ROWS
    n_lat = nb * N_LAT
    n_tok = n_lat + nb * N_CTX

    s = jnp.zeros((MOD_ROWS, D), F32).at[:nb].set(c).at[nb].set(c_ctx)
    modtab = _mod_tables(s, ada_w, ada_b)
    h = (x.reshape(n_lat, D), ctx.reshape(nb * N_CTX, D))
    w13 = ffn_w13.astype(BF16)
    w2 = ffn_w2.astype(BF16)

    for i in range(DEPTH):
        kind, j = i % 4, i // 4
        last = i == DEPTH - 1
        mod3 = modtab[i].reshape(MOD_ROWS * N_MOD, 1, D)
        h = _ffn(h, mod3, norm_g[i, 0], w13, w2, (i, 0), final_g, 0, n_tok, nb)
        n_out = n_lat if last else n_tok
        if kind == 0:
            ins = _na_mixer(h, mod3, norm_g[i, 1], na_w_kvq[j], na_rpb[j], nb)
            mixer, w_o, mixer_g = "plain", na_w_o[j], None
        elif kind == 1:
            ins = _ml_mixer(h, mod3, norm_g[i, 1], ml_w_in[j], ml_gate_b[j], ml_conv_w[j], nb)
            mixer, w_o, mixer_g = "ml", ml_w_o[j], ml_norm_g[j]
        elif kind == 2:
            ins = _da_mixer(h, mod3, norm_g[i, 1], da_w_kvq[j], da_lam[j], da_norm_g[j], i, nb)
            mixer, w_o, mixer_g = "plain", da_w_o[j], None
        else:
            ins = _gla_mixer(h, mod3, norm_g[i, 1], gla_w_in[j], gla_w_gate_up[j], gla_b_gate[j], nb)
            mixer, w_o, mixer_g = "gla", gla_w_o[j], gla_norm_g[j]
        h = _ffn(h, mod3, norm_g[i, 2], w13, w2, (i, 1), final_g, 6, n_out, nb, final=last, mixer=mixer, mixer_ins=ins, w_o=w_o.astype(BF16), mixer_g=mixer_g)
    return h.reshape(nb, N_LAT, D)
```
